```python
import jax, jax.numpy as jnp
from jax import lax
import numpy as np

D_MODEL = 1024
BATCH = 8
SEQ = 2048
DEPTH = 1

HG_HEADS = 4
HG_KEY_DIM = 128
HG_VAL_DIM = 128
HG_KEY_WIDTH = HG_HEADS * HG_KEY_DIM
HG_VAL_WIDTH = HG_HEADS * HG_VAL_DIM
HG_CHUNK = 64
ATT_GROUPS = ((128, 1), (512, 4), (2048, 16))
N_ATT_GROUPS = 3
ATT_HEADS = 8
ATT_HEAD_DIM = 64
ATT_WIDTH = ATT_HEADS * ATT_HEAD_DIM
ATT_BLOCK = 128
ALIBI_MAX = 8.0
D_FF = 2816
IN_SIZES = (HG_KEY_WIDTH, HG_KEY_WIDTH, HG_VAL_WIDTH, HG_VAL_WIDTH, N_ATT_GROUPS * 3 * ATT_WIDTH, D_MODEL, D_MODEL)
IN_COLS = sum(IN_SIZES)
EPS = 1e-6
NEG_INF = -1e30

kernel_name = "hybrid_hgrn2_dilated_alibi_macaron"


def _rmsnorm(x, gain):
    xf = x.astype(jnp.float32)
    xf = xf * lax.rsqrt(jnp.mean(xf * xf, axis=-1, keepdims=True) + EPS)
    return xf.astype(x.dtype) * gain


def _swiglu(x, w_gate_up, w_down):
    a, b = jnp.split(x @ w_gate_up, 2, axis=-1)
    return (jax.nn.silu(a) * b) @ w_down


def _hgrn2(q, f_pre, i, og, lower_bound, out_gain):
    B, S, _ = q.shape
    dt = q.dtype
    H, K, V, C = HG_HEADS, HG_KEY_DIM, HG_VAL_DIM, HG_CHUNK
    lb = lower_bound.reshape(H, K)
    f = lb + (1.0 - lb) * jax.nn.sigmoid(f_pre.astype(jnp.float32).reshape(B, S, H, K))
    log_f = jnp.log(f)
    k = 1.0 - f
    qf = q.astype(jnp.float32).reshape(B, S, H, K)
    v = i.astype(jnp.float32).reshape(B, S, H, V)
    Sp = -(-S // C) * C
    pad = Sp - S

    def to_chunks(a):
        a = jnp.pad(a, ((0, 0), (0, pad), (0, 0), (0, 0)))
        return a.reshape(B, Sp // C, C, H, a.shape[-1]).transpose(1, 0, 3, 2, 4)

    qc, kc, vc, gc = to_chunks(qf), to_chunks(k), to_chunks(v), to_chunks(log_f)
    Gc = jnp.cumsum(gc, axis=3)
    causal = jnp.tril(jnp.ones((C, C), dtype=bool))[:, :, None]

    def step(state, inp):
        q_, k_, v_, G = inp
        diff = G[:, :, :, None, :] - G[:, :, None, :, :]
        decay = jnp.where(causal, jnp.exp(jnp.where(causal, diff, 0.0)), 0.0)
        scores = jnp.einsum('bhtk,bhsk,bhtsk->bhts', q_, k_, decay)
        o = (jnp.einsum('bhts,bhsv->bhtv', scores, v_)
             + jnp.einsum('bhtk,bhkv->bhtv', q_ * jnp.exp(G), state))
        G_last = G[:, :, -1:, :]
        new_state = (jnp.exp(G_last[:, :, 0, :])[..., None] * state
                     + jnp.einsum('bhsk,bhsv->bhkv', k_ * jnp.exp(G_last - G), v_))
        return new_state, o

    state0 = jnp.zeros((B, H, K, V), jnp.float32)
    _, oc = lax.scan(step, state0, (qc, kc, vc, Gc))
    o = oc.transpose(1, 0, 3, 2, 4).reshape(B, Sp, H, V)[:, :S]
    o = o * lax.rsqrt(jnp.mean(o * o, axis=-1, keepdims=True) + EPS)
    o = o.reshape(B, S, H * V).astype(dt) * out_gain
    return o * jax.nn.silu(og)


def _dilated_group(q, k, v, window, dilation, slopes):
    B, S, H, E = q.shape
    BLK = ATT_BLOCK
    span = window // dilation
    unit = dilation * BLK
    Sp = -(-S // unit) * unit
    L = Sp // dilation
    nb = L // BLK

    def to_blocks(a):
        a = jnp.pad(a, ((0, 0), (0, Sp - S), (0, 0), (0, 0)))
        return a.reshape(B, nb, BLK, dilation, H, E)

    def band(a):
        prev = jnp.pad(a, ((0, 0), (1, 0), (0, 0), (0, 0), (0, 0), (0, 0)))[:, :-1]
        return jnp.concatenate([prev, a], axis=2)

    qb = to_blocks(q)
    kband, vband = band(to_blocks(k)), band(to_blocks(v))
    scores = jnp.einsum('bnqrhe,bnkrhe->bnrhqk', qb, kband).astype(jnp.float32) * (E ** -0.5)
    qi = jnp.arange(BLK)[:, None]
    kj = jnp.arange(2 * BLK)[None, :]
    delta = qi + BLK - kj
    blk_idx = jnp.arange(nb)[:, None, None]
    valid = (delta >= 0) & (delta <= span) & ((blk_idx > 0) | (kj >= BLK))
    bias = -slopes[:, None, None] * (dilation * delta).astype(jnp.float32)
    scores = scores + bias[None, None, None]
    scores = jnp.where(valid[None, :, None, None], scores, NEG_INF)
    lse = jax.nn.logsumexp(scores, axis=-1)
    p = jnp.exp(scores - lse[..., None]).astype(v.dtype)
    o = jnp.einsum('bnrhqk,bnkrhe->bnqrhe', p, vband).reshape(B, Sp, H, E)[:, :S]
    lse = lse.transpose(0, 1, 4, 2, 3).reshape(B, Sp, H)[:, :S]
    return o, lse


def _mixer(u, w_in, lower_bound, hg_out_norm, w_branch_hg, w_branch_att, w_out):
    B, S, _ = u.shape
    dt = u.dtype
    points = [int(p) for p in np.cumsum(IN_SIZES)[:-1]]
    hg_q, hg_f, hg_i, hg_og, att, gate_hg, gate_att = jnp.split(u @ w_in, points, axis=-1)
    y_hg = _hgrn2(hg_q, hg_f, hg_i, hg_og, lower_bound, hg_out_norm)
    n_heads = N_ATT_GROUPS * ATT_HEADS
    slopes = jnp.exp2(-ALIBI_MAX * jnp.arange(1, n_heads + 1, dtype=jnp.float32) / n_heads)
    qkv = att.reshape(B, S, N_ATT_GROUPS, 3, ATT_HEADS, ATT_HEAD_DIM)
    outs, lses = [], []
    for g, (window, dilation) in enumerate(ATT_GROUPS):
        o, lse = _dilated_group(qkv[:, :, g, 0], qkv[:, :, g, 1], qkv[:, :, g, 2], window, dilation,
                                slopes[g * ATT_HEADS:(g + 1) * ATT_HEADS])
        outs.append(o)
        lses.append(lse)
    wts = jax.nn.softmax(jnp.stack(lses), axis=0).astype(dt)
    y_att = jnp.einsum('gbshe,gbsh->bshe', jnp.stack(outs), wts).reshape(B, S, ATT_WIDTH)
    merged = (jax.nn.sigmoid(gate_hg) * (y_hg @ w_branch_hg)
              + jax.nn.sigmoid(gate_att) * (y_att @ w_branch_att))
    return merged @ w_out


def setup_inputs(seed: int = 0) -> dict:
    key = jax.random.key(seed)
    ks = jax.random.split(key, 16)

    def nrm(k, shape, scale):
        return jax.random.normal(k, shape, jnp.float32) * scale

    return {
        "x": nrm(ks[0], (BATCH, SEQ, D_MODEL), 1.0),
        "ffn1_norm": 1.0 + nrm(ks[1], (DEPTH, D_MODEL), 0.02),
        "ffn1_w_gate_up": nrm(ks[2], (DEPTH, D_MODEL, 2 * D_FF), D_MODEL ** -0.5),
        "ffn1_w_down": nrm(ks[3], (DEPTH, D_FF, D_MODEL), D_FF ** -0.5),
        "mix_norm": 1.0 + nrm(ks[4], (DEPTH, D_MODEL), 0.02),
        "w_in": nrm(ks[5], (DEPTH, D_MODEL, IN_COLS), D_MODEL ** -0.5),
        "hg_lower_bounds": nrm(ks[6], (DEPTH + 1, HG_KEY_WIDTH), 0.1),
        "hg_out_norm": 1.0 + nrm(ks[7], (DEPTH, HG_VAL_WIDTH), 0.02),
        "w_branch_hg": nrm(ks[8], (DEPTH, HG_VAL_WIDTH, D_MODEL), HG_VAL_WIDTH ** -0.5),
        "w_branch_att": nrm(ks[9], (DEPTH, ATT_WIDTH, D_MODEL), ATT_WIDTH ** -0.5),
        "w_out": nrm(ks[10], (DEPTH, D_MODEL, D_MODEL), D_MODEL ** -0.5),
        "ffn2_norm": 1.0 + nrm(ks[11], (DEPTH, D_MODEL), 0.02),
        "ffn2_w_gate_up": nrm(ks[12], (DEPTH, D_MODEL, 2 * D_FF), D_MODEL ** -0.5),
        "ffn2_w_down": nrm(ks[13], (DEPTH, D_FF, D_MODEL), D_FF ** -0.5),
        "final_norm": 1.0 + nrm(ks[14], (D_MODEL,), 0.02),
    }


def reference(x, ffn1_norm, ffn1_w_gate_up, ffn1_w_down, mix_norm, w_in, hg_lower_bounds, hg_out_norm,
              w_branch_hg, w_branch_att, w_out, ffn2_norm, ffn2_w_gate_up, ffn2_w_down, final_norm):
    lower_bounds = jnp.cumsum(jax.nn.softmax(hg_lower_bounds.astype(jnp.float32), axis=0), axis=0)
    h = x
    for l in range(DEPTH):
        h = h + 0.5 * _swiglu(_rmsnorm(h, ffn1_norm[l]), ffn1_w_gate_up[l], ffn1_w_down[l])
        h = h + _mixer(_rmsnorm(h, mix_norm[l]), w_in[l], lower_bounds[l], hg_out_norm[l],
                       w_branch_hg[l], w_branch_att[l], w_out[l])
        h = h + 0.5 * _swiglu(_rmsnorm(h, ffn2_norm[l]), ffn2_w_gate_up[l], ffn2_w_down[l])
    return _rmsnorm(h, final_norm)
```

```python
import functools

import jax
import jax.numpy as jnp
from jax import lax
from jax.experimental import pallas as pl
from jax.experimental.pallas import tpu as pltpu

F32 = jnp.float32
BF16 = jnp.bfloat16

D_MODEL = 1024
D_FF = 2816
HG_HEADS = 4
HG_DIM = 128
HG_WIDTH = HG_HEADS * HG_DIM
ATT_DILATIONS = (1, 4, 16)
ATT_BLOCK = 128
ATT_HEADS = 8
ATT_HEAD_DIM = 64
ATT_WIDTH = ATT_HEADS * ATT_HEAD_DIM
N_GROUPS = 3
ALIBI_MAX = 8.0
EPS = 1e-6
NEG_INF = -1e30

LANES = 128
SUBLANES = 8
VMEM_LIMIT = 56 * 1024 * 1024

ROW_TILE = 512
COL_BLOCK = 512
N_BF16_BLOCKS = 11
N_F32_BLOCKS = 6
HG_CHUNK = 64
HG_STEP = 512


def _sigmoid(x):
    return 1.0 / (1.0 + jnp.exp(-x))


def _rmsnorm(x, gain):
    ms = jnp.mean(x * x, axis=-1, keepdims=True)
    return x * lax.rsqrt(ms + EPS) * gain


def _dot(a, b):
    return jnp.dot(a, b, preferred_element_type=F32)


def _dot_nt(a, b):
    return lax.dot_general(a, b, (((1,), (1,)), ((), ())), preferred_element_type=F32)


def _dot_tn(a, b):
    return lax.dot_general(a, b, (((0,), (0,)), ((), ())), preferred_element_type=F32)


def _swiglu(xn_bf16, wgu_ref, wd_ref):
    half = D_FF // 2
    y = None
    for c in range(2):
        a = _dot(xn_bf16, wgu_ref[:, c * half:(c + 1) * half])
        b = _dot(xn_bf16, wgu_ref[:, D_FF + c * half:D_FF + (c + 1) * half])
        act = (a * _sigmoid(a) * b).astype(BF16)
        part = _dot(act, wd_ref[c * half:(c + 1) * half, :])
        y = part if y is None else y + part
    return y


def _ffn1_kernel(x_ref, g_ref, wgu_ref, wd_ref, o_ref):
    x = x_ref[...]
    xn = _rmsnorm(x, g_ref[...]).astype(BF16)
    o_ref[...] = x + 0.5 * _swiglu(xn, wgu_ref, wd_ref)


def _resident(shape):
    return pl.BlockSpec(shape, lambda *_: (0,) * len(shape), pipeline_mode=pl.Buffered(1))


def _ffn1(x2d, gain, wgu, wd):
    t = x2d.shape[0]
    return pl.pallas_call(
        _ffn1_kernel,
        grid=(t // ROW_TILE,),
        in_specs=[
            pl.BlockSpec((ROW_TILE, D_MODEL), lambda i: (i, 0)),
            _resident((1, D_MODEL)),
            _resident((D_MODEL, 2 * D_FF)),
            _resident((D_FF, D_MODEL)),
        ],
        out_specs=pl.BlockSpec((ROW_TILE, D_MODEL), lambda i: (i, 0)),
        out_shape=jax.ShapeDtypeStruct((t, D_MODEL), F32),
        compiler_params=pltpu.CompilerParams(
            dimension_semantics=("parallel",), vmem_limit_bytes=VMEM_LIMIT),
        name="ffn1",
    )(x2d, gain, wgu, wd)


def _proj_kernel(h_ref, g_ref, w_ref, oa_ref, ob_ref, slab_ref, u_ref, *, seq):
    j = pl.program_id(1)
    n_slabs = D_MODEL // LANES

    @pl.when(j == 0)
    def _normalise():
        rc = 256
        for c in range(seq // rc):
            rows = slice(c * rc, (c + 1) * rc)
            u = _rmsnorm(h_ref[0, rows, :], g_ref[...])
            u_ref[0, rows, :] = u.astype(BF16)
            for s in range(n_slabs):
                slab_ref[s, rows, :] = u[:, s * LANES:(s + 1) * LANES]
        for variant, d in ((1, ATT_DILATIONS[1]), (2, ATT_DILATIONS[2])):
            run = seq // d

            def body(r, carry, variant=variant, d=d, run=run):
                dst = pl.multiple_of(r * run, run)
                for s in range(n_slabs):
                    u_ref[variant, pl.ds(dst, run), s * LANES:(s + 1) * LANES] = (
                        slab_ref[s, pl.ds(r, run, stride=d), :].astype(BF16))
                return carry

            lax.fori_loop(0, d, body, 0)

    variant = jnp.where((j >= 5) & (j < 8), 1, jnp.where((j >= 8) & (j < N_BF16_BLOCKS), 2, 0))
    acc = _dot(u_ref[variant], w_ref[...])

    @pl.when(j < N_BF16_BLOCKS)
    def _():
        oa_ref[0] = acc.astype(BF16)

    @pl.when(j >= N_BF16_BLOCKS)
    def _():
        ob_ref[0] = acc


def _proj(h1, gain, w_all):
    b, seq, _ = h1.shape
    n_blocks = N_BF16_BLOCKS + N_F32_BLOCKS
    return pl.pallas_call(
        functools.partial(_proj_kernel, seq=seq),
        grid=(b, n_blocks),
        in_specs=[
            pl.BlockSpec((1, seq, D_MODEL), lambda i, j: (i, 0, 0), pipeline_mode=pl.Buffered(1)),
            _resident((1, D_MODEL)),
            pl.BlockSpec((D_MODEL, COL_BLOCK), lambda i, j: (0, j)),
        ],
        out_specs=[
            pl.BlockSpec((1, seq, COL_BLOCK), lambda i, j: (i, 0, jnp.minimum(j, N_BF16_BLOCKS - 1))),
            pl.BlockSpec((1, seq, COL_BLOCK), lambda i, j: (i, 0, jnp.maximum(j - N_BF16_BLOCKS, 0))),
        ],
        out_shape=[
            jax.ShapeDtypeStruct((b, seq, N_BF16_BLOCKS * COL_BLOCK), BF16),
            jax.ShapeDtypeStruct((b, seq, N_F32_BLOCKS * COL_BLOCK), F32),
        ],
        scratch_shapes=[
            pltpu.VMEM((D_MODEL // LANES, seq, LANES), F32),
            pltpu.VMEM((3, seq, D_MODEL), BF16),
        ],
        compiler_params=pltpu.CompilerParams(
            dimension_semantics=("parallel", "arbitrary"), vmem_limit_bytes=VMEM_LIMIT),
        name="proj",
    )(h1, gain, w_all)


def _split3(x):
    hi = x.astype(BF16)
    r1 = x - hi.astype(F32)
    mid = r1.astype(BF16)
    lo = (r1 - mid.astype(F32)).astype(BF16)
    return hi, mid, lo


def _bcast_rows(x, block, row):
    n = x.shape[0] // block
    parts = [jnp.broadcast_to(x[i * block + row:i * block + row + 1, :], (block, x.shape[1]))
             for i in range(n)]
    return parts[0] if n == 1 else jnp.concatenate(parts, axis=0)


def _hgrn2_kernel(q_ref, i_ref, f_ref, og_ref, lbp_ref, gain_ref, o_ref, st_ref):
    c_len = HG_CHUNK

    @pl.when(pl.program_id(1) == 0)
    def _():
        st_ref[...] = jnp.zeros_like(st_ref)

    lbp = lbp_ref[...]
    e = jnp.exp(lbp - jnp.max(lbp, axis=0, keepdims=True))
    lb_all = e[0:1, :] / jnp.sum(e, axis=0, keepdims=True)
    gain_all = gain_ref[...]

    t_idx = lax.broadcasted_iota(jnp.int32, (c_len, c_len), 0)
    s_idx = lax.broadcasted_iota(jnp.int32, (c_len, c_len), 1)
    tri = jnp.where(s_idx <= t_idx, 1.0, 0.0).astype(BF16)
    levels = []
    b = c_len // 2
    while b >= SUBLANES:
        shift = (2 * b).bit_length() - 1
        mask = ((t_idx >> shift == s_idx >> shift) & ((t_idx & b) != 0) & ((s_idx & b) == 0))
        levels.append((b, mask))
        b //= 2
    sub_shift = SUBLANES.bit_length() - 1
    diag_mask = (t_idx >> sub_shift == s_idx >> sub_shift) & (s_idx <= t_idx)
    e_row = lax.broadcasted_iota(jnp.int32, (SUBLANES * HG_DIM, c_len), 0)
    e_col = lax.broadcasted_iota(jnp.int32, (SUBLANES * HG_DIM, c_len), 1)
    spread = jnp.where(e_row >> (HG_DIM.bit_length() - 1) == (e_col & (SUBLANES - 1)),
                       1.0, 0.0).astype(BF16)

    def chunk(c, carry):
        rows = pl.ds(pl.multiple_of(c * c_len, c_len), c_len)
        for h in range(HG_HEADS):
            cols = slice(h * HG_DIM, (h + 1) * HG_DIM)
            q = q_ref[0, rows, cols].astype(F32)
            v = i_ref[0, rows, cols]
            lb = lb_all[:, cols]
            f = lb + (1.0 - lb) * _sigmoid(f_ref[0, rows, cols])
            k = 1.0 - f
            hi, mid, lo = _split3(jnp.log(f))
            g3 = _dot(tri, jnp.concatenate([hi, mid, lo], axis=1))
            g = (g3[:, :HG_DIM] + g3[:, HG_DIM:2 * HG_DIM]) + g3[:, 2 * HG_DIM:]
            g_last = g[c_len - 1:c_len, :]

            a = jnp.zeros((c_len, c_len), F32)
            for blk, mask in levels:
                gm = _bcast_rows(g, 2 * blk, blk - 1)
                ql = (q * jnp.exp(jnp.minimum(g - gm, 0.0))).astype(BF16)
                kl = (k * jnp.exp(jnp.minimum(gm - g, 0.0))).astype(BF16)
                a = a + jnp.where(mask, _dot_nt(ql, kl), 0.0)
            ws = []
            for jrow in range(SUBLANES):
                gb = _bcast_rows(g, SUBLANES, jrow)
                kb = _bcast_rows(k, SUBLANES, jrow)
                ws.append((q * kb * jnp.exp(jnp.minimum(g - gb, 0.0))).astype(BF16))
            a = a + jnp.where(diag_mask, _dot(jnp.concatenate(ws, axis=1), spread), 0.0)

            st = st_ref[h]
            o = _dot(a.astype(BF16), v) + _dot_nt((q * jnp.exp(g)).astype(BF16), st.astype(BF16))
            kg = (k * jnp.exp(g_last - g)).astype(BF16)
            st_ref[h] = st * jnp.exp(g_last) + _dot_tn(v, kg)

            o = o * lax.rsqrt(jnp.mean(o * o, axis=-1, keepdims=True) + EPS)
            og = og_ref[0, rows, cols]
            o_ref[0, rows, cols] = (o * gain_all[:, cols] * (og * _sigmoid(og))).astype(BF16)
        return carry

    lax.fori_loop(0, HG_STEP // c_len, chunk, 0)


def _hgrn2(proj_a, proj_b, lower_bounds, out_gain):
    b, seq, _ = proj_a.shape
    blk = (1, HG_STEP, HG_WIDTH)
    return pl.pallas_call(
        _hgrn2_kernel,
        grid=(b, seq // HG_STEP),
        in_specs=[
            pl.BlockSpec(blk, lambda i, j: (i, j, 0)),
            pl.BlockSpec(blk, lambda i, j: (i, j, 1)),
            pl.BlockSpec(blk, lambda i, j: (i, j, 0)),
            pl.BlockSpec(blk, lambda i, j: (i, j, 1)),
            _resident(lower_bounds.shape),
            _resident((1, HG_WIDTH)),
        ],
        out_specs=pl.BlockSpec(blk, lambda i, j: (i, j, 0)),
        out_shape=jax.ShapeDtypeStruct((b, seq, HG_WIDTH), BF16),
        scratch_shapes=[pltpu.VMEM((HG_HEADS, HG_DIM, HG_DIM), F32)],
        compiler_params=pltpu.CompilerParams(
            dimension_semantics=("parallel", "arbitrary"), vmem_limit_bytes=VMEM_LIMIT),
        name="hgrn2",
    )(proj_a, proj_a, proj_b, proj_b, lower_bounds, out_gain)


def _att_kernel(slope_ref, *refs, seq):
    qkv_refs = refs[:3 * N_GROUPS]
    y_ref = refs[3 * N_GROUPS]
    o_scr = refs[3 * N_GROUPS + 1:3 * N_GROUPS + 1 + N_GROUPS]
    l_scr = refs[3 * N_GROUPS + 1 + N_GROUPS:]
    blk = ATT_BLOCK
    n_blocks = seq // blk

    qi = lax.broadcasted_iota(jnp.int32, (blk, blk), 0)
    kj = lax.broadcasted_iota(jnp.int32, (blk, blk), 1)
    lane = lax.broadcasted_iota(jnp.int32, (blk, LANES), 1)
    first_head = lane < ATT_HEAD_DIM
    cur_valid = kj <= qi
    cur_dist = (qi - kj).astype(F32)
    prev_valid = kj >= qi
    prev_dist = (qi + blk - kj).astype(F32)

    for g in range(N_GROUPS):
        d = ATT_DILATIONS[g]
        per_class = n_blocks // d
        has_prev = per_class > 1
        q_ref, k_ref, v_ref = qkv_refs[3 * g:3 * g + 3]

        def block(idx, carry, g=g, d=d, per_class=per_class, has_prev=has_prev,
                  q_ref=q_ref, k_ref=k_ref, v_ref=v_ref):
            n = idx % per_class
            r = idx // per_class
            rows = pl.ds(pl.multiple_of(idx * blk, blk), blk)
            q = q_ref[0, rows, :]
            k_cur = k_ref[0, rows, :]
            v_cur = v_ref[0, rows, :]
            if has_prev:
                prows = pl.ds(pl.multiple_of(jnp.maximum(idx - 1, 0) * blk, blk), blk)
                k_prev = k_ref[0, prows, :]
                v_prev = v_ref[0, prows, :]
                first_shift = jnp.where(n > 0, 0.0, NEG_INF).astype(F32)
            outs, lses = [], []
            for hh in range(2):
                slope = slope_ref[0, 2 * g + hh:2 * g + hh + 1, :]
                qm = jnp.where(first_head if hh == 0 else ~first_head, q, jnp.zeros_like(q))
                s_cur = jnp.where(cur_valid, _dot_nt(qm, k_cur) - slope * cur_dist, NEG_INF)
                m = jnp.max(s_cur, axis=-1, keepdims=True)
                if has_prev:
                    s_prev = jnp.where(prev_valid, _dot_nt(qm, k_prev) - slope * prev_dist + first_shift,
                                       NEG_INF)
                    m = jnp.maximum(m, jnp.max(s_prev, axis=-1, keepdims=True))
                p_cur = jnp.exp(s_cur - m)
                den = jnp.sum(p_cur, axis=-1, keepdims=True)
                pv = _dot(p_cur.astype(BF16), v_cur)
                if has_prev:
                    p_prev = jnp.exp(s_prev - m)
                    den = den + jnp.sum(p_prev, axis=-1, keepdims=True)
                    pv = pv + _dot(p_prev.astype(BF16), v_prev)
                outs.append(pv / den)
                lses.append(jnp.broadcast_to(m + jnp.log(den), (blk, LANES)))
            o = jnp.where(first_head, outs[0], outs[1])
            lse = jnp.where(first_head, lses[0], lses[1])
            if d == 1:
                o_scr[g][rows, :] = o
                l_scr[g][rows, :] = lse
            else:
                dst = pl.ds(n * (blk * d) + r, blk, stride=d)
                o_scr[g][dst, :] = o
                l_scr[g][dst, :] = lse
            return carry

        lax.fori_loop(0, n_blocks, block, 0)

    rc = 256
    for c in range(seq // rc):
        rows = slice(c * rc, (c + 1) * rc)
        ls = [l_scr[g][rows, :] for g in range(N_GROUPS)]
        m = jnp.maximum(jnp.maximum(ls[0], ls[1]), ls[2])
        ws = [jnp.exp(l - m) for l in ls]
        num = ws[0] * o_scr[0][rows, :] + ws[1] * o_scr[1][rows, :] + ws[2] * o_scr[2][rows, :]
        y_ref[0, rows, :] = (num / (ws[0] + ws[1] + ws[2])).astype(BF16)


def _att(proj_a, slopes):
    b, seq, _ = proj_a.shape
    pairs = ATT_WIDTH // LANES
    base = 2 * HG_WIDTH // LANES
    per_tensor = ATT_WIDTH // LANES

    def qkv_spec(g, t):
        off = base + (3 * g + t) * per_tensor
        return pl.BlockSpec((1, seq, LANES), lambda i, j, off=off: (i, 0, off + j))

    return pl.pallas_call(
        functools.partial(_att_kernel, seq=seq),
        grid=(b, pairs),
        in_specs=[pl.BlockSpec((1, SUBLANES, LANES), lambda i, j: (j, 0, 0))]
        + [qkv_spec(g, t) for g in range(N_GROUPS) for t in range(3)],
        out_specs=pl.BlockSpec((1, seq, LANES), lambda i, j: (i, 0, j)),
        out_shape=jax.ShapeDtypeStruct((b, seq, ATT_WIDTH), BF16),
        scratch_shapes=[pltpu.VMEM((seq, LANES), F32) for _ in range(2 * N_GROUPS)],
        compiler_params=pltpu.CompilerParams(
            dimension_semantics=("parallel", "parallel"), vmem_limit_bytes=VMEM_LIMIT),
        name="att",
    )(slopes, *([proj_a] * (3 * N_GROUPS)))


def _tail_kernel(h1_ref, yhg_ref, yatt_ref, ghg_ref, gatt_ref, wa_ref, wb_ref, wo_ref,
                 g2_ref, wgu_ref, wd_ref, gf_ref, o_ref):
    merged = (_sigmoid(ghg_ref[...]) * _dot(yhg_ref[...], wa_ref[...])
              + _sigmoid(gatt_ref[...]) * _dot(yatt_ref[...], wb_ref[...]))
    h2 = h1_ref[...] + _dot(merged.astype(BF16), wo_ref[...])
    xn = _rmsnorm(h2, g2_ref[...]).astype(BF16)
    h3 = h2 + 0.5 * _swiglu(xn, wgu_ref, wd_ref)
    o_ref[...] = _rmsnorm(h3, gf_ref[...])


def _tail(h1, y_hg, y_att, proj_b, wa, wb, wo, g2, wgu, wd, gf):
    t = h1.shape[0]
    row = lambda width, col=0: pl.BlockSpec((ROW_TILE, width), lambda i, col=col: (i, col))
    return pl.pallas_call(
        _tail_kernel,
        grid=(t // ROW_TILE,),
        in_specs=[
            row(D_MODEL), row(HG_WIDTH), row(ATT_WIDTH),
            row(D_MODEL, 1), row(D_MODEL, 2),
            _resident((HG_WIDTH, D_MODEL)), _resident((ATT_WIDTH, D_MODEL)), _resident((D_MODEL, D_MODEL)),
            _resident((1, D_MODEL)), _resident((D_MODEL, 2 * D_FF)), _resident((D_FF, D_MODEL)),
            _resident((1, D_MODEL)),
        ],
        out_specs=row(D_MODEL),
        out_shape=jax.ShapeDtypeStruct((t, D_MODEL), F32),
        compiler_params=pltpu.CompilerParams(
            dimension_semantics=("parallel",), vmem_limit_bytes=VMEM_LIMIT),
        name="tail",
    )(h1, y_hg, y_att, proj_b, proj_b, wa, wb, wo, g2, wgu, wd, gf)


def _alibi_table():
    n_heads = N_GROUPS * ATT_HEADS
    slopes = jnp.exp2(-ALIBI_MAX * jnp.arange(1, n_heads + 1, dtype=F32) / n_heads)
    slopes = slopes.reshape(N_GROUPS, ATT_HEADS // 2, 2) * jnp.asarray(ATT_DILATIONS, F32)[:, None, None]
    table = jnp.zeros((ATT_HEADS // 2, SUBLANES, LANES), F32)
    table = table.at[:, :2 * N_GROUPS, :].set(
        jnp.broadcast_to(slopes.transpose(1, 0, 2).reshape(ATT_HEADS // 2, 2 * N_GROUPS, 1),
                         (ATT_HEADS // 2, 2 * N_GROUPS, LANES)))
    return table


def _arrange_w_in(w_in):
    kw = HG_WIDTH
    hg_q, hg_f, hg_i, hg_og = (w_in[:, i * kw:(i + 1) * kw] for i in range(4))
    att = w_in[:, 4 * kw:4 * kw + 3 * N_GROUPS * ATT_WIDTH]
    gates = w_in[:, 4 * kw + 3 * N_GROUPS * ATT_WIDTH:]
    att = att.reshape(D_MODEL, N_GROUPS, 3, ATT_WIDTH)
    att = att.at[:, :, 0, :].multiply(ATT_HEAD_DIM ** -0.5).reshape(D_MODEL, -1)
    return jnp.concatenate([hg_q, hg_i, att, hg_f, hg_og, gates], axis=1).astype(BF16)


def kernel(x, ffn1_norm, ffn1_w_gate_up, ffn1_w_down, mix_norm, w_in, hg_lower_bounds, hg_out_norm,
           w_branch_hg, w_branch_att, w_out, ffn2_norm, ffn2_w_gate_up, ffn2_w_down, final_norm):
    b, seq, d = x.shape
    assert d == D_MODEL and seq % (ATT_BLOCK * ATT_DILATIONS[-1]) == 0 and (b * seq) % ROW_TILE == 0
    x2d = x.reshape(b * seq, d)

    h1 = _ffn1(x2d, ffn1_norm[0:1], ffn1_w_gate_up[0].astype(BF16), ffn1_w_down[0].astype(BF16))
    proj_a, proj_b = _proj(h1.reshape(b, seq, d), mix_norm[0:1], _arrange_w_in(w_in[0]))
    y_hg = _hgrn2(proj_a, proj_b, hg_lower_bounds.astype(F32), hg_out_norm[0:1])
    y_att = _att(proj_a, _alibi_table())
    out = _tail(h1, y_hg.reshape(b * seq, -1), y_att.reshape(b * seq, -1),
                proj_b.reshape(b * seq, -1),
                w_branch_hg[0].astype(BF16), w_branch_att[0].astype(BF16), w_out[0].astype(BF16),
                ffn2_norm[0:1], ffn2_w_gate_up[0].astype(BF16), ffn2_w_down[0].astype(BF16),
                final_norm.reshape(1, d))
    return out.reshape(b, seq, d)
```

```python
import functools

import jax
import jax.numpy as jnp
from jax import lax
from jax.experimental import pallas as pl
from jax.experimental.pallas import tpu as pltpu

F32 = jnp.float32
BF16 = jnp.bfloat16

D_MODEL = 1024
D_FF = 2816
HG_HEADS = 4
HG_DIM = 128
HG_WIDTH = HG_HEADS * HG_DIM
ATT_DILATIONS = (1, 4, 16)
ATT_BLOCK = 128
ATT_HEADS = 8
ATT_HEAD_DIM = 64
ATT_WIDTH = ATT_HEADS * ATT_HEAD_DIM
N_GROUPS = 3
ALIBI_MAX = 8.0
EPS = 1e-6
NEG_INF = -1e30

LANES = 128
SUBLANES = 8
VMEM_LIMIT = 56 * 1024 * 1024

ROW_TILE = 512
COL_BLOCK = 512
N_BF16_BLOCKS = 11
N_F32_BLOCKS = 6
HG_CHUNK = 64
HG_STEP = 512
ATT_UNROLL = 8


def _sigmoid(x):
    return 1.0 / (1.0 + jnp.exp(-x))


def _rmsnorm(x, gain):
    ms = jnp.mean(x * x, axis=-1, keepdims=True)
    return x * lax.rsqrt(ms + EPS) * gain


def _dot(a, b):
    return jnp.dot(a, b, preferred_element_type=F32)


def _dot_nt(a, b):
    return lax.dot_general(a, b, (((1,), (1,)), ((), ())), preferred_element_type=F32)


def _dot_tn(a, b):
    return lax.dot_general(a, b, (((0,), (0,)), ((), ())), preferred_element_type=F32)


def _swiglu(xn_bf16, wgu_ref, wd_ref):
    half = D_FF // 2
    y = None
    for c in range(2):
        a = _dot(xn_bf16, wgu_ref[:, c * half:(c + 1) * half])
        b = _dot(xn_bf16, wgu_ref[:, D_FF + c * half:D_FF + (c + 1) * half])
        act = (a * _sigmoid(a) * b).astype(BF16)
        part = _dot(act, wd_ref[c * half:(c + 1) * half, :])
        y = part if y is None else y + part
    return y


def _ffn1_kernel(x_ref, g_ref, wgu_ref, wd_ref, o_ref):
    x = x_ref[...]
    xn = _rmsnorm(x, g_ref[...]).astype(BF16)
    o_ref[...] = x + 0.5 * _swiglu(xn, wgu_ref, wd_ref)


def _resident(shape):
    return pl.BlockSpec(shape, lambda *_: (0,) * len(shape), pipeline_mode=pl.Buffered(1))


def _ffn1(x2d, gain, wgu, wd):
    t = x2d.shape[0]
    return pl.pallas_call(
        _ffn1_kernel,
        grid=(t // ROW_TILE,),
        in_specs=[
            pl.BlockSpec((ROW_TILE, D_MODEL), lambda i: (i, 0)),
            _resident((1, D_MODEL)),
            _resident((D_MODEL, 2 * D_FF)),
            _resident((D_FF, D_MODEL)),
        ],
        out_specs=pl.BlockSpec((ROW_TILE, D_MODEL), lambda i: (i, 0)),
        out_shape=jax.ShapeDtypeStruct((t, D_MODEL), F32),
        compiler_params=pltpu.CompilerParams(
            dimension_semantics=("parallel",), vmem_limit_bytes=VMEM_LIMIT),
        name="ffn1",
    )(x2d, gain, wgu, wd)


def _proj_kernel(h_ref, g_ref, w_ref, oa_ref, ob_ref, slab_ref, u_ref, *, seq):
    j = pl.program_id(1)
    n_slabs = D_MODEL // LANES

    @pl.when(j == 0)
    def _normalise():
        rc = 256
        for c in range(seq // rc):
            rows = slice(c * rc, (c + 1) * rc)
            u = _rmsnorm(h_ref[0, rows, :], g_ref[...])
            u_ref[0, rows, :] = u.astype(BF16)
            for s in range(n_slabs):
                slab_ref[s, rows, :] = u[:, s * LANES:(s + 1) * LANES]
        for variant, d in ((1, ATT_DILATIONS[1]), (2, ATT_DILATIONS[2])):
            run = seq // d

            def body(r, carry, variant=variant, d=d, run=run):
                dst = pl.multiple_of(r * run, run)
                for s in range(n_slabs):
                    u_ref[variant, pl.ds(dst, run), s * LANES:(s + 1) * LANES] = (
                        slab_ref[s, pl.ds(r, run, stride=d), :].astype(BF16))
                return carry

            lax.fori_loop(0, d, body, 0)

    variant = jnp.where((j >= 5) & (j < 8), 1, jnp.where((j >= 8) & (j < N_BF16_BLOCKS), 2, 0))
    acc = _dot(u_ref[variant], w_ref[...])

    @pl.when(j < N_BF16_BLOCKS)
    def _():
        oa_ref[0] = acc.astype(BF16)

    @pl.when(j >= N_BF16_BLOCKS)
    def _():
        ob_ref[0] = acc


def _proj(h1, gain, w_all):
    b, seq, _ = h1.shape
    n_blocks = N_BF16_BLOCKS + N_F32_BLOCKS
    return pl.pallas_call(
        functools.partial(_proj_kernel, seq=seq),
        grid=(b, n_blocks),
        in_specs=[
            pl.BlockSpec((1, seq, D_MODEL), lambda i, j: (i, 0, 0), pipeline_mode=pl.Buffered(1)),
            _resident((1, D_MODEL)),
            pl.BlockSpec((D_MODEL, COL_BLOCK), lambda i, j: (0, j)),
        ],
        out_specs=[
            pl.BlockSpec((1, seq, COL_BLOCK), lambda i, j: (i, 0, jnp.minimum(j, N_BF16_BLOCKS - 1))),
            pl.BlockSpec((1, seq, COL_BLOCK), lambda i, j: (i, 0, jnp.maximum(j - N_BF16_BLOCKS, 0))),
        ],
        out_shape=[
            jax.ShapeDtypeStruct((b, seq, N_BF16_BLOCKS * COL_BLOCK), BF16),
            jax.ShapeDtypeStruct((b, seq, N_F32_BLOCKS * COL_BLOCK), F32),
        ],
        scratch_shapes=[
            pltpu.VMEM((D_MODEL // LANES, seq, LANES), F32),
            pltpu.VMEM((3, seq, D_MODEL), BF16),
        ],
        compiler_params=pltpu.CompilerParams(
            dimension_semantics=("parallel", "arbitrary"), vmem_limit_bytes=VMEM_LIMIT),
        name="proj",
    )(h1, gain, w_all)


def _split3(x):
    hi = x.astype(BF16)
    r1 = x - hi.astype(F32)
    mid = r1.astype(BF16)
    lo = (r1 - mid.astype(F32)).astype(BF16)
    return hi, mid, lo


def _bcast_rows(x, block, row):
    n = x.shape[0] // block
    parts = [jnp.broadcast_to(x[i * block + row:i * block + row + 1, :], (block, x.shape[1]))
             for i in range(n)]
    return parts[0] if n == 1 else jnp.concatenate(parts, axis=0)


def _hgrn2_kernel(q_ref, i_ref, f_ref, og_ref, lbp_ref, gain_ref, o_ref, st_ref):
    c_len = HG_CHUNK

    @pl.when(pl.program_id(1) == 0)
    def _():
        st_ref[...] = jnp.zeros_like(st_ref)

    lbp = lbp_ref[...]
    e = jnp.exp(lbp - jnp.max(lbp, axis=0, keepdims=True))
    lb_all = e[0:1, :] / jnp.sum(e, axis=0, keepdims=True)
    gain_all = gain_ref[...]

    t_idx = lax.broadcasted_iota(jnp.int32, (c_len, c_len), 0)
    s_idx = lax.broadcasted_iota(jnp.int32, (c_len, c_len), 1)
    tri = jnp.where(s_idx <= t_idx, 1.0, 0.0).astype(BF16)
    levels = []
    b = c_len // 2
    while b >= SUBLANES:
        shift = (2 * b).bit_length() - 1
        mask = ((t_idx >> shift == s_idx >> shift) & ((t_idx & b) != 0) & ((s_idx & b) == 0))
        levels.append((b, mask))
        b //= 2
    sub_shift = SUBLANES.bit_length() - 1
    diag_mask = (t_idx >> sub_shift == s_idx >> sub_shift) & (s_idx <= t_idx)
    e_row = lax.broadcasted_iota(jnp.int32, (SUBLANES * HG_DIM, c_len), 0)
    e_col = lax.broadcasted_iota(jnp.int32, (SUBLANES * HG_DIM, c_len), 1)
    spread = jnp.where(e_row >> (HG_DIM.bit_length() - 1) == (e_col & (SUBLANES - 1)),
                       1.0, 0.0).astype(BF16)

    def chunk(c, carry):
        rows = pl.ds(pl.multiple_of(c * c_len, c_len), c_len)
        for h in range(HG_HEADS):
            cols = slice(h * HG_DIM, (h + 1) * HG_DIM)
            q = q_ref[0, rows, cols].astype(F32)
            v = i_ref[0, rows, cols]
            lb = lb_all[:, cols]
            f = lb + (1.0 - lb) * _sigmoid(f_ref[0, rows, cols])
            k = 1.0 - f
            hi, mid, lo = _split3(jnp.log(f))
            g3 = _dot(tri, jnp.concatenate([hi, mid, lo], axis=1))
            g = (g3[:, :HG_DIM] + g3[:, HG_DIM:2 * HG_DIM]) + g3[:, 2 * HG_DIM:]
            g_last = g[c_len - 1:c_len, :]

            a = jnp.zeros((c_len, c_len), F32)
            for blk, mask in levels:
                gm = _bcast_rows(g, 2 * blk, blk - 1)
                ql = (q * jnp.exp(jnp.minimum(g - gm, 0.0))).astype(BF16)
                kl = (k * jnp.exp(jnp.minimum(gm - g, 0.0))).astype(BF16)
                a = a + jnp.where(mask, _dot_nt(ql, kl), 0.0)
            ws = []
            for jrow in range(SUBLANES):
                gb = _bcast_rows(g, SUBLANES, jrow)
                kb = _bcast_rows(k, SUBLANES, jrow)
                ws.append((q * kb * jnp.exp(jnp.minimum(g - gb, 0.0))).astype(BF16))
            a = a + jnp.where(diag_mask, _dot(jnp.concatenate(ws, axis=1), spread), 0.0)

            st = st_ref[h]
            o = _dot(a.astype(BF16), v) + _dot_nt((q * jnp.exp(g)).astype(BF16), st.astype(BF16))
            kg = (k * jnp.exp(g_last - g)).astype(BF16)
            st_ref[h] = st * jnp.exp(g_last) + _dot_tn(v, kg)

            o = o * lax.rsqrt(jnp.mean(o * o, axis=-1, keepdims=True) + EPS)
            og = og_ref[0, rows, cols]
            o_ref[0, rows, cols] = (o * gain_all[:, cols] * (og * _sigmoid(og))).astype(BF16)
        return carry

    lax.fori_loop(0, HG_STEP // c_len, chunk, 0)


def _hgrn2(proj_a, proj_b, lower_bounds, out_gain):
    b, seq, _ = proj_a.shape
    blk = (1, HG_STEP, HG_WIDTH)
    return pl.pallas_call(
        _hgrn2_kernel,
        grid=(b, seq // HG_STEP),
        in_specs=[
            pl.BlockSpec(blk, lambda i, j: (i, j, 0)),
            pl.BlockSpec(blk, lambda i, j: (i, j, 1)),
            pl.BlockSpec(blk, lambda i, j: (i, j, 0)),
            pl.BlockSpec(blk, lambda i, j: (i, j, 1)),
            _resident(lower_bounds.shape),
            _resident((1, HG_WIDTH)),
        ],
        out_specs=pl.BlockSpec(blk, lambda i, j: (i, j, 0)),
        out_shape=jax.ShapeDtypeStruct((b, seq, HG_WIDTH), BF16),
        scratch_shapes=[pltpu.VMEM((HG_HEADS, HG_DIM, HG_DIM), F32)],
        compiler_params=pltpu.CompilerParams(
            dimension_semantics=("parallel", "arbitrary"), vmem_limit_bytes=VMEM_LIMIT),
        name="hgrn2",
    )(proj_a, proj_a, proj_b, proj_b, lower_bounds, out_gain)


def _att_kernel(slope_ref, *refs, seq):
    qkv_refs = refs[:3 * N_GROUPS]
    y_ref = refs[3 * N_GROUPS]
    scr = refs[3 * N_GROUPS + 1:]
    o_scr, m_scr, d_scr = scr[0:N_GROUPS], scr[N_GROUPS:2 * N_GROUPS], scr[2 * N_GROUPS:3 * N_GROUPS]
    bias_scr = scr[3 * N_GROUPS]
    blk = ATT_BLOCK
    n_blocks = seq // blk

    qi = lax.broadcasted_iota(jnp.int32, (blk, 2 * blk), 0)
    kj = lax.broadcasted_iota(jnp.int32, (blk, 2 * blk), 1)
    dist = qi + blk - kj
    in_window = (dist >= 0) & (dist <= blk)
    in_window_first = in_window & (kj >= blk)
    dist_f = dist.astype(F32)
    for gh in range(2 * N_GROUPS):
        alibi = -slope_ref[0, gh:gh + 1, :] * dist_f
        bias_scr[2 * gh] = jnp.where(in_window, alibi, NEG_INF)
        bias_scr[2 * gh + 1] = jnp.where(in_window_first, alibi, NEG_INF)

    lane = lax.broadcasted_iota(jnp.int32, (blk, LANES), 1)
    first_head = lane < ATT_HEAD_DIM
    ones2 = jnp.ones((2 * blk, LANES), BF16)

    for g in range(N_GROUPS):
        d = ATT_DILATIONS[g]
        per_class = n_blocks // d
        q_ref, k_ref, v_ref = qkv_refs[3 * g:3 * g + 3]

        def block(idx, g=g, d=d, per_class=per_class, q_ref=q_ref, k_ref=k_ref, v_ref=v_ref):
            n = idx & (per_class - 1)
            r = idx >> (per_class.bit_length() - 1)
            rows = pl.ds(pl.multiple_of(idx * blk, blk), blk)
            prows = pl.ds(pl.multiple_of(jnp.maximum(idx - 1, 0) * blk, blk), blk)
            q = q_ref[0, rows, :]
            k = jnp.concatenate([k_ref[0, prows, :], k_ref[0, rows, :]], axis=0)
            v = jnp.concatenate([v_ref[0, prows, :], v_ref[0, rows, :]], axis=0)
            v_aug = jnp.concatenate([v, ones2], axis=1)
            is_first = (n == 0).astype(jnp.int32)
            pvs, ms = [], []
            for hh in range(2):
                qm = jnp.where(first_head if hh == 0 else ~first_head, q, jnp.zeros_like(q))
                s = _dot_nt(qm, k) + bias_scr[2 * (2 * g + hh) + is_first]
                m = jnp.max(jnp.maximum(s[:, :blk], s[:, blk:]), axis=-1, keepdims=True)
                pvs.append(_dot(jnp.exp(s - m).astype(BF16), v_aug))
                ms.append(jnp.broadcast_to(m, (blk, LANES)))
            o = jnp.where(first_head, pvs[0][:, :LANES], pvs[1][:, :LANES])
            den = jnp.where(first_head, pvs[0][:, LANES:], pvs[1][:, LANES:])
            mx = jnp.where(first_head, ms[0], ms[1])
            if d == 1:
                dst = rows
            else:
                dst = pl.ds(n * (blk * d) + r, blk, stride=d)
            o_scr[g][dst, :] = o
            d_scr[g][dst, :] = den
            m_scr[g][dst, :] = mx

        def trip(i, carry, block=block):
            for u in range(ATT_UNROLL):
                block(i * ATT_UNROLL + u)
            return carry

        lax.fori_loop(0, n_blocks // ATT_UNROLL, trip, 0)

    rc = 256
    for c in range(seq // rc):
        rows = slice(c * rc, (c + 1) * rc)
        ms = [m_scr[g][rows, :] for g in range(N_GROUPS)]
        m = jnp.maximum(jnp.maximum(ms[0], ms[1]), ms[2])
        ws = [jnp.exp(mg - m) for mg in ms]
        num = ws[0] * o_scr[0][rows, :] + ws[1] * o_scr[1][rows, :] + ws[2] * o_scr[2][rows, :]
        den = ws[0] * d_scr[0][rows, :] + ws[1] * d_scr[1][rows, :] + ws[2] * d_scr[2][rows, :]
        y_ref[0, rows, :] = (num / den).astype(BF16)


def _att(proj_a, slopes):
    b, seq, _ = proj_a.shape
    pairs = ATT_WIDTH // LANES
    base = 2 * HG_WIDTH // LANES
    per_tensor = ATT_WIDTH // LANES

    def qkv_spec(g, t):
        off = base + (3 * g + t) * per_tensor
        return pl.BlockSpec((1, seq, LANES), lambda i, j, off=off: (i, 0, off + j))

    return pl.pallas_call(
        functools.partial(_att_kernel, seq=seq),
        grid=(b, pairs),
        in_specs=[pl.BlockSpec((1, SUBLANES, 2 * ATT_BLOCK), lambda i, j: (j, 0, 0))]
        + [qkv_spec(g, t) for g in range(N_GROUPS) for t in range(3)],
        out_specs=pl.BlockSpec((1, seq, LANES), lambda i, j: (i, 0, j)),
        out_shape=jax.ShapeDtypeStruct((b, seq, ATT_WIDTH), BF16),
        scratch_shapes=[pltpu.VMEM((seq, LANES), F32) for _ in range(3 * N_GROUPS)]
        + [pltpu.VMEM((4 * N_GROUPS, ATT_BLOCK, 2 * ATT_BLOCK), F32)],
        compiler_params=pltpu.CompilerParams(
            dimension_semantics=("parallel", "parallel"), vmem_limit_bytes=VMEM_LIMIT),
        name="att",
    )(slopes, *([proj_a] * (3 * N_GROUPS)))


def _tail_kernel(h1_ref, yhg_ref, yatt_ref, ghg_ref, gatt_ref, wa_ref, wb_ref, wo_ref,
                 g2_ref, wgu_ref, wd_ref, gf_ref, o_ref):
    merged = (_sigmoid(ghg_ref[...]) * _dot(yhg_ref[...], wa_ref[...])
              + _sigmoid(gatt_ref[...]) * _dot(yatt_ref[...], wb_ref[...]))
    h2 = h1_ref[...] + _dot(merged.astype(BF16), wo_ref[...])
    xn = _rmsnorm(h2, g2_ref[...]).astype(BF16)
    h3 = h2 + 0.5 * _swiglu(xn, wgu_ref, wd_ref)
    o_ref[...] = _rmsnorm(h3, gf_ref[...])


def _tail(h1, y_hg, y_att, proj_b, wa, wb, wo, g2, wgu, wd, gf):
    t = h1.shape[0]
    row = lambda width, col=0: pl.BlockSpec((ROW_TILE, width), lambda i, col=col: (i, col))
    return pl.pallas_call(
        _tail_kernel,
        grid=(t // ROW_TILE,),
        in_specs=[
            row(D_MODEL), row(HG_WIDTH), row(ATT_WIDTH),
            row(D_MODEL, 1), row(D_MODEL, 2),
            _resident((HG_WIDTH, D_MODEL)), _resident((ATT_WIDTH, D_MODEL)), _resident((D_MODEL, D_MODEL)),
            _resident((1, D_MODEL)), _resident((D_MODEL, 2 * D_FF)), _resident((D_FF, D_MODEL)),
            _resident((1, D_MODEL)),
        ],
        out_specs=row(D_MODEL),
        out_shape=jax.ShapeDtypeStruct((t, D_MODEL), F32),
        compiler_params=pltpu.CompilerParams(
            dimension_semantics=("parallel",), vmem_limit_bytes=VMEM_LIMIT),
        name="tail",
    )(h1, y_hg, y_att, proj_b, proj_b, wa, wb, wo, g2, wgu, wd, gf)


def _alibi_table():
    n_heads = N_GROUPS * ATT_HEADS
    slopes = jnp.exp2(-ALIBI_MAX * jnp.arange(1, n_heads + 1, dtype=F32) / n_heads)
    slopes = slopes.reshape(N_GROUPS, ATT_HEADS // 2, 2) * jnp.asarray(ATT_DILATIONS, F32)[:, None, None]
    table = jnp.zeros((ATT_HEADS // 2, SUBLANES, 2 * ATT_BLOCK), F32)
    table = table.at[:, :2 * N_GROUPS, :].set(
        jnp.broadcast_to(slopes.transpose(1, 0, 2).reshape(ATT_HEADS // 2, 2 * N_GROUPS, 1),
                         (ATT_HEADS // 2, 2 * N_GROUPS, 2 * ATT_BLOCK)))
    return table


def _arrange_w_in(w_in):
    kw = HG_WIDTH
    hg_q, hg_f, hg_i, hg_og = (w_in[:, i * kw:(i + 1) * kw] for i in range(4))
    att = w_in[:, 4 * kw:4 * kw + 3 * N_GROUPS * ATT_WIDTH]
    gates = w_in[:, 4 * kw + 3 * N_GROUPS * ATT_WIDTH:]
    att = att.reshape(D_MODEL, N_GROUPS, 3, ATT_WIDTH)
    att = att.at[:, :, 0, :].multiply(ATT_HEAD_DIM ** -0.5).reshape(D_MODEL, -1)
    return jnp.concatenate([hg_q, hg_i, att, hg_f, hg_og, gates], axis=1).astype(BF16)


def kernel(x, ffn1_norm, ffn1_w_gate_up, ffn1_w_down, mix_norm, w_in, hg_lower_bounds, hg_out_norm,
           w_branch_hg, w_branch_att, w_out, ffn2_norm, ffn2_w_gate_up, ffn2_w_down, final_norm):
    b, seq, d = x.shape
    assert d == D_MODEL and seq % (ATT_BLOCK * ATT_DILATIONS[-1]) == 0 and (b * seq) % ROW_TILE == 0
    x2d = x.reshape(b * seq, d)

    h1 = _ffn1(x2d, ffn1_norm[0:1], ffn1_w_gate_up[0].astype(BF16), ffn1_w_down[0].astype(BF16))
    proj_a, proj_b = _proj(h1.reshape(b, seq, d), mix_norm[0:1], _arrange_w_in(w_in[0]))
    y_hg = _hgrn2(proj_a, proj_b, hg_lower_bounds.astype(F32), hg_out_norm[0:1])
    y_att = _att(proj_a, _alibi_table())
    out = _tail(h1, y_hg.reshape(b * seq, -1), y_att.reshape(b * seq, -1),
                proj_b.reshape(b * seq, -1),
                w_branch_hg[0].astype(BF16), w_branch_att[0].astype(BF16), w_out[0].astype(BF16),
                ffn2_norm[0:1], ffn2_w_gate_up[0].astype(BF16), ffn2_w_down[0].astype(BF16),
                final_norm.reshape(1, d))
    return out.reshape(b, seq, d)
```

```python
import functools

import jax
import jax.numpy as jnp
from jax import lax
from jax.experimental import pallas as pl
from jax.experimental.pallas import tpu as pltpu

F32 = jnp.float32
BF16 = jnp.bfloat16

D_MODEL = 1024
D_FF = 2816
HG_HEADS = 4
HG_DIM = 128
HG_WIDTH = HG_HEADS * HG_DIM
ATT_DILATIONS = (1, 4, 16)
ATT_BLOCK = 128
ATT_HEADS = 8
ATT_HEAD_DIM = 64
ATT_WIDTH = ATT_HEADS * ATT_HEAD_DIM
N_GROUPS = 3
ALIBI_MAX = 8.0
EPS = 1e-6
NEG_INF = -1e30

LANES = 128
SUBLANES = 8
VMEM_LIMIT = 56 * 1024 * 1024

ROW_TILE = 512
COL_BLOCK = 512
N_BF16_BLOCKS = 11
N_F32_BLOCKS = 6
HG_CHUNK = 64
HG_STEP = 512
HG_UNROLL = 2
ATT_UNROLL = 8


def _sigmoid(x):
    return 1.0 / (1.0 + jnp.exp(-x))


def _rmsnorm(x, gain):
    ms = jnp.mean(x * x, axis=-1, keepdims=True)
    return x * lax.rsqrt(ms + EPS) * gain


def _dot(a, b):
    return jnp.dot(a, b, preferred_element_type=F32)


def _dot_nt(a, b):
    return lax.dot_general(a, b, (((1,), (1,)), ((), ())), preferred_element_type=F32)


def _dot_tn(a, b):
    return lax.dot_general(a, b, (((0,), (0,)), ((), ())), preferred_element_type=F32)


def _swiglu(xn_bf16, wgu_ref, wd_ref):
    half = D_FF // 2
    y = None
    for c in range(2):
        a = _dot(xn_bf16, wgu_ref[:, c * half:(c + 1) * half])
        b = _dot(xn_bf16, wgu_ref[:, D_FF + c * half:D_FF + (c + 1) * half])
        act = (a * _sigmoid(a) * b).astype(BF16)
        part = _dot(act, wd_ref[c * half:(c + 1) * half, :])
        y = part if y is None else y + part
    return y


def _ffn1_kernel(x_ref, g_ref, wgu_ref, wd_ref, o_ref):
    x = x_ref[...]
    xn = _rmsnorm(x, g_ref[...]).astype(BF16)
    o_ref[...] = x + 0.5 * _swiglu(xn, wgu_ref, wd_ref)


def _resident(shape):
    return pl.BlockSpec(shape, lambda *_: (0,) * len(shape), pipeline_mode=pl.Buffered(1))


def _ffn1(x2d, gain, wgu, wd):
    t = x2d.shape[0]
    return pl.pallas_call(
        _ffn1_kernel,
        grid=(t // ROW_TILE,),
        in_specs=[
            pl.BlockSpec((ROW_TILE, D_MODEL), lambda i: (i, 0)),
            _resident((1, D_MODEL)),
            _resident((D_MODEL, 2 * D_FF)),
            _resident((D_FF, D_MODEL)),
        ],
        out_specs=pl.BlockSpec((ROW_TILE, D_MODEL), lambda i: (i, 0)),
        out_shape=jax.ShapeDtypeStruct((t, D_MODEL), F32),
        compiler_params=pltpu.CompilerParams(
            dimension_semantics=("parallel",), vmem_limit_bytes=VMEM_LIMIT),
        name="ffn1",
    )(x2d, gain, wgu, wd)


def _proj_kernel(h_ref, g_ref, w_ref, oa_ref, ob_ref, slab_ref, u_ref, *, seq):
    j = pl.program_id(1)
    n_slabs = D_MODEL // LANES

    @pl.when(j == 0)
    def _normalise():
        rc = 256
        for c in range(seq // rc):
            rows = slice(c * rc, (c + 1) * rc)
            u = _rmsnorm(h_ref[0, rows, :], g_ref[...])
            u_ref[0, rows, :] = u.astype(BF16)
            for s in range(n_slabs):
                slab_ref[s, rows, :] = u[:, s * LANES:(s + 1) * LANES]
        for variant, d in ((1, ATT_DILATIONS[1]), (2, ATT_DILATIONS[2])):
            run = seq // d

            def body(r, carry, variant=variant, d=d, run=run):
                dst = pl.multiple_of(r * run, run)
                for s in range(n_slabs):
                    u_ref[variant, pl.ds(dst, run), s * LANES:(s + 1) * LANES] = (
                        slab_ref[s, pl.ds(r, run, stride=d), :].astype(BF16))
                return carry

            lax.fori_loop(0, d, body, 0)

    variant = jnp.where((j >= 5) & (j < 8), 1, jnp.where((j >= 8) & (j < N_BF16_BLOCKS), 2, 0))
    acc = _dot(u_ref[variant], w_ref[...])

    @pl.when(j < N_BF16_BLOCKS)
    def _():
        oa_ref[0] = acc.astype(BF16)

    @pl.when(j >= N_BF16_BLOCKS)
    def _():
        ob_ref[0] = acc


def _proj(h1, gain, w_all):
    b, seq, _ = h1.shape
    n_blocks = N_BF16_BLOCKS + N_F32_BLOCKS
    return pl.pallas_call(
        functools.partial(_proj_kernel, seq=seq),
        grid=(b, n_blocks),
        in_specs=[
            pl.BlockSpec((1, seq, D_MODEL), lambda i, j: (i, 0, 0), pipeline_mode=pl.Buffered(1)),
            _resident((1, D_MODEL)),
            pl.BlockSpec((D_MODEL, COL_BLOCK), lambda i, j: (0, j)),
        ],
        out_specs=[
            pl.BlockSpec((1, seq, COL_BLOCK), lambda i, j: (i, 0, jnp.minimum(j, N_BF16_BLOCKS - 1))),
            pl.BlockSpec((1, seq, COL_BLOCK), lambda i, j: (i, 0, jnp.maximum(j - N_BF16_BLOCKS, 0))),
        ],
        out_shape=[
            jax.ShapeDtypeStruct((b, seq, N_BF16_BLOCKS * COL_BLOCK), BF16),
            jax.ShapeDtypeStruct((b, seq, N_F32_BLOCKS * COL_BLOCK), F32),
        ],
        scratch_shapes=[
            pltpu.VMEM((D_MODEL // LANES, seq, LANES), F32),
            pltpu.VMEM((3, seq, D_MODEL), BF16),
        ],
        compiler_params=pltpu.CompilerParams(
            dimension_semantics=("parallel", "arbitrary"), vmem_limit_bytes=VMEM_LIMIT),
        name="proj",
    )(h1, gain, w_all)


def _bcast_rows(x, block, row):
    n = x.shape[0] // block
    parts = [jnp.broadcast_to(x[i * block + row:i * block + row + 1, :], (block, x.shape[1]))
             for i in range(n)]
    return parts[0] if n == 1 else jnp.concatenate(parts, axis=0)


def _hgrn2_kernel(q_ref, i_ref, f_ref, og_ref, lbp_ref, gain_ref, o_ref, st_ref):
    c_len = HG_CHUNK

    @pl.when(pl.program_id(1) == 0)
    def _():
        st_ref[...] = jnp.zeros_like(st_ref)

    lbp = lbp_ref[...]
    e = jnp.exp(lbp - jnp.max(lbp, axis=0, keepdims=True))
    lb_all = e[0:1, :] / jnp.sum(e, axis=0, keepdims=True)
    gain_all = gain_ref[...]

    t_idx = lax.broadcasted_iota(jnp.int32, (c_len, c_len), 0)
    s_idx = lax.broadcasted_iota(jnp.int32, (c_len, c_len), 1)
    halves = [1 << i for i in range(c_len.bit_length() - 1)]
    level_masks = [((t_idx >> b.bit_length()) == (s_idx >> b.bit_length()))
                   & ((t_idx & b) != 0) & ((s_idx & b) == 0) for b in halves]
    diag_mask = t_idx == s_idx
    ones_rhs = jnp.ones((HG_DIM, c_len), BF16)
    sub = lax.broadcasted_iota(jnp.int32, (c_len, HG_WIDTH), 0) & (SUBLANES - 1)
    pair = sub & 6

    def chunk(c):
        rows = pl.ds(pl.multiple_of(c * c_len, c_len), c_len)
        q = q_ref[0, rows, :].astype(F32)
        v = i_ref[0, rows, :]
        f = lb_all + (1.0 - lb_all) * _sigmoid(f_ref[0, rows, :])
        k = 1.0 - f

        operands = [(q * f, k)]
        fr = [_bcast_rows(f, SUBLANES, r) for r in range(SUBLANES)]
        pq = f * jnp.where(sub == 1, fr[0], jnp.where(sub == 3, fr[2],
                           jnp.where(sub == 5, fr[4], jnp.where(sub == 7, fr[6], 1.0))))
        sk = jnp.where(sub == 0, fr[1], jnp.where(sub == 2, fr[3],
                       jnp.where(sub == 4, fr[5], jnp.where(sub == 6, fr[7], 1.0))))
        operands.append((q * pq, k * sk))
        r1, r3, r5, r7 = (_bcast_rows(pq, SUBLANES, r) for r in (1, 3, 5, 7))
        sk = sk * jnp.where(pair == 0, r3, jnp.where(pair == 4, r7, 1.0))
        pq = pq * jnp.where(pair == 2, r1, jnp.where(pair == 6, r5, 1.0))
        operands.append((q * pq, k * sk))
        r3, r7 = _bcast_rows(pq, SUBLANES, 3), _bcast_rows(pq, SUBLANES, 7)
        sk = sk * jnp.where(sub < 4, r7, 1.0)
        pq = pq * jnp.where(sub >= 4, r3, 1.0)
        b = SUBLANES
        while b < c_len:
            n = c_len // b
            tot = _bcast_rows(pq, b, b - 1)
            zero = jnp.zeros((b, HG_WIDTH), F32)

            def blk(x, i, b=b):
                return x[i * b:(i + 1) * b]

            operands.append((
                jnp.concatenate([zero if i % 2 == 0 else blk(q, i) * blk(pq, i) for i in range(n)], axis=0),
                jnp.concatenate([blk(k, i) * blk(sk, i) if i % 2 == 0 else zero for i in range(n)], axis=0)))
            pq, sk = (
                jnp.concatenate([blk(pq, i) if i % 2 == 0 else blk(pq, i) * blk(tot, i - 1)
                                 for i in range(n)], axis=0),
                jnp.concatenate([blk(sk, i) * blk(tot, i + 1) if i % 2 == 0 else blk(sk, i)
                                 for i in range(n)], axis=0))
            b *= 2
        operands = [(a.astype(BF16), kk.astype(BF16)) for a, kk in operands]
        qk = (q * k).astype(BF16)
        qg = (q * pq).astype(BF16)
        kg = (k * sk).astype(BF16)
        chunk_decay = pq[c_len - 1:c_len, :]

        outs = []
        for h in range(HG_HEADS):
            cols = slice(h * HG_DIM, (h + 1) * HG_DIM)
            a = jnp.where(diag_mask, _dot(qk[:, cols], ones_rhs), 0.0)
            for (ql, kl), mask in zip(operands, level_masks):
                a = jnp.where(mask, _dot_nt(ql[:, cols], kl[:, cols]), a)
            st = st_ref[h]
            o = _dot(a.astype(BF16), v[:, cols]) + _dot_nt(qg[:, cols], st.astype(BF16))
            st_ref[h] = st * chunk_decay[:, cols] + _dot_tn(v[:, cols], kg[:, cols])
            outs.append(o * lax.rsqrt(jnp.mean(o * o, axis=-1, keepdims=True) + EPS))
        og = og_ref[0, rows, :]
        o_ref[0, rows, :] = (jnp.concatenate(outs, axis=1) * gain_all * (og * _sigmoid(og))).astype(BF16)

    def trip(i, carry):
        for u in range(HG_UNROLL):
            chunk(i * HG_UNROLL + u)
        return carry

    lax.fori_loop(0, HG_STEP // (c_len * HG_UNROLL), trip, 0)


def _hgrn2(proj_a, proj_b, lower_bounds, out_gain):
    b, seq, _ = proj_a.shape
    blk = (1, HG_STEP, HG_WIDTH)
    return pl.pallas_call(
        _hgrn2_kernel,
        grid=(b, seq // HG_STEP),
        in_specs=[
            pl.BlockSpec(blk, lambda i, j: (i, j, 0)),
            pl.BlockSpec(blk, lambda i, j: (i, j, 1)),
            pl.BlockSpec(blk, lambda i, j: (i, j, 0)),
            pl.BlockSpec(blk, lambda i, j: (i, j, 1)),
            _resident(lower_bounds.shape),
            _resident((1, HG_WIDTH)),
        ],
        out_specs=pl.BlockSpec(blk, lambda i, j: (i, j, 0)),
        out_shape=jax.ShapeDtypeStruct((b, seq, HG_WIDTH), BF16),
        scratch_shapes=[pltpu.VMEM((HG_HEADS, HG_DIM, HG_DIM), F32)],
        compiler_params=pltpu.CompilerParams(
            dimension_semantics=("parallel", "arbitrary"), vmem_limit_bytes=VMEM_LIMIT),
        name="hgrn2",
    )(proj_a, proj_a, proj_b, proj_b, lower_bounds, out_gain)


def _att_kernel(slope_ref, *refs, seq):
    qkv_refs = refs[:3 * N_GROUPS]
    y_ref = refs[3 * N_GROUPS]
    scr = refs[3 * N_GROUPS + 1:]
    o_scr, m_scr, d_scr = scr[0:N_GROUPS], scr[N_GROUPS:2 * N_GROUPS], scr[2 * N_GROUPS:3 * N_GROUPS]
    bias_scr = scr[3 * N_GROUPS]
    blk = ATT_BLOCK
    n_blocks = seq // blk

    qi = lax.broadcasted_iota(jnp.int32, (blk, 2 * blk), 0)
    kj = lax.broadcasted_iota(jnp.int32, (blk, 2 * blk), 1)
    dist = qi + blk - kj
    in_window = (dist >= 0) & (dist <= blk)
    in_window_first = in_window & (kj >= blk)
    dist_f = dist.astype(F32)
    for gh in range(2 * N_GROUPS):
        alibi = -slope_ref[0, gh:gh + 1, :] * dist_f
        bias_scr[2 * gh] = jnp.where(in_window, alibi, NEG_INF)
        bias_scr[2 * gh + 1] = jnp.where(in_window_first, alibi, NEG_INF)

    lane = lax.broadcasted_iota(jnp.int32, (blk, LANES), 1)
    first_head = lane < ATT_HEAD_DIM
    ones2 = jnp.ones((2 * blk, LANES), BF16)

    for g in range(N_GROUPS):
        d = ATT_DILATIONS[g]
        per_class = n_blocks // d
        q_ref, k_ref, v_ref = qkv_refs[3 * g:3 * g + 3]

        def block(idx, g=g, d=d, per_class=per_class, q_ref=q_ref, k_ref=k_ref, v_ref=v_ref):
            n = idx & (per_class - 1)
            r = idx >> (per_class.bit_length() - 1)
            rows = pl.ds(pl.multiple_of(idx * blk, blk), blk)
            prows = pl.ds(pl.multiple_of(jnp.maximum(idx - 1, 0) * blk, blk), blk)
            q = q_ref[0, rows, :]
            k = jnp.concatenate([k_ref[0, prows, :], k_ref[0, rows, :]], axis=0)
            v = jnp.concatenate([v_ref[0, prows, :], v_ref[0, rows, :]], axis=0)
            v_aug = jnp.concatenate([v, ones2], axis=1)
            is_first = jnp.where(n == 0, 1, 0)
            pvs, ms = [], []
            for hh in range(2):
                qm = jnp.where(first_head if hh == 0 else ~first_head, q, jnp.zeros_like(q))
                s = _dot_nt(qm, k) + bias_scr[2 * (2 * g + hh) + is_first]
                m = jnp.max(jnp.maximum(s[:, :blk], s[:, blk:]), axis=-1, keepdims=True)
                pvs.append(_dot(jnp.exp(s - m).astype(BF16), v_aug))
                ms.append(jnp.broadcast_to(m, (blk, LANES)))
            o = jnp.where(first_head, pvs[0][:, :LANES], pvs[1][:, :LANES])
            den = jnp.where(first_head, pvs[0][:, LANES:], pvs[1][:, LANES:])
            mx = jnp.where(first_head, ms[0], ms[1])
            if d == 1:
                dst = rows
            else:
                dst = pl.ds(n * (blk * d) + r, blk, stride=d)
            o_scr[g][dst, :] = o
            d_scr[g][dst, :] = den
            m_scr[g][dst, :] = mx

        def trip(i, carry, block=block):
            for u in range(ATT_UNROLL):
                block(i * ATT_UNROLL + u)
            return carry

        lax.fori_loop(0, n_blocks // ATT_UNROLL, trip, 0)

    rc = 256
    for c in range(seq // rc):
        rows = slice(c * rc, (c + 1) * rc)
        ms = [m_scr[g][rows, :] for g in range(N_GROUPS)]
        m = jnp.maximum(jnp.maximum(ms[0], ms[1]), ms[2])
        ws = [jnp.exp(mg - m) for mg in ms]
        num = ws[0] * o_scr[0][rows, :] + ws[1] * o_scr[1][rows, :] + ws[2] * o_scr[2][rows, :]
        den = ws[0] * d_scr[0][rows, :] + ws[1] * d_scr[1][rows, :] + ws[2] * d_scr[2][rows, :]
        y_ref[0, rows, :] = (num / den).astype(BF16)


def _att(proj_a, slopes):
    b, seq, _ = proj_a.shape
    pairs = ATT_WIDTH // LANES
    base = 2 * HG_WIDTH // LANES
    per_tensor = ATT_WIDTH // LANES

    def qkv_spec(g, t):
        off = base + (3 * g + t) * per_tensor
        return pl.BlockSpec((1, seq, LANES), lambda i, j, off=off: (i, 0, off + j))

    return pl.pallas_call(
        functools.partial(_att_kernel, seq=seq),
        grid=(b, pairs),
        in_specs=[pl.BlockSpec((1, SUBLANES, 2 * ATT_BLOCK), lambda i, j: (j, 0, 0))]
        + [qkv_spec(g, t) for g in range(N_GROUPS) for t in range(3)],
        out_specs=pl.BlockSpec((1, seq, LANES), lambda i, j: (i, 0, j)),
        out_shape=jax.ShapeDtypeStruct((b, seq, ATT_WIDTH), BF16),
        scratch_shapes=[pltpu.VMEM((seq, LANES), F32) for _ in range(3 * N_GROUPS)]
        + [pltpu.VMEM((4 * N_GROUPS, ATT_BLOCK, 2 * ATT_BLOCK), F32)],
        compiler_params=pltpu.CompilerParams(
            dimension_semantics=("parallel", "parallel"), vmem_limit_bytes=VMEM_LIMIT),
        name="att",
    )(slopes, *([proj_a] * (3 * N_GROUPS)))


def _tail_kernel(h1_ref, yhg_ref, yatt_ref, ghg_ref, gatt_ref, wa_ref, wb_ref, wo_ref,
                 g2_ref, wgu_ref, wd_ref, gf_ref, o_ref):
    merged = (_sigmoid(ghg_ref[...]) * _dot(yhg_ref[...], wa_ref[...])
              + _sigmoid(gatt_ref[...]) * _dot(yatt_ref[...], wb_ref[...]))
    h2 = h1_ref[...] + _dot(merged.astype(BF16), wo_ref[...])
    xn = _rmsnorm(h2, g2_ref[...]).astype(BF16)
    h3 = h2 + 0.5 * _swiglu(xn, wgu_ref, wd_ref)
    o_ref[...] = _rmsnorm(h3, gf_ref[...])


def _tail(h1, y_hg, y_att, proj_b, wa, wb, wo, g2, wgu, wd, gf):
    t = h1.shape[0]
    row = lambda width, col=0: pl.BlockSpec((ROW_TILE, width), lambda i, col=col: (i, col))
    return pl.pallas_call(
        _tail_kernel,
        grid=(t // ROW_TILE,),
        in_specs=[
            row(D_MODEL), row(HG_WIDTH), row(ATT_WIDTH),
            row(D_MODEL, 1), row(D_MODEL, 2),
            _resident((HG_WIDTH, D_MODEL)), _resident((ATT_WIDTH, D_MODEL)), _resident((D_MODEL, D_MODEL)),
            _resident((1, D_MODEL)), _resident((D_MODEL, 2 * D_FF)), _resident((D_FF, D_MODEL)),
            _resident((1, D_MODEL)),
        ],
        out_specs=row(D_MODEL),
        out_shape=jax.ShapeDtypeStruct((t, D_MODEL), F32),
        compiler_params=pltpu.CompilerParams(
            dimension_semantics=("parallel",), vmem_limit_bytes=VMEM_LIMIT),
        name="tail",
    )(h1, y_hg, y_att, proj_b, proj_b, wa, wb, wo, g2, wgu, wd, gf)


def _alibi_table():
    n_heads = N_GROUPS * ATT_HEADS
    slopes = jnp.exp2(-ALIBI_MAX * jnp.arange(1, n_heads + 1, dtype=F32) / n_heads)
    slopes = slopes.reshape(N_GROUPS, ATT_HEADS // 2, 2) * jnp.asarray(ATT_DILATIONS, F32)[:, None, None]
    table = jnp.zeros((ATT_HEADS // 2, SUBLANES, 2 * ATT_BLOCK), F32)
    table = table.at[:, :2 * N_GROUPS, :].set(
        jnp.broadcast_to(slopes.transpose(1, 0, 2).reshape(ATT_HEADS // 2, 2 * N_GROUPS, 1),
                         (ATT_HEADS // 2, 2 * N_GROUPS, 2 * ATT_BLOCK)))
    return table


def _arrange_w_in(w_in):
    kw = HG_WIDTH
    hg_q, hg_f, hg_i, hg_og = (w_in[:, i * kw:(i + 1) * kw] for i in range(4))
    att = w_in[:, 4 * kw:4 * kw + 3 * N_GROUPS * ATT_WIDTH]
    gates = w_in[:, 4 * kw + 3 * N_GROUPS * ATT_WIDTH:]
    att = att.reshape(D_MODEL, N_GROUPS, 3, ATT_WIDTH)
    att = att.at[:, :, 0, :].multiply(ATT_HEAD_DIM ** -0.5).reshape(D_MODEL, -1)
    return jnp.concatenate([hg_q, hg_i, att, hg_f, hg_og, gates], axis=1).astype(BF16)


def kernel(x, ffn1_norm, ffn1_w_gate_up, ffn1_w_down, mix_norm, w_in, hg_lower_bounds, hg_out_norm,
           w_branch_hg, w_branch_att, w_out, ffn2_norm, ffn2_w_gate_up, ffn2_w_down, final_norm):
    b, seq, d = x.shape
    assert d == D_MODEL and seq % (ATT_BLOCK * ATT_DILATIONS[-1]) == 0 and (b * seq) % ROW_TILE == 0
    x2d = x.reshape(b * seq, d)

    h1 = _ffn1(x2d, ffn1_norm[0:1], ffn1_w_gate_up[0].astype(BF16), ffn1_w_down[0].astype(BF16))
    proj_a, proj_b = _proj(h1.reshape(b, seq, d), mix_norm[0:1], _arrange_w_in(w_in[0]))
    y_hg = _hgrn2(proj_a, proj_b, hg_lower_bounds.astype(F32), hg_out_norm[0:1])
    y_att = _att(proj_a, _alibi_table())
    out = _tail(h1, y_hg.reshape(b * seq, -1), y_att.reshape(b * seq, -1),
                proj_b.reshape(b * seq, -1),
                w_branch_hg[0].astype(BF16), w_branch_att[0].astype(BF16), w_out[0].astype(BF16),
                ffn2_norm[0:1], ffn2_w_gate_up[0].astype(BF16), ffn2_w_down[0].astype(BF16),
                final_norm.reshape(1, d))
    return out.reshape(b, seq, d)
```

```python
import functools

import jax
import jax.numpy as jnp
from jax import lax
from jax.experimental import pallas as pl
from jax.experimental.pallas import tpu as pltpu

F32 = jnp.float32
BF16 = jnp.bfloat16

D_MODEL = 1024
D_FF = 2816
HG_HEADS = 4
HG_DIM = 128
HG_WIDTH = HG_HEADS * HG_DIM
ATT_DILATIONS = (1, 4, 16)
ATT_BLOCK = 128
ATT_HEADS = 8
ATT_HEAD_DIM = 64
ATT_WIDTH = ATT_HEADS * ATT_HEAD_DIM
N_GROUPS = 3
ALIBI_MAX = 8.0
EPS = 1e-6
NEG_INF = -1e30

LANES = 128
SUBLANES = 8
MXU_DIM = 256
VMEM_LIMIT = 56 * 1024 * 1024

ROW_TILE = 512
COL_BLOCK = 512
NAT_BF16_COLS = 2 * HG_WIDTH + 3 * ATT_WIDTH
NAT_F32_COLS = 2 * HG_WIDTH + 2 * D_MODEL
HG_CHUNK = 64
HG_STEP = 512
HG_UNROLL = 2
ATT_UNROLL = 8


def _sigmoid(x):
    return 1.0 / (1.0 + jnp.exp(-x))


def _rmsnorm(x, gain):
    ms = jnp.mean(x * x, axis=-1, keepdims=True)
    return x * lax.rsqrt(ms + EPS) * gain


def _dot(a, b):
    return jnp.dot(a, b, preferred_element_type=F32)


def _dot_nt(a, b):
    return lax.dot_general(a, b, (((1,), (1,)), ((), ())), preferred_element_type=F32)


def _dot_tn(a, b):
    return lax.dot_general(a, b, (((0,), (0,)), ((), ())), preferred_element_type=F32)


def _swiglu(xn_bf16, wgu_ref, wd_ref):
    split = (D_FF // (2 * MXU_DIM) + 1) * MXU_DIM
    y = None
    for lo, hi in ((0, split), (split, D_FF)):
        a = _dot(xn_bf16, wgu_ref[:, lo:hi])
        b = _dot(xn_bf16, wgu_ref[:, D_FF + lo:D_FF + hi])
        act = (a * _sigmoid(a) * b).astype(BF16)
        part = _dot(act, wd_ref[lo:hi, :])
        y = part if y is None else y + part
    return y


def _ffn1_kernel(x_ref, g_ref, wgu_ref, wd_ref, o_ref):
    x = x_ref[...]
    xn = _rmsnorm(x, g_ref[...]).astype(BF16)
    o_ref[...] = x + 0.5 * _swiglu(xn, wgu_ref, wd_ref)


def _resident(shape):
    return pl.BlockSpec(shape, lambda *_: (0,) * len(shape), pipeline_mode=pl.Buffered(1))


def _ffn1(x2d, gain, wgu, wd):
    t = x2d.shape[0]
    return pl.pallas_call(
        _ffn1_kernel,
        grid=(t // ROW_TILE,),
        in_specs=[
            pl.BlockSpec((ROW_TILE, D_MODEL), lambda i: (i, 0)),
            _resident((1, D_MODEL)),
            _resident((D_MODEL, 2 * D_FF)),
            _resident((D_FF, D_MODEL)),
        ],
        out_specs=pl.BlockSpec((ROW_TILE, D_MODEL), lambda i: (i, 0)),
        out_shape=jax.ShapeDtypeStruct((t, D_MODEL), F32),
        compiler_params=pltpu.CompilerParams(
            dimension_semantics=("parallel",), vmem_limit_bytes=VMEM_LIMIT),
        name="ffn1",
    )(x2d, gain, wgu, wd)


def _proj_nat_kernel(h_ref, g_ref, w_ref, oa_ref, ob_ref, slab_ref):
    u = _rmsnorm(h_ref[...], g_ref[...])
    for s in range(D_MODEL // LANES):
        slab_ref[s] = u[:, s * LANES:(s + 1) * LANES]
    ub = u.astype(BF16)
    for c in range(NAT_BF16_COLS // COL_BLOCK):
        cols = slice(c * COL_BLOCK, (c + 1) * COL_BLOCK)
        oa_ref[:, cols] = _dot(ub, w_ref[:, cols]).astype(BF16)
    for c in range(NAT_F32_COLS // COL_BLOCK):
        ob_ref[:, c * COL_BLOCK:(c + 1) * COL_BLOCK] = _dot(
            ub, w_ref[:, NAT_BF16_COLS + c * COL_BLOCK:NAT_BF16_COLS + (c + 1) * COL_BLOCK])


def _proj_nat(h1, gain, w_nat):
    t = h1.shape[0]
    n_slabs = D_MODEL // LANES
    return pl.pallas_call(
        _proj_nat_kernel,
        grid=(t // ROW_TILE,),
        in_specs=[
            pl.BlockSpec((ROW_TILE, D_MODEL), lambda i: (i, 0)),
            _resident((1, D_MODEL)),
            _resident((D_MODEL, NAT_BF16_COLS + NAT_F32_COLS)),
        ],
        out_specs=[
            pl.BlockSpec((ROW_TILE, NAT_BF16_COLS), lambda i: (i, 0)),
            pl.BlockSpec((ROW_TILE, NAT_F32_COLS), lambda i: (i, 0)),
            pl.BlockSpec((n_slabs, ROW_TILE, LANES), lambda i: (0, i, 0)),
        ],
        out_shape=[
            jax.ShapeDtypeStruct((t, NAT_BF16_COLS), BF16),
            jax.ShapeDtypeStruct((t, NAT_F32_COLS), F32),
            jax.ShapeDtypeStruct((n_slabs, t, LANES), F32),
        ],
        compiler_params=pltpu.CompilerParams(
            dimension_semantics=("parallel",), vmem_limit_bytes=VMEM_LIMIT),
        name="proj_nat",
    )(h1, gain, w_nat)


def _proj_perm_kernel(slab_ref, w_ref, o_ref, *, seq):
    t = pl.program_id(1)
    n_slabs = D_MODEL // LANES
    rows_out = o_ref.shape[1]
    for g in (1, 2):
        d = ATT_DILATIONS[g]
        run = seq // d
        per_tile = rows_out // run if run < rows_out else 1
        pieces = []
        for c in range(per_tile):
            if run >= rows_out:
                r = (t * rows_out) // run
                start = r + d * ((t * rows_out) % run)
                n_rows = rows_out
            else:
                start = t * per_tile + c
                n_rows = run
            pieces.append(jnp.concatenate(
                [slab_ref[s, pl.ds(start, n_rows, stride=d), :].astype(BF16) for s in range(n_slabs)], axis=1))
        lhs = pieces[0] if len(pieces) == 1 else jnp.concatenate(pieces, axis=0)
        for c in range(3 * ATT_WIDTH // COL_BLOCK):
            cols = slice((g - 1) * 3 * ATT_WIDTH + c * COL_BLOCK, (g - 1) * 3 * ATT_WIDTH + (c + 1) * COL_BLOCK)
            o_ref[0, :, cols] = _dot(lhs, w_ref[:, cols]).astype(BF16)


def _proj_perm(u_slab, w_perm, b, seq):
    n_slabs = D_MODEL // LANES
    n_cols = 2 * 3 * ATT_WIDTH
    return pl.pallas_call(
        functools.partial(_proj_perm_kernel, seq=seq),
        grid=(b, seq // ROW_TILE),
        in_specs=[
            pl.BlockSpec((n_slabs, seq, LANES), lambda i, j: (0, i, 0)),
            _resident((D_MODEL, n_cols)),
        ],
        out_specs=pl.BlockSpec((1, ROW_TILE, n_cols), lambda i, j: (i, j, 0)),
        out_shape=jax.ShapeDtypeStruct((b, seq, n_cols), BF16),
        compiler_params=pltpu.CompilerParams(
            dimension_semantics=("parallel", "arbitrary"), vmem_limit_bytes=VMEM_LIMIT),
        name="proj_perm",
    )(u_slab, w_perm)


def _bcast_rows(x, block, row):
    n = x.shape[0] // block
    parts = [jnp.broadcast_to(x[i * block + row:i * block + row + 1, :], (block, x.shape[1]))
             for i in range(n)]
    return parts[0] if n == 1 else jnp.concatenate(parts, axis=0)


def _hgrn2_kernel(q_ref, i_ref, f_ref, og_ref, lbp_ref, gain_ref, o_ref, st_ref):
    c_len = HG_CHUNK

    @pl.when(pl.program_id(1) == 0)
    def _():
        st_ref[...] = jnp.zeros_like(st_ref)

    lbp = lbp_ref[...]
    e = jnp.exp(lbp - jnp.max(lbp, axis=0, keepdims=True))
    lb_all = e[0:1, :] / jnp.sum(e, axis=0, keepdims=True)
    gain_all = gain_ref[...]

    t_idx = lax.broadcasted_iota(jnp.int32, (c_len, c_len), 0)
    s_idx = lax.broadcasted_iota(jnp.int32, (c_len, c_len), 1)
    halves = [1 << i for i in range(c_len.bit_length() - 1)]
    level_masks = [((t_idx >> b.bit_length()) == (s_idx >> b.bit_length()))
                   & ((t_idx & b) != 0) & ((s_idx & b) == 0) for b in halves]
    diag_mask = t_idx == s_idx
    ones_rhs = jnp.ones((HG_DIM, c_len), BF16)
    sub = lax.broadcasted_iota(jnp.int32, (c_len, HG_WIDTH), 0) & (SUBLANES - 1)
    pair = sub & 6

    def chunk(c):
        rows = pl.ds(pl.multiple_of(c * c_len, c_len), c_len)
        q = q_ref[0, rows, :].astype(F32)
        v = i_ref[0, rows, :]
        f = lb_all + (1.0 - lb_all) * _sigmoid(f_ref[0, rows, :])
        k = 1.0 - f

        operands = [(q * f, k)]
        fr = [_bcast_rows(f, SUBLANES, r) for r in range(SUBLANES)]
        pq = f * jnp.where(sub == 1, fr[0], jnp.where(sub == 3, fr[2],
                           jnp.where(sub == 5, fr[4], jnp.where(sub == 7, fr[6], 1.0))))
        sk = jnp.where(sub == 0, fr[1], jnp.where(sub == 2, fr[3],
                       jnp.where(sub == 4, fr[5], jnp.where(sub == 6, fr[7], 1.0))))
        operands.append((q * pq, k * sk))
        r1, r3, r5, r7 = (_bcast_rows(pq, SUBLANES, r) for r in (1, 3, 5, 7))
        sk = sk * jnp.where(pair == 0, r3, jnp.where(pair == 4, r7, 1.0))
        pq = pq * jnp.where(pair == 2, r1, jnp.where(pair == 6, r5, 1.0))
        operands.append((q * pq, k * sk))
        r3, r7 = _bcast_rows(pq, SUBLANES, 3), _bcast_rows(pq, SUBLANES, 7)
        sk = sk * jnp.where(sub < 4, r7, 1.0)
        pq = pq * jnp.where(sub >= 4, r3, 1.0)
        b = SUBLANES
        while b < c_len:
            n = c_len // b
            tot = _bcast_rows(pq, b, b - 1)
            zero = jnp.zeros((b, HG_WIDTH), F32)

            def blk(x, i, b=b):
                return x[i * b:(i + 1) * b]

            operands.append((
                jnp.concatenate([zero if i % 2 == 0 else blk(q, i) * blk(pq, i) for i in range(n)], axis=0),
                jnp.concatenate([blk(k, i) * blk(sk, i) if i % 2 == 0 else zero for i in range(n)], axis=0)))
            pq, sk = (
                jnp.concatenate([blk(pq, i) if i % 2 == 0 else blk(pq, i) * blk(tot, i - 1)
                                 for i in range(n)], axis=0),
                jnp.concatenate([blk(sk, i) * blk(tot, i + 1) if i % 2 == 0 else blk(sk, i)
                                 for i in range(n)], axis=0))
            b *= 2
        operands = [(a.astype(BF16), kk.astype(BF16)) for a, kk in operands]
        qk = (q * k).astype(BF16)
        qg = (q * pq).astype(BF16)
        kg = (k * sk).astype(BF16)
        chunk_decay = pq[c_len - 1:c_len, :]

        outs = []
        for h in range(HG_HEADS):
            cols = slice(h * HG_DIM, (h + 1) * HG_DIM)
            a = jnp.where(diag_mask, _dot(qk[:, cols], ones_rhs), 0.0)
            for (ql, kl), mask in zip(operands, level_masks):
                a = jnp.where(mask, _dot_nt(ql[:, cols], kl[:, cols]), a)
            st = st_ref[h]
            o = _dot(a.astype(BF16), v[:, cols]) + _dot_nt(qg[:, cols], st.astype(BF16))
            st_ref[h] = st * chunk_decay[:, cols] + _dot_tn(v[:, cols], kg[:, cols])
            outs.append(o * lax.rsqrt(jnp.mean(o * o, axis=-1, keepdims=True) + EPS))
        og = og_ref[0, rows, :]
        o_ref[0, rows, :] = (jnp.concatenate(outs, axis=1) * gain_all * (og * _sigmoid(og))).astype(BF16)

    def trip(i, carry):
        for u in range(HG_UNROLL):
            chunk(i * HG_UNROLL + u)
        return carry

    lax.fori_loop(0, HG_STEP // (c_len * HG_UNROLL), trip, 0)


def _hgrn2(proj_a, proj_b, lower_bounds, out_gain):
    b, seq, _ = proj_a.shape
    blk = (1, HG_STEP, HG_WIDTH)
    return pl.pallas_call(
        _hgrn2_kernel,
        grid=(b, seq // HG_STEP),
        in_specs=[
            pl.BlockSpec(blk, lambda i, j: (i, j, 0)),
            pl.BlockSpec(blk, lambda i, j: (i, j, 1)),
            pl.BlockSpec(blk, lambda i, j: (i, j, 0)),
            pl.BlockSpec(blk, lambda i, j: (i, j, 1)),
            _resident(lower_bounds.shape),
            _resident((1, HG_WIDTH)),
        ],
        out_specs=pl.BlockSpec(blk, lambda i, j: (i, j, 0)),
        out_shape=jax.ShapeDtypeStruct((b, seq, HG_WIDTH), BF16),
        scratch_shapes=[pltpu.VMEM((HG_HEADS, HG_DIM, HG_DIM), F32)],
        compiler_params=pltpu.CompilerParams(
            dimension_semantics=("parallel", "arbitrary"), vmem_limit_bytes=VMEM_LIMIT),
        name="hgrn2",
    )(proj_a, proj_a, proj_b, proj_b, lower_bounds, out_gain)


def _att_kernel(slope_ref, *refs, seq):
    qkv_refs = refs[:3 * N_GROUPS]
    y_ref = refs[3 * N_GROUPS]
    scr = refs[3 * N_GROUPS + 1:]
    o_scr, m_scr, d_scr = scr[0:N_GROUPS], scr[N_GROUPS:2 * N_GROUPS], scr[2 * N_GROUPS:3 * N_GROUPS]
    bias_scr = scr[3 * N_GROUPS]
    blk = ATT_BLOCK
    n_blocks = seq // blk

    qi = lax.broadcasted_iota(jnp.int32, (blk, 2 * blk), 0)
    kj = lax.broadcasted_iota(jnp.int32, (blk, 2 * blk), 1)
    dist = qi + blk - kj
    in_window = (dist >= 0) & (dist <= blk)
    in_window_first = in_window & (kj >= blk)
    dist_f = dist.astype(F32)
    for gh in range(2 * N_GROUPS):
        alibi = -slope_ref[0, gh:gh + 1, :] * dist_f
        bias_scr[2 * gh] = jnp.where(in_window, alibi, NEG_INF)
        bias_scr[2 * gh + 1] = jnp.where(in_window_first, alibi, NEG_INF)

    lane = lax.broadcasted_iota(jnp.int32, (blk, LANES), 1)
    first_head = lane < ATT_HEAD_DIM
    ones2 = jnp.ones((2 * blk, LANES), BF16)

    for g in range(N_GROUPS):
        d = ATT_DILATIONS[g]
        per_class = n_blocks // d
        q_ref, k_ref, v_ref = qkv_refs[3 * g:3 * g + 3]

        def block(idx, g=g, d=d, per_class=per_class, q_ref=q_ref, k_ref=k_ref, v_ref=v_ref):
            n = idx & (per_class - 1)
            r = idx >> (per_class.bit_length() - 1)
            rows = pl.ds(pl.multiple_of(idx * blk, blk), blk)
            prows = pl.ds(pl.multiple_of(jnp.maximum(idx - 1, 0) * blk, blk), blk)
            q = q_ref[0, rows, :]
            k = jnp.concatenate([k_ref[0, prows, :], k_ref[0, rows, :]], axis=0)
            v = jnp.concatenate([v_ref[0, prows, :], v_ref[0, rows, :]], axis=0)
            v_aug = jnp.concatenate([v, ones2], axis=1)
            is_first = jnp.where(n == 0, 1, 0)
            pvs, ms = [], []
            for hh in range(2):
                qm = jnp.where(first_head if hh == 0 else ~first_head, q, jnp.zeros_like(q))
                s = _dot_nt(qm, k) + bias_scr[2 * (2 * g + hh) + is_first]
                m = jnp.max(jnp.maximum(s[:, :blk], s[:, blk:]), axis=-1, keepdims=True)
                pvs.append(_dot(jnp.exp(s - m).astype(BF16), v_aug))
                ms.append(jnp.broadcast_to(m, (blk, LANES)))
            o = jnp.where(first_head, pvs[0][:, :LANES], pvs[1][:, :LANES])
            den = jnp.where(first_head, pvs[0][:, LANES:], pvs[1][:, LANES:])
            mx = jnp.where(first_head, ms[0], ms[1])
            if d == 1:
                dst = rows
            else:
                dst = pl.ds(n * (blk * d) + r, blk, stride=d)
            o_scr[g][dst, :] = o
            d_scr[g][dst, :] = den
            m_scr[g][dst, :] = mx

        def trip(i, carry, block=block):
            for u in range(ATT_UNROLL):
                block(i * ATT_UNROLL + u)
            return carry

        lax.fori_loop(0, n_blocks // ATT_UNROLL, trip, 0)

    rc = 256
    for c in range(seq // rc):
        rows = slice(c * rc, (c + 1) * rc)
        ms = [m_scr[g][rows, :] for g in range(N_GROUPS)]
        m = jnp.maximum(jnp.maximum(ms[0], ms[1]), ms[2])
        ws = [jnp.exp(mg - m) for mg in ms]
        num = ws[0] * o_scr[0][rows, :] + ws[1] * o_scr[1][rows, :] + ws[2] * o_scr[2][rows, :]
        den = ws[0] * d_scr[0][rows, :] + ws[1] * d_scr[1][rows, :] + ws[2] * d_scr[2][rows, :]
        y_ref[0, rows, :] = (num / den).astype(BF16)


def _att(proj_nat, proj_perm, slopes):
    b, seq, _ = proj_nat.shape
    pairs = ATT_WIDTH // LANES
    per_tensor = ATT_WIDTH // LANES

    def qkv_spec(g, t):
        off = (2 * HG_WIDTH // LANES + t * per_tensor) if g == 0 else (3 * (g - 1) + t) * per_tensor
        return pl.BlockSpec((1, seq, LANES), lambda i, j, off=off: (i, 0, off + j))

    return pl.pallas_call(
        functools.partial(_att_kernel, seq=seq),
        grid=(b, pairs),
        in_specs=[pl.BlockSpec((1, SUBLANES, 2 * ATT_BLOCK), lambda i, j: (j, 0, 0))]
        + [qkv_spec(g, t) for g in range(N_GROUPS) for t in range(3)],
        out_specs=pl.BlockSpec((1, seq, LANES), lambda i, j: (i, 0, j)),
        out_shape=jax.ShapeDtypeStruct((b, seq, ATT_WIDTH), BF16),
        scratch_shapes=[pltpu.VMEM((seq, LANES), F32) for _ in range(3 * N_GROUPS)]
        + [pltpu.VMEM((4 * N_GROUPS, ATT_BLOCK, 2 * ATT_BLOCK), F32)],
        compiler_params=pltpu.CompilerParams(
            dimension_semantics=("parallel", "parallel"), vmem_limit_bytes=VMEM_LIMIT),
        name="att",
    )(slopes, *([proj_nat] * 3 + [proj_perm] * (3 * (N_GROUPS - 1))))


def _tail_kernel(h1_ref, yhg_ref, yatt_ref, ghg_ref, gatt_ref, wa_ref, wb_ref, wo_ref,
                 g2_ref, wgu_ref, wd_ref, gf_ref, o_ref):
    merged = (_sigmoid(ghg_ref[...]) * _dot(yhg_ref[...], wa_ref[...])
              + _sigmoid(gatt_ref[...]) * _dot(yatt_ref[...], wb_ref[...]))
    h2 = h1_ref[...] + _dot(merged.astype(BF16), wo_ref[...])
    xn = _rmsnorm(h2, g2_ref[...]).astype(BF16)
    h3 = h2 + 0.5 * _swiglu(xn, wgu_ref, wd_ref)
    o_ref[...] = _rmsnorm(h3, gf_ref[...])


def _tail(h1, y_hg, y_att, proj_b, wa, wb, wo, g2, wgu, wd, gf):
    t = h1.shape[0]
    row = lambda width, col=0: pl.BlockSpec((ROW_TILE, width), lambda i, col=col: (i, col))
    return pl.pallas_call(
        _tail_kernel,
        grid=(t // ROW_TILE,),
        in_specs=[
            row(D_MODEL), row(HG_WIDTH), row(ATT_WIDTH),
            row(D_MODEL, 1), row(D_MODEL, 2),
            _resident((HG_WIDTH, D_MODEL)), _resident((ATT_WIDTH, D_MODEL)), _resident((D_MODEL, D_MODEL)),
            _resident((1, D_MODEL)), _resident((D_MODEL, 2 * D_FF)), _resident((D_FF, D_MODEL)),
            _resident((1, D_MODEL)),
        ],
        out_specs=row(D_MODEL),
        out_shape=jax.ShapeDtypeStruct((t, D_MODEL), F32),
        compiler_params=pltpu.CompilerParams(
            dimension_semantics=("parallel",), vmem_limit_bytes=VMEM_LIMIT),
        name="tail",
    )(h1, y_hg, y_att, proj_b, proj_b, wa, wb, wo, g2, wgu, wd, gf)


def _alibi_table():
    n_heads = N_GROUPS * ATT_HEADS
    slopes = jnp.exp2(-ALIBI_MAX * jnp.arange(1, n_heads + 1, dtype=F32) / n_heads)
    slopes = slopes.reshape(N_GROUPS, ATT_HEADS // 2, 2) * jnp.asarray(ATT_DILATIONS, F32)[:, None, None]
    table = jnp.zeros((ATT_HEADS // 2, SUBLANES, 2 * ATT_BLOCK), F32)
    table = table.at[:, :2 * N_GROUPS, :].set(
        jnp.broadcast_to(slopes.transpose(1, 0, 2).reshape(ATT_HEADS // 2, 2 * N_GROUPS, 1),
                         (ATT_HEADS // 2, 2 * N_GROUPS, 2 * ATT_BLOCK)))
    return table


def _arrange_w_in(w_in):
    kw = HG_WIDTH
    hg_q, hg_f, hg_i, hg_og = (w_in[:, i * kw:(i + 1) * kw] for i in range(4))
    gates = w_in[:, 4 * kw + 3 * N_GROUPS * ATT_WIDTH:]
    scale = ATT_HEAD_DIM ** -0.5
    groups = []
    for g in range(N_GROUPS):
        base = 4 * kw + 3 * g * ATT_WIDTH
        groups += [w_in[:, base:base + ATT_WIDTH] * scale, w_in[:, base + ATT_WIDTH:base + 3 * ATT_WIDTH]]
    w_nat = jnp.concatenate([hg_q, hg_i] + groups[:2] + [hg_f, hg_og, gates], axis=1).astype(BF16)
    w_perm = jnp.concatenate(groups[2:], axis=1).astype(BF16)
    return w_nat, w_perm


def kernel(x, ffn1_norm, ffn1_w_gate_up, ffn1_w_down, mix_norm, w_in, hg_lower_bounds, hg_out_norm,
           w_branch_hg, w_branch_att, w_out, ffn2_norm, ffn2_w_gate_up, ffn2_w_down, final_norm):
    b, seq, d = x.shape
    assert d == D_MODEL and seq % (ATT_BLOCK * ATT_DILATIONS[-1]) == 0 and (b * seq) % ROW_TILE == 0
    x2d = x.reshape(b * seq, d)

    h1 = _ffn1(x2d, ffn1_norm[0:1], ffn1_w_gate_up[0].astype(BF16), ffn1_w_down[0].astype(BF16))
    w_nat, w_perm = _arrange_w_in(w_in[0])
    proj_a, proj_b, u_slab = _proj_nat(h1, mix_norm[0:1], w_nat)
    proj_perm = _proj_perm(u_slab, w_perm, b, seq)
    proj_a = proj_a.reshape(b, seq, -1)
    y_hg = _hgrn2(proj_a, proj_b.reshape(b, seq, -1), hg_lower_bounds.astype(F32), hg_out_norm[0:1])
    y_att = _att(proj_a, proj_perm, _alibi_table())
    out = _tail(h1, y_hg.reshape(b * seq, -1), y_att.reshape(b * seq, -1),
                proj_b,
                w_branch_hg[0].astype(BF16), w_branch_att[0].astype(BF16), w_out[0].astype(BF16),
                ffn2_norm[0:1], ffn2_w_gate_up[0].astype(BF16), ffn2_w_down[0].astype(BF16),
                final_norm.reshape(1, d))
    return out.reshape(b, seq, d)
```

```python
import functools

import jax
import jax.numpy as jnp
from jax import lax
from jax.experimental import pallas as pl
from jax.experimental.pallas import tpu as pltpu

F32 = jnp.float32
BF16 = jnp.bfloat16

D_MODEL = 1024
D_FF = 2816
HG_HEADS = 4
HG_DIM = 128
HG_WIDTH = HG_HEADS * HG_DIM
ATT_DILATIONS = (1, 4, 16)
ATT_BLOCK = 128
ATT_HEADS = 8
ATT_HEAD_DIM = 64
ATT_WIDTH = ATT_HEADS * ATT_HEAD_DIM
N_GROUPS = 3
ALIBI_MAX = 8.0
EPS = 1e-6
NEG_INF = -1e30

LANES = 128
SUBLANES = 8
MXU_DIM = 256
VMEM_LIMIT = 56 * 1024 * 1024

ROW_TILE = 512
COL_BLOCK = 512
NAT_BF16_COLS = 2 * HG_WIDTH + 3 * ATT_WIDTH
NAT_F32_COLS = 2 * HG_WIDTH + 2 * D_MODEL
HG_CHUNK = 64
HG_STEP = 512
HG_UNROLL = 8


def _sigmoid(x):
    return 1.0 / (1.0 + jnp.exp(-x))


def _rmsnorm(x, gain):
    ms = jnp.mean(x * x, axis=-1, keepdims=True)
    return x * lax.rsqrt(ms + EPS) * gain


def _dot(a, b):
    return jnp.dot(a, b, preferred_element_type=F32)


def _dot_nt(a, b):
    return lax.dot_general(a, b, (((1,), (1,)), ((), ())), preferred_element_type=F32)


def _dot_tn(a, b):
    return lax.dot_general(a, b, (((0,), (0,)), ((), ())), preferred_element_type=F32)


def _swiglu(xn_bf16, wgu_ref, wd_ref):
    split = (D_FF // (2 * MXU_DIM) + 1) * MXU_DIM
    y = None
    for lo, hi in ((0, split), (split, D_FF)):
        a = _dot(xn_bf16, wgu_ref[:, lo:hi])
        b = _dot(xn_bf16, wgu_ref[:, D_FF + lo:D_FF + hi])
        act = (a * _sigmoid(a) * b).astype(BF16)
        part = _dot(act, wd_ref[lo:hi, :])
        y = part if y is None else y + part
    return y


def _ffn1_kernel(x_ref, g_ref, wgu_ref, wd_ref, o_ref):
    x = x_ref[...]
    xn = _rmsnorm(x, g_ref[...]).astype(BF16)
    o_ref[...] = x + 0.5 * _swiglu(xn, wgu_ref, wd_ref)


def _resident(shape):
    return pl.BlockSpec(shape, lambda *_: (0,) * len(shape), pipeline_mode=pl.Buffered(1))


def _ffn1(x2d, gain, wgu, wd):
    t = x2d.shape[0]
    return pl.pallas_call(
        _ffn1_kernel,
        grid=(t // ROW_TILE,),
        in_specs=[
            pl.BlockSpec((ROW_TILE, D_MODEL), lambda i: (i, 0)),
            _resident((1, D_MODEL)),
            _resident((D_MODEL, 2 * D_FF)),
            _resident((D_FF, D_MODEL)),
        ],
        out_specs=pl.BlockSpec((ROW_TILE, D_MODEL), lambda i: (i, 0)),
        out_shape=jax.ShapeDtypeStruct((t, D_MODEL), F32),
        compiler_params=pltpu.CompilerParams(
            dimension_semantics=("parallel",), vmem_limit_bytes=VMEM_LIMIT),
        name="ffn1",
    )(x2d, gain, wgu, wd)


def _proj_nat_kernel(h_ref, g_ref, w_ref, oa_ref, ob_ref, slab_ref):
    u = _rmsnorm(h_ref[...], g_ref[...])
    for s in range(D_MODEL // LANES):
        slab_ref[s] = u[:, s * LANES:(s + 1) * LANES]
    ub = u.astype(BF16)
    for c in range(NAT_BF16_COLS // COL_BLOCK):
        cols = slice(c * COL_BLOCK, (c + 1) * COL_BLOCK)
        oa_ref[:, cols] = _dot(ub, w_ref[:, cols]).astype(BF16)
    for c in range(NAT_F32_COLS // COL_BLOCK):
        ob_ref[:, c * COL_BLOCK:(c + 1) * COL_BLOCK] = _dot(
            ub, w_ref[:, NAT_BF16_COLS + c * COL_BLOCK:NAT_BF16_COLS + (c + 1) * COL_BLOCK])


def _proj_nat(h1, gain, w_nat):
    t = h1.shape[0]
    n_slabs = D_MODEL // LANES
    return pl.pallas_call(
        _proj_nat_kernel,
        grid=(t // ROW_TILE,),
        in_specs=[
            pl.BlockSpec((ROW_TILE, D_MODEL), lambda i: (i, 0)),
            _resident((1, D_MODEL)),
            _resident((D_MODEL, NAT_BF16_COLS + NAT_F32_COLS)),
        ],
        out_specs=[
            pl.BlockSpec((ROW_TILE, NAT_BF16_COLS), lambda i: (i, 0)),
            pl.BlockSpec((ROW_TILE, NAT_F32_COLS), lambda i: (i, 0)),
            pl.BlockSpec((n_slabs, ROW_TILE, LANES), lambda i: (0, i, 0)),
        ],
        out_shape=[
            jax.ShapeDtypeStruct((t, NAT_BF16_COLS), BF16),
            jax.ShapeDtypeStruct((t, NAT_F32_COLS), F32),
            jax.ShapeDtypeStruct((n_slabs, t, LANES), F32),
        ],
        compiler_params=pltpu.CompilerParams(
            dimension_semantics=("parallel",), vmem_limit_bytes=VMEM_LIMIT),
        name="proj_nat",
    )(h1, gain, w_nat)


def _proj_perm_kernel(slab_ref, w_ref, o_ref, *, seq):
    t = pl.program_id(1)
    n_slabs = D_MODEL // LANES
    rows_out = o_ref.shape[1]
    for g in (1, 2):
        d = ATT_DILATIONS[g]
        run = seq // d
        per_tile = rows_out // run if run < rows_out else 1
        pieces = []
        for c in range(per_tile):
            if run >= rows_out:
                r = (t * rows_out) // run
                start = r + d * ((t * rows_out) % run)
                n_rows = rows_out
            else:
                start = t * per_tile + c
                n_rows = run
            pieces.append(jnp.concatenate(
                [slab_ref[s, pl.ds(start, n_rows, stride=d), :].astype(BF16) for s in range(n_slabs)], axis=1))
        lhs = pieces[0] if len(pieces) == 1 else jnp.concatenate(pieces, axis=0)
        for c in range(3 * ATT_WIDTH // COL_BLOCK):
            cols = slice((g - 1) * 3 * ATT_WIDTH + c * COL_BLOCK, (g - 1) * 3 * ATT_WIDTH + (c + 1) * COL_BLOCK)
            o_ref[0, :, cols] = _dot(lhs, w_ref[:, cols]).astype(BF16)


def _proj_perm(u_slab, w_perm, b, seq):
    n_slabs = D_MODEL // LANES
    n_cols = 2 * 3 * ATT_WIDTH
    return pl.pallas_call(
        functools.partial(_proj_perm_kernel, seq=seq),
        grid=(b, seq // ROW_TILE),
        in_specs=[
            pl.BlockSpec((n_slabs, seq, LANES), lambda i, j: (0, i, 0)),
            _resident((D_MODEL, n_cols)),
        ],
        out_specs=pl.BlockSpec((1, ROW_TILE, n_cols), lambda i, j: (i, j, 0)),
        out_shape=jax.ShapeDtypeStruct((b, seq, n_cols), BF16),
        compiler_params=pltpu.CompilerParams(
            dimension_semantics=("parallel", "arbitrary"), vmem_limit_bytes=VMEM_LIMIT),
        name="proj_perm",
    )(u_slab, w_perm)


def _bcast_rows(x, block, row):
    n = x.shape[0] // block
    parts = [jnp.broadcast_to(x[i * block + row:i * block + row + 1, :], (block, x.shape[1]))
             for i in range(n)]
    return parts[0] if n == 1 else jnp.concatenate(parts, axis=0)


def _hgrn2_kernel(q_ref, i_ref, f_ref, og_ref, lbp_ref, gain_ref, o_ref, st_ref):
    c_len = HG_CHUNK

    @pl.when(pl.program_id(1) == 0)
    def _():
        st_ref[...] = jnp.zeros_like(st_ref)

    lbp = lbp_ref[...]
    e = jnp.exp(lbp - jnp.max(lbp, axis=0, keepdims=True))
    lb_all = e[0:1, :] / jnp.sum(e, axis=0, keepdims=True)
    gain_all = gain_ref[...]

    t_idx = lax.broadcasted_iota(jnp.int32, (c_len, c_len), 0)
    s_idx = lax.broadcasted_iota(jnp.int32, (c_len, c_len), 1)
    halves = [1 << i for i in range(c_len.bit_length() - 1)]
    level_masks = [((t_idx >> b.bit_length()) == (s_idx >> b.bit_length()))
                   & ((t_idx & b) != 0) & ((s_idx & b) == 0) for b in halves]
    diag_mask = t_idx == s_idx
    ones_rhs = jnp.ones((HG_DIM, c_len), BF16)
    sub = lax.broadcasted_iota(jnp.int32, (c_len, HG_WIDTH), 0) & (SUBLANES - 1)
    pair = sub & 6

    def chunk(c):
        rows = pl.ds(pl.multiple_of(c * c_len, c_len), c_len)
        q = q_ref[0, rows, :].astype(F32)
        v = i_ref[0, rows, :]
        f = lb_all + (1.0 - lb_all) * _sigmoid(f_ref[0, rows, :])
        k = 1.0 - f

        operands = [(q * f, k)]
        fr = [_bcast_rows(f, SUBLANES, r) for r in range(SUBLANES)]
        pq = f * jnp.where(sub == 1, fr[0], jnp.where(sub == 3, fr[2],
                           jnp.where(sub == 5, fr[4], jnp.where(sub == 7, fr[6], 1.0))))
        sk = jnp.where(sub == 0, fr[1], jnp.where(sub == 2, fr[3],
                       jnp.where(sub == 4, fr[5], jnp.where(sub == 6, fr[7], 1.0))))
        operands.append((q * pq, k * sk))
        r1, r3, r5, r7 = (_bcast_rows(pq, SUBLANES, r) for r in (1, 3, 5, 7))
        sk = sk * jnp.where(pair == 0, r3, jnp.where(pair == 4, r7, 1.0))
        pq = pq * jnp.where(pair == 2, r1, jnp.where(pair == 6, r5, 1.0))
        operands.append((q * pq, k * sk))
        r3, r7 = _bcast_rows(pq, SUBLANES, 3), _bcast_rows(pq, SUBLANES, 7)
        sk = sk * jnp.where(sub < 4, r7, 1.0)
        pq = pq * jnp.where(sub >= 4, r3, 1.0)
        b = SUBLANES
        while b < c_len:
            n = c_len // b
            tot = _bcast_rows(pq, b, b - 1)
            zero = jnp.zeros((b, HG_WIDTH), F32)

            def blk(x, i, b=b):
                return x[i * b:(i + 1) * b]

            operands.append((
                jnp.concatenate([zero if i % 2 == 0 else blk(q, i) * blk(pq, i) for i in range(n)], axis=0),
                jnp.concatenate([blk(k, i) * blk(sk, i) if i % 2 == 0 else zero for i in range(n)], axis=0)))
            pq, sk = (
                jnp.concatenate([blk(pq, i) if i % 2 == 0 else blk(pq, i) * blk(tot, i - 1)
                                 for i in range(n)], axis=0),
                jnp.concatenate([blk(sk, i) * blk(tot, i + 1) if i % 2 == 0 else blk(sk, i)
                                 for i in range(n)], axis=0))
            b *= 2
        operands = [(a.astype(BF16), kk.astype(BF16)) for a, kk in operands]
        qk = (q * k).astype(BF16)
        qg = (q * pq).astype(BF16)
        kg = (k * sk).astype(BF16)
        chunk_decay = pq[c_len - 1:c_len, :]

        outs = []
        for h in range(HG_HEADS):
            cols = slice(h * HG_DIM, (h + 1) * HG_DIM)
            a = jnp.where(diag_mask, _dot(qk[:, cols], ones_rhs), 0.0)
            for (ql, kl), mask in zip(operands, level_masks):
                a = jnp.where(mask, _dot_nt(ql[:, cols], kl[:, cols]), a)
            st = st_ref[h]
            o = _dot(a.astype(BF16), v[:, cols]) + _dot_nt(qg[:, cols], st.astype(BF16))
            st_ref[h] = st * chunk_decay[:, cols] + _dot_tn(v[:, cols], kg[:, cols])
            outs.append(o * lax.rsqrt(jnp.mean(o * o, axis=-1, keepdims=True) + EPS))
        og = og_ref[0, rows, :]
        o_ref[0, rows, :] = (jnp.concatenate(outs, axis=1) * gain_all * (og * _sigmoid(og))).astype(BF16)

    def trip(i, carry):
        for u in range(HG_UNROLL):
            chunk(i * HG_UNROLL + u)
        return carry

    lax.fori_loop(0, HG_STEP // (c_len * HG_UNROLL), trip, 0)


def _hgrn2(proj_a, proj_b, lower_bounds, out_gain):
    b, seq, _ = proj_a.shape
    blk = (1, HG_STEP, HG_WIDTH)
    return pl.pallas_call(
        _hgrn2_kernel,
        grid=(b, seq // HG_STEP),
        in_specs=[
            pl.BlockSpec(blk, lambda i, j: (i, j, 0)),
            pl.BlockSpec(blk, lambda i, j: (i, j, 1)),
            pl.BlockSpec(blk, lambda i, j: (i, j, 0)),
            pl.BlockSpec(blk, lambda i, j: (i, j, 1)),
            _resident(lower_bounds.shape),
            _resident((1, HG_WIDTH)),
        ],
        out_specs=pl.BlockSpec(blk, lambda i, j: (i, j, 0)),
        out_shape=jax.ShapeDtypeStruct((b, seq, HG_WIDTH), BF16),
        scratch_shapes=[pltpu.VMEM((HG_HEADS, HG_DIM, HG_DIM), F32)],
        compiler_params=pltpu.CompilerParams(
            dimension_semantics=("parallel", "arbitrary"), vmem_limit_bytes=VMEM_LIMIT),
        name="hgrn2",
    )(proj_a, proj_a, proj_b, proj_b, lower_bounds, out_gain)


def _att_kernel(slope_ref, *refs, seq):
    qkv_refs = refs[:3 * N_GROUPS]
    y_ref = refs[3 * N_GROUPS]
    scr = refs[3 * N_GROUPS + 1:]
    o_scr, l_scr = scr[0:N_GROUPS], scr[N_GROUPS:2 * N_GROUPS]
    bias_scr = scr[2 * N_GROUPS]
    blk = ATT_BLOCK
    n_blocks = seq // blk

    qi = lax.broadcasted_iota(jnp.int32, (blk, 2 * blk), 0)
    kj = lax.broadcasted_iota(jnp.int32, (blk, 2 * blk), 1)
    dist = qi + blk - kj
    in_window = (dist >= 0) & (dist <= blk)
    dist_f = dist.astype(F32)
    for gh in range(2 * N_GROUPS):
        alibi = -slope_ref[0, gh:gh + 1, :] * dist_f
        bias_scr[gh] = jnp.where(in_window, alibi, NEG_INF)

    lane = lax.broadcasted_iota(jnp.int32, (blk, LANES), 1)
    first_head = lane < ATT_HEAD_DIM
    ones2 = jnp.ones((2 * blk, LANES), BF16)

    for g in range(N_GROUPS):
        d = ATT_DILATIONS[g]
        per_class = n_blocks // d
        q_ref, k_ref, v_ref = qkv_refs[3 * g:3 * g + 3]

        for idx in range(n_blocks):
            n, r = idx % per_class, idx // per_class
            rows = slice(idx * blk, (idx + 1) * blk)
            krows = rows if n == 0 else slice((idx - 1) * blk, (idx + 1) * blk)
            q = q_ref[0, rows, :]
            k = k_ref[0, krows, :]
            v_aug = jnp.concatenate([v_ref[0, krows, :], ones2[:k.shape[0]]], axis=1)
            pvs, ms = [], []
            for hh in range(2):
                qm = jnp.where(first_head if hh == 0 else ~first_head, q, jnp.zeros_like(q))
                if n == 0:
                    s = _dot_nt(qm, k) + bias_scr[2 * g + hh, :, blk:]
                    m = jnp.max(s, axis=-1, keepdims=True)
                else:
                    s = _dot_nt(qm, k) + bias_scr[2 * g + hh]
                    m = jnp.max(jnp.maximum(s[:, :blk], s[:, blk:]), axis=-1, keepdims=True)
                pvs.append(_dot(jnp.exp(s - m).astype(BF16), v_aug))
                ms.append(jnp.broadcast_to(m, (blk, LANES)))
            o = jnp.where(first_head, pvs[0][:, :LANES], pvs[1][:, :LANES])
            den = jnp.where(first_head, pvs[0][:, LANES:], pvs[1][:, LANES:])
            mx = jnp.where(first_head, ms[0], ms[1])
            dst = rows if d == 1 else pl.ds(n * (blk * d) + r, blk, stride=d)
            o_scr[g][dst, :] = o / den
            l_scr[g][dst, :] = mx + jnp.log(den)

    rc = 256
    for c in range(seq // rc):
        rows = slice(c * rc, (c + 1) * rc)
        ls = [l_scr[g][rows, :] for g in range(N_GROUPS)]
        m = jnp.maximum(jnp.maximum(ls[0], ls[1]), ls[2])
        ws = [jnp.exp(lg - m) for lg in ls]
        num = ws[0] * o_scr[0][rows, :] + ws[1] * o_scr[1][rows, :] + ws[2] * o_scr[2][rows, :]
        y_ref[0, rows, :] = (num / (ws[0] + ws[1] + ws[2])).astype(BF16)


def _att(proj_nat, proj_perm, slopes):
    b, seq, _ = proj_nat.shape
    pairs = ATT_WIDTH // LANES
    per_tensor = ATT_WIDTH // LANES

    def qkv_spec(g, t):
        off = (2 * HG_WIDTH // LANES + t * per_tensor) if g == 0 else (3 * (g - 1) + t) * per_tensor
        return pl.BlockSpec((1, seq, LANES), lambda i, j, off=off: (i, 0, off + j))

    return pl.pallas_call(
        functools.partial(_att_kernel, seq=seq),
        grid=(b, pairs),
        in_specs=[pl.BlockSpec((1, SUBLANES, 2 * ATT_BLOCK), lambda i, j: (j, 0, 0))]
        + [qkv_spec(g, t) for g in range(N_GROUPS) for t in range(3)],
        out_specs=pl.BlockSpec((1, seq, LANES), lambda i, j: (i, 0, j)),
        out_shape=jax.ShapeDtypeStruct((b, seq, ATT_WIDTH), BF16),
        scratch_shapes=[pltpu.VMEM((seq, LANES), F32) for _ in range(2 * N_GROUPS)]
        + [pltpu.VMEM((2 * N_GROUPS, ATT_BLOCK, 2 * ATT_BLOCK), F32)],
        compiler_params=pltpu.CompilerParams(
            dimension_semantics=("parallel", "parallel"), vmem_limit_bytes=VMEM_LIMIT),
        name="att",
    )(slopes, *([proj_nat] * 3 + [proj_perm] * (3 * (N_GROUPS - 1))))


def _tail_kernel(h1_ref, yhg_ref, yatt_ref, ghg_ref, gatt_ref, wa_ref, wb_ref, wo_ref,
                 g2_ref, wgu_ref, wd_ref, gf_ref, o_ref):
    merged = (_sigmoid(ghg_ref[...]) * _dot(yhg_ref[...], wa_ref[...])
              + _sigmoid(gatt_ref[...]) * _dot(yatt_ref[...], wb_ref[...]))
    h2 = h1_ref[...] + _dot(merged.astype(BF16), wo_ref[...])
    xn = _rmsnorm(h2, g2_ref[...]).astype(BF16)
    h3 = h2 + 0.5 * _swiglu(xn, wgu_ref, wd_ref)
    o_ref[...] = _rmsnorm(h3, gf_ref[...])


def _tail(h1, y_hg, y_att, proj_b, wa, wb, wo, g2, wgu, wd, gf):
    t = h1.shape[0]
    row = lambda width, col=0: pl.BlockSpec((ROW_TILE, width), lambda i, col=col: (i, col))
    return pl.pallas_call(
        _tail_kernel,
        grid=(t // ROW_TILE,),
        in_specs=[
            row(D_MODEL), row(HG_WIDTH), row(ATT_WIDTH),
            row(D_MODEL, 1), row(D_MODEL, 2),
            _resident((HG_WIDTH, D_MODEL)), _resident((ATT_WIDTH, D_MODEL)), _resident((D_MODEL, D_MODEL)),
            _resident((1, D_MODEL)), _resident((D_MODEL, 2 * D_FF)), _resident((D_FF, D_MODEL)),
            _resident((1, D_MODEL)),
        ],
        out_specs=row(D_MODEL),
        out_shape=jax.ShapeDtypeStruct((t, D_MODEL), F32),
        compiler_params=pltpu.CompilerParams(
            dimension_semantics=("parallel",), vmem_limit_bytes=VMEM_LIMIT),
        name="tail",
    )(h1, y_hg, y_att, proj_b, proj_b, wa, wb, wo, g2, wgu, wd, gf)


def _alibi_table():
    n_heads = N_GROUPS * ATT_HEADS
    slopes = jnp.exp2(-ALIBI_MAX * jnp.arange(1, n_heads + 1, dtype=F32) / n_heads)
    slopes = slopes.reshape(N_GROUPS, ATT_HEADS // 2, 2) * jnp.asarray(ATT_DILATIONS, F32)[:, None, None]
    table = jnp.zeros((ATT_HEADS // 2, SUBLANES, 2 * ATT_BLOCK), F32)
    table = table.at[:, :2 * N_GROUPS, :].set(
        jnp.broadcast_to(slopes.transpose(1, 0, 2).reshape(ATT_HEADS // 2, 2 * N_GROUPS, 1),
                         (ATT_HEADS // 2, 2 * N_GROUPS, 2 * ATT_BLOCK)))
    return table


def _arrange_w_in(w_in):
    kw = HG_WIDTH
    hg_q, hg_f, hg_i, hg_og = (w_in[:, i * kw:(i + 1) * kw] for i in range(4))
    gates = w_in[:, 4 * kw + 3 * N_GROUPS * ATT_WIDTH:]
    scale = ATT_HEAD_DIM ** -0.5
    groups = []
    for g in range(N_GROUPS):
        base = 4 * kw + 3 * g * ATT_WIDTH
        groups += [w_in[:, base:base + ATT_WIDTH] * scale, w_in[:, base + ATT_WIDTH:base + 3 * ATT_WIDTH]]
    w_nat = jnp.concatenate([hg_q, hg_i] + groups[:2] + [hg_f, hg_og, gates], axis=1).astype(BF16)
    w_perm = jnp.concatenate(groups[2:], axis=1).astype(BF16)
    return w_nat, w_perm


def kernel(x, ffn1_norm, ffn1_w_gate_up, ffn1_w_down, mix_norm, w_in, hg_lower_bounds, hg_out_norm,
           w_branch_hg, w_branch_att, w_out, ffn2_norm, ffn2_w_gate_up, ffn2_w_down, final_norm):
    b, seq, d = x.shape
    assert d == D_MODEL and seq % (ATT_BLOCK * ATT_DILATIONS[-1]) == 0 and (b * seq) % ROW_TILE == 0
    x2d = x.reshape(b * seq, d)

    h1 = _ffn1(x2d, ffn1_norm[0:1], ffn1_w_gate_up[0].astype(BF16), ffn1_w_down[0].astype(BF16))
    w_nat, w_perm = _arrange_w_in(w_in[0])
    proj_a, proj_b, u_slab = _proj_nat(h1, mix_norm[0:1], w_nat)
    proj_perm = _proj_perm(u_slab, w_perm, b, seq)
    proj_a = proj_a.reshape(b, seq, -1)
    y_hg = _hgrn2(proj_a, proj_b.reshape(b, seq, -1), hg_lower_bounds.astype(F32), hg_out_norm[0:1])
    y_att = _att(proj_a, proj_perm, _alibi_table())
    out = _tail(h1, y_hg.reshape(b * seq, -1), y_att.reshape(b * seq, -1),
                proj_b,
                w_branch_hg[0].astype(BF16), w_branch_att[0].astype(BF16), w_out[0].astype(BF16),
                ffn2_norm[0:1], ffn2_w_gate_up[0].astype(BF16), ffn2_w_down[0].astype(BF16),
                final_norm.reshape(1, d))
    return out.reshape(b, seq, d)
```

```python
import functools

import jax
import jax.numpy as jnp
from jax import lax
from jax.experimental import pallas as pl
from jax.experimental.pallas import tpu as pltpu

F32 = jnp.float32
BF16 = jnp.bfloat16

D_MODEL = 1024
D_FF = 2816
HG_HEADS = 4
HG_DIM = 128
HG_WIDTH = HG_HEADS * HG_DIM
ATT_DILATIONS = (1, 4, 16)
ATT_BLOCK = 128
ATT_HEADS = 8
ATT_HEAD_DIM = 64
ATT_WIDTH = ATT_HEADS * ATT_HEAD_DIM
N_GROUPS = 3
ALIBI_MAX = 8.0
EPS = 1e-6
NEG_INF = -1e30

LANES = 128
SUBLANES = 8
MXU_DIM = 256
VMEM_LIMIT = 56 * 1024 * 1024

ROW_TILE = 512
COL_BLOCK = 512
NAT_BF16_COLS = 2 * HG_WIDTH + 3 * ATT_WIDTH
NAT_F32_COLS = 2 * HG_WIDTH + 2 * D_MODEL
HG_CHUNK = 64


def _sigmoid(x):
    return 1.0 / (1.0 + jnp.exp(-x))


def _rmsnorm(x, gain):
    ms = jnp.mean(x * x, axis=-1, keepdims=True)
    return x * lax.rsqrt(ms + EPS) * gain


def _dot(a, b):
    return jnp.dot(a, b, preferred_element_type=F32)


def _dot_nt(a, b):
    return lax.dot_general(a, b, (((1,), (1,)), ((), ())), preferred_element_type=F32)


def _dot_tn(a, b):
    return lax.dot_general(a, b, (((0,), (0,)), ((), ())), preferred_element_type=F32)


def _swiglu(xn_bf16, wgu_ref, wd_ref):
    split = (D_FF // (2 * MXU_DIM) + 1) * MXU_DIM
    y = None
    for lo, hi in ((0, split), (split, D_FF)):
        a = _dot(xn_bf16, wgu_ref[:, lo:hi])
        b = _dot(xn_bf16, wgu_ref[:, D_FF + lo:D_FF + hi])
        act = (a * _sigmoid(a) * b).astype(BF16)
        part = _dot(act, wd_ref[lo:hi, :])
        y = part if y is None else y + part
    return y


def _ffn1_kernel(x_ref, g_ref, wgu_ref, wd_ref, o_ref):
    x = x_ref[...]
    xn = _rmsnorm(x, g_ref[...]).astype(BF16)
    o_ref[...] = x + 0.5 * _swiglu(xn, wgu_ref, wd_ref)


def _resident(shape):
    return pl.BlockSpec(shape, lambda *_: (0,) * len(shape), pipeline_mode=pl.Buffered(1))


def _ffn1(x2d, gain, wgu, wd):
    t = x2d.shape[0]
    return pl.pallas_call(
        _ffn1_kernel,
        grid=(t // ROW_TILE,),
        in_specs=[
            pl.BlockSpec((ROW_TILE, D_MODEL), lambda i: (i, 0)),
            _resident((1, D_MODEL)),
            _resident((D_MODEL, 2 * D_FF)),
            _resident((D_FF, D_MODEL)),
        ],
        out_specs=pl.BlockSpec((ROW_TILE, D_MODEL), lambda i: (i, 0)),
        out_shape=jax.ShapeDtypeStruct((t, D_MODEL), F32),
        compiler_params=pltpu.CompilerParams(
            dimension_semantics=("parallel",), vmem_limit_bytes=VMEM_LIMIT),
        name="ffn1",
    )(x2d, gain, wgu, wd)


def _proj_nat_kernel(h_ref, g_ref, w_ref, oa_ref, ob_ref, slab_ref):
    u = _rmsnorm(h_ref[...], g_ref[...])
    for s in range(D_MODEL // LANES):
        slab_ref[s] = u[:, s * LANES:(s + 1) * LANES]
    ub = u.astype(BF16)
    for c in range(NAT_BF16_COLS // COL_BLOCK):
        cols = slice(c * COL_BLOCK, (c + 1) * COL_BLOCK)
        oa_ref[:, cols] = _dot(ub, w_ref[:, cols]).astype(BF16)
    for c in range(NAT_F32_COLS // COL_BLOCK):
        ob_ref[:, c * COL_BLOCK:(c + 1) * COL_BLOCK] = _dot(
            ub, w_ref[:, NAT_BF16_COLS + c * COL_BLOCK:NAT_BF16_COLS + (c + 1) * COL_BLOCK])


def _proj_nat(h1, gain, w_nat):
    t = h1.shape[0]
    n_slabs = D_MODEL // LANES
    return pl.pallas_call(
        _proj_nat_kernel,
        grid=(t // ROW_TILE,),
        in_specs=[
            pl.BlockSpec((ROW_TILE, D_MODEL), lambda i: (i, 0)),
            _resident((1, D_MODEL)),
            _resident((D_MODEL, NAT_BF16_COLS + NAT_F32_COLS)),
        ],
        out_specs=[
            pl.BlockSpec((ROW_TILE, NAT_BF16_COLS), lambda i: (i, 0)),
            pl.BlockSpec((ROW_TILE, NAT_F32_COLS), lambda i: (i, 0)),
            pl.BlockSpec((n_slabs, ROW_TILE, LANES), lambda i: (0, i, 0)),
        ],
        out_shape=[
            jax.ShapeDtypeStruct((t, NAT_BF16_COLS), BF16),
            jax.ShapeDtypeStruct((t, NAT_F32_COLS), F32),
            jax.ShapeDtypeStruct((n_slabs, t, LANES), F32),
        ],
        compiler_params=pltpu.CompilerParams(
            dimension_semantics=("parallel",), vmem_limit_bytes=VMEM_LIMIT),
        name="proj_nat",
    )(h1, gain, w_nat)


def _proj_perm_kernel(slab_ref, w_ref, o_ref, *, seq):
    t = pl.program_id(1)
    n_slabs = D_MODEL // LANES
    rows_out = o_ref.shape[1]

    def gather(g):
        d = ATT_DILATIONS[g]
        run = seq // d
        per_tile = rows_out // run if run < rows_out else 1
        pieces = []
        for c in range(per_tile):
            if run >= rows_out:
                r = (t * rows_out) // run
                start = r + d * ((t * rows_out) % run)
                n_rows = rows_out
            else:
                start = t * per_tile + c
                n_rows = run
            pieces.append(jnp.concatenate(
                [slab_ref[s, pl.ds(start, n_rows, stride=d), :].astype(BF16) for s in range(n_slabs)], axis=1))
        return pieces[0] if len(pieces) == 1 else jnp.concatenate(pieces, axis=0)

    lhs = {g: gather(g) for g in (1, 2)}
    for g in (1, 2):
        for c in range(3 * ATT_WIDTH // COL_BLOCK):
            cols = slice((g - 1) * 3 * ATT_WIDTH + c * COL_BLOCK, (g - 1) * 3 * ATT_WIDTH + (c + 1) * COL_BLOCK)
            o_ref[0, :, cols] = _dot(lhs[g], w_ref[:, cols]).astype(BF16)


def _proj_perm(u_slab, w_perm, b, seq):
    n_slabs = D_MODEL // LANES
    n_cols = 2 * 3 * ATT_WIDTH
    return pl.pallas_call(
        functools.partial(_proj_perm_kernel, seq=seq),
        grid=(b, seq // ROW_TILE),
        in_specs=[
            pl.BlockSpec((n_slabs, seq, LANES), lambda i, j: (0, i, 0)),
            _resident((D_MODEL, n_cols)),
        ],
        out_specs=pl.BlockSpec((1, ROW_TILE, n_cols), lambda i, j: (i, j, 0)),
        out_shape=jax.ShapeDtypeStruct((b, seq, n_cols), BF16),
        compiler_params=pltpu.CompilerParams(
            dimension_semantics=("parallel", "arbitrary"), vmem_limit_bytes=VMEM_LIMIT),
        name="proj_perm",
    )(u_slab, w_perm)


def _bcast_rows(x, block, row):
    n = x.shape[0] // block
    parts = [jnp.broadcast_to(x[i * block + row:i * block + row + 1, :], (block, x.shape[1]))
             for i in range(n)]
    return parts[0] if n == 1 else jnp.concatenate(parts, axis=0)


def _hgrn2_kernel(q_ref, i_ref, f_ref, og_ref, lbp_ref, gain_ref, o_ref, st_ref):
    c_len = HG_CHUNK

    @pl.when(pl.program_id(1) == 0)
    def _():
        st_ref[...] = jnp.zeros_like(st_ref)

    lbp = lbp_ref[...]
    e = jnp.exp(lbp - jnp.max(lbp, axis=0, keepdims=True))
    lb_all = e[0:1, :] / jnp.sum(e, axis=0, keepdims=True)
    gain_all = gain_ref[...]

    t_idx = lax.broadcasted_iota(jnp.int32, (c_len, c_len), 0)
    s_idx = lax.broadcasted_iota(jnp.int32, (c_len, c_len), 1)
    halves = [1 << i for i in range(c_len.bit_length() - 1)]
    level_masks = [((t_idx >> b.bit_length()) == (s_idx >> b.bit_length()))
                   & ((t_idx & b) != 0) & ((s_idx & b) == 0) for b in halves]
    diag_mask = t_idx == s_idx
    ones_rhs = jnp.ones((HG_DIM, c_len), BF16)
    sub = lax.broadcasted_iota(jnp.int32, (c_len, HG_WIDTH), 0) & (SUBLANES - 1)
    pair = sub & 6

    def chunk(c):
        rows = slice(c * c_len, (c + 1) * c_len)
        q = q_ref[0, rows, :].astype(F32)
        v = i_ref[0, rows, :]
        f = lb_all + (1.0 - lb_all) * _sigmoid(f_ref[0, rows, :])
        k = 1.0 - f

        operands = [(q * f, k)]
        odd = (sub & 1) == 1
        pq = f * jnp.where(odd, pltpu.roll(f, 1, axis=0), 1.0)
        sk = jnp.where(odd, 1.0, pltpu.roll(f, c_len - 1, axis=0))
        operands.append((q * pq, k * sk))
        r1, r3, r5, r7 = (_bcast_rows(pq, SUBLANES, r) for r in (1, 3, 5, 7))
        sk = sk * jnp.where(pair == 0, r3, jnp.where(pair == 4, r7, 1.0))
        pq = pq * jnp.where(pair == 2, r1, jnp.where(pair == 6, r5, 1.0))
        operands.append((q * pq, k * sk))
        r3, r7 = _bcast_rows(pq, SUBLANES, 3), _bcast_rows(pq, SUBLANES, 7)
        sk = sk * jnp.where(sub < 4, r7, 1.0)
        pq = pq * jnp.where(sub >= 4, r3, 1.0)
        b = SUBLANES
        while b < c_len:
            n = c_len // b
            tot = _bcast_rows(pq, b, b - 1)
            zero = jnp.zeros((b, HG_WIDTH), F32)

            def blk(x, i, b=b):
                return x[i * b:(i + 1) * b]

            operands.append((
                jnp.concatenate([zero if i % 2 == 0 else blk(q, i) * blk(pq, i) for i in range(n)], axis=0),
                jnp.concatenate([blk(k, i) * blk(sk, i) if i % 2 == 0 else zero for i in range(n)], axis=0)))
            pq, sk = (
                jnp.concatenate([blk(pq, i) if i % 2 == 0 else blk(pq, i) * blk(tot, i - 1)
                                 for i in range(n)], axis=0),
                jnp.concatenate([blk(sk, i) * blk(tot, i + 1) if i % 2 == 0 else blk(sk, i)
                                 for i in range(n)], axis=0))
            b *= 2
        operands = [(a.astype(BF16), kk.astype(BF16)) for a, kk in operands]
        qk = (q * k).astype(BF16)
        qg = (q * pq).astype(BF16)
        kg = (k * sk).astype(BF16)
        chunk_decay = pq[c_len - 1:c_len, :]

        outs = []
        for h in range(HG_HEADS):
            cols = slice(h * HG_DIM, (h + 1) * HG_DIM)
            a = jnp.where(diag_mask, _dot(qk[:, cols], ones_rhs), 0.0)
            for (ql, kl), mask in zip(operands, level_masks):
                a = jnp.where(mask, _dot_nt(ql[:, cols], kl[:, cols]), a)
            st = st_ref[h]
            o = _dot(a.astype(BF16), v[:, cols]) + _dot_nt(qg[:, cols], st.astype(BF16))
            st_ref[h] = st * chunk_decay[:, cols] + _dot_tn(v[:, cols], kg[:, cols])
            outs.append(o * lax.rsqrt(jnp.mean(o * o, axis=-1, keepdims=True) + EPS))
        og = og_ref[0, rows, :]
        o_ref[0, rows, :] = (jnp.concatenate(outs, axis=1) * gain_all * (og * _sigmoid(og))).astype(BF16)

    for c in range(ROW_TILE // c_len):
        chunk(c)


def _hgrn2(proj_a, proj_b, lower_bounds, out_gain):
    b, seq, _ = proj_a.shape
    blk = (1, ROW_TILE, HG_WIDTH)
    return pl.pallas_call(
        _hgrn2_kernel,
        grid=(b, seq // ROW_TILE),
        in_specs=[
            pl.BlockSpec(blk, lambda i, j: (i, j, 0)),
            pl.BlockSpec(blk, lambda i, j: (i, j, 1)),
            pl.BlockSpec(blk, lambda i, j: (i, j, 0)),
            pl.BlockSpec(blk, lambda i, j: (i, j, 1)),
            _resident(lower_bounds.shape),
            _resident((1, HG_WIDTH)),
        ],
        out_specs=pl.BlockSpec(blk, lambda i, j: (i, j, 0)),
        out_shape=jax.ShapeDtypeStruct((b, seq, HG_WIDTH), BF16),
        scratch_shapes=[pltpu.VMEM((HG_HEADS, HG_DIM, HG_DIM), F32)],
        compiler_params=pltpu.CompilerParams(
            dimension_semantics=("parallel", "arbitrary"), vmem_limit_bytes=VMEM_LIMIT),
        name="hgrn2",
    )(proj_a, proj_a, proj_b, proj_b, lower_bounds, out_gain)


def _att_kernel(slope_ref, *refs, seq):
    qkv_refs = refs[:3 * N_GROUPS]
    y_ref = refs[3 * N_GROUPS]
    scr = refs[3 * N_GROUPS + 1:]
    o_scr, l_scr = scr[0:N_GROUPS], scr[N_GROUPS:2 * N_GROUPS]
    y_scr, bias_scr = scr[2 * N_GROUPS], scr[2 * N_GROUPS + 1]
    blk = ATT_BLOCK
    n_blocks = seq // blk
    merge_d = ATT_DILATIONS[1]
    assert all(d % merge_d == 0 for d in ATT_DILATIONS[1:])

    qi = lax.broadcasted_iota(jnp.int32, (blk, 2 * blk), 0)
    kj = lax.broadcasted_iota(jnp.int32, (blk, 2 * blk), 1)
    dist = qi + blk - kj
    in_window = (dist >= 0) & (dist <= blk)
    dist_f = dist.astype(F32)
    for gh in range(2 * N_GROUPS):
        alibi = -slope_ref[0, gh:gh + 1, :] * dist_f
        bias_scr[gh] = jnp.where(in_window, alibi, NEG_INF)

    lane = lax.broadcasted_iota(jnp.int32, (blk, LANES), 1)
    first_head = lane < ATT_HEAD_DIM
    ones2 = jnp.ones((2 * blk, LANES), BF16)

    for g in range(N_GROUPS):
        d = ATT_DILATIONS[g]
        per_class = n_blocks // d
        q_ref, k_ref, v_ref = qkv_refs[3 * g:3 * g + 3]

        for idx in range(n_blocks):
            n, r = idx % per_class, idx // per_class
            rows = slice(idx * blk, (idx + 1) * blk)
            krows = rows if n == 0 else slice((idx - 1) * blk, (idx + 1) * blk)
            q = q_ref[0, rows, :]
            k = k_ref[0, krows, :]
            v_aug = jnp.concatenate([v_ref[0, krows, :], ones2[:k.shape[0]]], axis=1)
            pvs, ms = [], []
            for hh in range(2):
                qm = jnp.where(first_head if hh == 0 else ~first_head, q, jnp.zeros_like(q))
                if n == 0:
                    s = _dot_nt(qm, k) + bias_scr[2 * g + hh, :, blk:]
                    m = jnp.max(s, axis=-1, keepdims=True)
                else:
                    s = _dot_nt(qm, k) + bias_scr[2 * g + hh]
                    m = jnp.max(jnp.maximum(s[:, :blk], s[:, blk:]), axis=-1, keepdims=True)
                pvs.append(_dot(jnp.exp(s - m).astype(BF16), v_aug))
                ms.append(jnp.broadcast_to(m, (blk, LANES)))
            o = jnp.where(first_head, pvs[0][:, :LANES], pvs[1][:, :LANES])
            den = jnp.where(first_head, pvs[0][:, LANES:], pvs[1][:, LANES:])
            mx = jnp.where(first_head, ms[0], ms[1])
            if d == 1:
                dst = rows
            else:
                dst = pl.ds((r % merge_d) * (seq // merge_d) + n * blk * (d // merge_d) + r // merge_d,
                            blk, stride=d // merge_d)
            o_scr[g][dst, :] = o / den
            l_scr[g][dst, :] = mx + jnp.log(den)

    rc = 256
    for c4 in range(merge_d):
        for j0 in range(0, seq // merge_d, rc):
            rows = slice(c4 * (seq // merge_d) + j0, c4 * (seq // merge_d) + j0 + rc)
            tokens = pl.ds(c4 + merge_d * j0, rc, stride=merge_d)
            ls = [l_scr[0][tokens, :]] + [l_scr[g][rows, :] for g in range(1, N_GROUPS)]
            os_ = [o_scr[0][tokens, :]] + [o_scr[g][rows, :] for g in range(1, N_GROUPS)]
            m = jnp.maximum(jnp.maximum(ls[0], ls[1]), ls[2])
            ws = [jnp.exp(lg - m) for lg in ls]
            num = ws[0] * os_[0] + ws[1] * os_[1] + ws[2] * os_[2]
            y_scr[tokens, :] = num / (ws[0] + ws[1] + ws[2])
    for c in range(seq // rc):
        rows = slice(c * rc, (c + 1) * rc)
        y_ref[0, rows, :] = y_scr[rows, :].astype(BF16)


def _att(proj_nat, proj_perm, slopes):
    b, seq, _ = proj_nat.shape
    pairs = ATT_WIDTH // LANES
    per_tensor = ATT_WIDTH // LANES

    def qkv_spec(g, t):
        off = (2 * HG_WIDTH // LANES + t * per_tensor) if g == 0 else (3 * (g - 1) + t) * per_tensor
        return pl.BlockSpec((1, seq, LANES), lambda i, j, off=off: (i, 0, off + j))

    return pl.pallas_call(
        functools.partial(_att_kernel, seq=seq),
        grid=(b, pairs),
        in_specs=[pl.BlockSpec((1, SUBLANES, 2 * ATT_BLOCK), lambda i, j: (j, 0, 0))]
        + [qkv_spec(g, t) for g in range(N_GROUPS) for t in range(3)],
        out_specs=pl.BlockSpec((1, seq, LANES), lambda i, j: (i, 0, j)),
        out_shape=jax.ShapeDtypeStruct((b, seq, ATT_WIDTH), BF16),
        scratch_shapes=[pltpu.VMEM((seq, LANES), F32) for _ in range(2 * N_GROUPS + 1)]
        + [pltpu.VMEM((2 * N_GROUPS, ATT_BLOCK, 2 * ATT_BLOCK), F32)],
        compiler_params=pltpu.CompilerParams(
            dimension_semantics=("parallel", "parallel"), vmem_limit_bytes=VMEM_LIMIT),
        name="att",
    )(slopes, *([proj_nat] * 3 + [proj_perm] * (3 * (N_GROUPS - 1))))


def _tail_kernel(h1_ref, yhg_ref, yatt_ref, ghg_ref, gatt_ref, wa_ref, wb_ref, wo_ref,
                 g2_ref, wgu_ref, wd_ref, gf_ref, o_ref):
    merged = (_sigmoid(ghg_ref[...]) * _dot(yhg_ref[...], wa_ref[...])
              + _sigmoid(gatt_ref[...]) * _dot(yatt_ref[...], wb_ref[...]))
    h2 = h1_ref[...] + _dot(merged.astype(BF16), wo_ref[...])
    xn = _rmsnorm(h2, g2_ref[...]).astype(BF16)
    h3 = h2 + 0.5 * _swiglu(xn, wgu_ref, wd_ref)
    o_ref[...] = _rmsnorm(h3, gf_ref[...])


def _tail(h1, y_hg, y_att, proj_b, wa, wb, wo, g2, wgu, wd, gf):
    t = h1.shape[0]
    row = lambda width, col=0: pl.BlockSpec((ROW_TILE, width), lambda i, col=col: (i, col))
    return pl.pallas_call(
        _tail_kernel,
        grid=(t // ROW_TILE,),
        in_specs=[
            row(D_MODEL), row(HG_WIDTH), row(ATT_WIDTH),
            row(D_MODEL, 1), row(D_MODEL, 2),
            _resident((HG_WIDTH, D_MODEL)), _resident((ATT_WIDTH, D_MODEL)), _resident((D_MODEL, D_MODEL)),
            _resident((1, D_MODEL)), _resident((D_MODEL, 2 * D_FF)), _resident((D_FF, D_MODEL)),
            _resident((1, D_MODEL)),
        ],
        out_specs=row(D_MODEL),
        out_shape=jax.ShapeDtypeStruct((t, D_MODEL), F32),
        compiler_params=pltpu.CompilerParams(
            dimension_semantics=("parallel",), vmem_limit_bytes=VMEM_LIMIT),
        name="tail",
    )(h1, y_hg, y_att, proj_b, proj_b, wa, wb, wo, g2, wgu, wd, gf)


def _alibi_table():
    n_heads = N_GROUPS * ATT_HEADS
    slopes = jnp.exp2(-ALIBI_MAX * jnp.arange(1, n_heads + 1, dtype=F32) / n_heads)
    slopes = slopes.reshape(N_GROUPS, ATT_HEADS // 2, 2) * jnp.asarray(ATT_DILATIONS, F32)[:, None, None]
    table = jnp.zeros((ATT_HEADS // 2, SUBLANES, 2 * ATT_BLOCK), F32)
    table = table.at[:, :2 * N_GROUPS, :].set(
        jnp.broadcast_to(slopes.transpose(1, 0, 2).reshape(ATT_HEADS // 2, 2 * N_GROUPS, 1),
                         (ATT_HEADS // 2, 2 * N_GROUPS, 2 * ATT_BLOCK)))
    return table


def _arrange_w_in(w_in):
    kw = HG_WIDTH
    hg_q, hg_f, hg_i, hg_og = (w_in[:, i * kw:(i + 1) * kw] for i in range(4))
    gates = w_in[:, 4 * kw + 3 * N_GROUPS * ATT_WIDTH:]
    scale = ATT_HEAD_DIM ** -0.5
    groups = []
    for g in range(N_GROUPS):
        base = 4 * kw + 3 * g * ATT_WIDTH
        groups += [w_in[:, base:base + ATT_WIDTH] * scale, w_in[:, base + ATT_WIDTH:base + 3 * ATT_WIDTH]]
    w_nat = jnp.concatenate([hg_q, hg_i] + groups[:2] + [hg_f, hg_og, gates], axis=1).astype(BF16)
    w_perm = jnp.concatenate(groups[2:], axis=1).astype(BF16)
    return w_nat, w_perm


def kernel(x, ffn1_norm, ffn1_w_gate_up, ffn1_w_down, mix_norm, w_in, hg_lower_bounds, hg_out_norm,
           w_branch_hg, w_branch_att, w_out, ffn2_norm, ffn2_w_gate_up, ffn2_w_down, final_norm):
    b, seq, d = x.shape
    assert d == D_MODEL and seq % (ATT_BLOCK * ATT_DILATIONS[-1]) == 0 and (b * seq) % ROW_TILE == 0
    x2d = x.reshape(b * seq, d)

    h1 = _ffn1(x2d, ffn1_norm[0:1], ffn1_w_gate_up[0].astype(BF16), ffn1_w_down[0].astype(BF16))
    w_nat, w_perm = _arrange_w_in(w_in[0])
    proj_a, proj_b, u_slab = _proj_nat(h1, mix_norm[0:1], w_nat)
    proj_perm = _proj_perm(u_slab, w_perm, b, seq)
    proj_a = proj_a.reshape(b, seq, -1)
    y_hg = _hgrn2(proj_a, proj_b.reshape(b, seq, -1), hg_lower_bounds.astype(F32), hg_out_norm[0:1])
    y_att = _att(proj_a, proj_perm, _alibi_table())
    out = _tail(h1, y_hg.reshape(b * seq, -1), y_att.reshape(b * seq, -1),
                proj_b,
                w_branch_hg[0].astype(BF16), w_branch_att[0].astype(BF16), w_out[0].astype(BF16),
                ffn2_norm[0:1], ffn2_w_gate_up[0].astype(BF16), ffn2_w_down[0].astype(BF16),
                final_norm.reshape(1, d))
    return out.reshape(b, seq, d)
```

```python
import functools

import jax
import jax.numpy as jnp
from jax import lax
from jax.experimental import pallas as pl
from jax.experimental.pallas import tpu as pltpu

F32 = jnp.float32
BF16 = jnp.bfloat16

D_MODEL = 1024
D_FF = 2816
HG_HEADS = 4
HG_DIM = 128
HG_WIDTH = HG_HEADS * HG_DIM
ATT_DILATIONS = (1, 4, 16)
ATT_BLOCK = 128
ATT_HEADS = 8
ATT_HEAD_DIM = 64
ATT_WIDTH = ATT_HEADS * ATT_HEAD_DIM
N_GROUPS = 3
ALIBI_MAX = 8.0
EPS = 1e-6
NEG_INF = -1e30

LANES = 128
SUBLANES = 8
MXU_DIM = 256
VMEM_LIMIT = 56 * 1024 * 1024

ROW_TILE = 512
COL_BLOCK = 512
NAT_BF16_COLS = 2 * HG_WIDTH + 3 * ATT_WIDTH
NAT_F32_COLS = 2 * HG_WIDTH + 2 * D_MODEL
HG_CHUNK = 64


def _sigmoid(x):
    return 1.0 / (1.0 + jnp.exp(-x))


def _rmsnorm(x, gain):
    ms = jnp.mean(x * x, axis=-1, keepdims=True)
    return x * lax.rsqrt(ms + EPS) * gain


def _dot(a, b):
    return jnp.dot(a, b, preferred_element_type=F32)


def _dot_nt(a, b):
    return lax.dot_general(a, b, (((1,), (1,)), ((), ())), preferred_element_type=F32)


def _dot_tn(a, b):
    return lax.dot_general(a, b, (((0,), (0,)), ((), ())), preferred_element_type=F32)


def _swiglu(xn_bf16, wgu_ref, wd_ref):
    split = (D_FF // (2 * MXU_DIM) + 1) * MXU_DIM
    y = None
    for lo, hi in ((0, split), (split, D_FF)):
        a = _dot(xn_bf16, wgu_ref[:, lo:hi])
        b = _dot(xn_bf16, wgu_ref[:, D_FF + lo:D_FF + hi])
        act = (a * _sigmoid(a) * b).astype(BF16)
        part = _dot(act, wd_ref[lo:hi, :])
        y = part if y is None else y + part
    return y


W_IN_SRC_OF_DST = (0, 2, 4, 5, 6, 1, 3, 13, 14, 15, 16, 7, 8, 9, 10, 11, 12)
W_IN_Q_BLOCKS = (4, 7, 10)
N_NAT_BLOCKS = (NAT_BF16_COLS + NAT_F32_COLS) // COL_BLOCK
N_PERM_BLOCKS = len(W_IN_SRC_OF_DST) - N_NAT_BLOCKS


def _ffn1_kernel(x_ref, g_ref, wgu_ref, wd_ref, win_ref, o_ref, wall_ref):
    x = x_ref[...]
    xn = _rmsnorm(x, g_ref[...]).astype(BF16)
    o_ref[...] = x + 0.5 * _swiglu(xn, wgu_ref, wd_ref)

    w = win_ref[...]
    wall_ref[...] = jnp.concatenate(
        [w[:, s * COL_BLOCK:(s + 1) * COL_BLOCK] * (ATT_HEAD_DIM ** -0.5 if s in W_IN_Q_BLOCKS else 1.0)
         for s in W_IN_SRC_OF_DST], axis=1).astype(BF16)


def _resident(shape):
    return pl.BlockSpec(shape, lambda *_: (0,) * len(shape), pipeline_mode=pl.Buffered(1))


BF16_ROWS = 2 * SUBLANES


def _side_cast_specs(weights, grid):
    n_steps = 1
    for g in grid:
        n_steps *= g
    in_specs, out_specs, out_shape = [], [], []
    for w in weights:
        rows, cols = w.shape
        per_step = BF16_ROWS
        while rows % per_step or per_step * n_steps < rows:
            per_step += BF16_ROWS
        last = rows // per_step - 1

        def index(*ids, last=last):
            step = ids[0]
            for g, i in zip(grid[1:], ids[1:]):
                step = step * g + i
            return (jnp.minimum(step, last), 0)

        in_specs.append(pl.BlockSpec((per_step, cols), index))
        out_specs.append(pl.BlockSpec((per_step, cols), index))
        out_shape.append(jax.ShapeDtypeStruct((rows, cols), BF16))
    return in_specs, out_specs, out_shape


def _side_cast(in_refs, out_refs):
    for i_ref, o_ref in zip(in_refs, out_refs):
        o_ref[...] = i_ref[...].astype(BF16)


def _ffn1(x2d, gain, wgu, wd, w_in):
    t = x2d.shape[0]
    grid = (t // ROW_TILE,)
    side_in, side_out, side_shape = _side_cast_specs([w_in], grid)
    return pl.pallas_call(
        _ffn1_kernel,
        grid=grid,
        in_specs=[
            pl.BlockSpec((ROW_TILE, D_MODEL), lambda i: (i, 0)),
            _resident((1, D_MODEL)),
            _resident((D_MODEL, 2 * D_FF)),
            _resident((D_FF, D_MODEL)),
        ] + side_in,
        out_specs=[pl.BlockSpec((ROW_TILE, D_MODEL), lambda i: (i, 0))] + side_out,
        out_shape=[jax.ShapeDtypeStruct((t, D_MODEL), F32)] + side_shape,
        compiler_params=pltpu.CompilerParams(
            dimension_semantics=("arbitrary",), vmem_limit_bytes=VMEM_LIMIT),
        name="ffn1",
    )(x2d, gain, wgu, wd, w_in)


def _proj_nat_kernel(h_ref, g_ref, w_ref, oa_ref, ob_ref, slab_ref):
    u = _rmsnorm(h_ref[...], g_ref[...])
    for s in range(D_MODEL // LANES):
        slab_ref[s] = u[:, s * LANES:(s + 1) * LANES]
    ub = u.astype(BF16)
    for c in range(NAT_BF16_COLS // COL_BLOCK):
        cols = slice(c * COL_BLOCK, (c + 1) * COL_BLOCK)
        oa_ref[:, cols] = _dot(ub, w_ref[:, cols]).astype(BF16)
    for c in range(NAT_F32_COLS // COL_BLOCK):
        ob_ref[:, c * COL_BLOCK:(c + 1) * COL_BLOCK] = _dot(
            ub, w_ref[:, NAT_BF16_COLS + c * COL_BLOCK:NAT_BF16_COLS + (c + 1) * COL_BLOCK])


def _proj_nat(h1, gain, w_nat):
    t = h1.shape[0]
    n_slabs = D_MODEL // LANES
    return pl.pallas_call(
        _proj_nat_kernel,
        grid=(t // ROW_TILE,),
        in_specs=[
            pl.BlockSpec((ROW_TILE, D_MODEL), lambda i: (i, 0)),
            _resident((1, D_MODEL)),
            _resident((D_MODEL, NAT_BF16_COLS + NAT_F32_COLS)),
        ],
        out_specs=[
            pl.BlockSpec((ROW_TILE, NAT_BF16_COLS), lambda i: (i, 0)),
            pl.BlockSpec((ROW_TILE, NAT_F32_COLS), lambda i: (i, 0)),
            pl.BlockSpec((n_slabs, ROW_TILE, LANES), lambda i: (0, i, 0)),
        ],
        out_shape=[
            jax.ShapeDtypeStruct((t, NAT_BF16_COLS), BF16),
            jax.ShapeDtypeStruct((t, NAT_F32_COLS), F32),
            jax.ShapeDtypeStruct((n_slabs, t, LANES), F32),
        ],
        compiler_params=pltpu.CompilerParams(
            dimension_semantics=("parallel",), vmem_limit_bytes=VMEM_LIMIT),
        name="proj_nat",
    )(h1, gain, w_nat)


def _proj_perm_kernel(slab_ref, *refs, seq):
    w_refs, o_ref = refs[:-1], refs[-1]
    t = pl.program_id(1)
    n_slabs = D_MODEL // LANES
    rows_out = o_ref.shape[1]

    def gather(g):
        d = ATT_DILATIONS[g]
        run = seq // d
        per_tile = rows_out // run if run < rows_out else 1
        pieces = []
        for c in range(per_tile):
            if run >= rows_out:
                r = (t * rows_out) // run
                start = r + d * ((t * rows_out) % run)
                n_rows = rows_out
            else:
                start = t * per_tile + c
                n_rows = run
            pieces.append(jnp.concatenate(
                [slab_ref[s, pl.ds(start, n_rows, stride=d), :].astype(BF16) for s in range(n_slabs)], axis=1))
        return pieces[0] if len(pieces) == 1 else jnp.concatenate(pieces, axis=0)

    lhs = {g: gather(g) for g in (1, 2)}
    per_group = 3 * ATT_WIDTH // COL_BLOCK
    for g in (1, 2):
        for c in range(per_group):
            blk = (g - 1) * per_group + c
            o_ref[0, :, blk * COL_BLOCK:(blk + 1) * COL_BLOCK] = _dot(lhs[g], w_refs[blk][...]).astype(BF16)


def _proj_perm(u_slab, w_all, b, seq):
    n_slabs = D_MODEL // LANES
    n_cols = N_PERM_BLOCKS * COL_BLOCK
    return pl.pallas_call(
        functools.partial(_proj_perm_kernel, seq=seq),
        grid=(b, seq // ROW_TILE),
        in_specs=[pl.BlockSpec((n_slabs, seq, LANES), lambda i, j: (0, i, 0))]
        + [pl.BlockSpec((D_MODEL, COL_BLOCK), lambda i, j, c=c: (0, N_NAT_BLOCKS + c), pipeline_mode=pl.Buffered(1))
           for c in range(N_PERM_BLOCKS)],
        out_specs=pl.BlockSpec((1, ROW_TILE, n_cols), lambda i, j: (i, j, 0)),
        out_shape=jax.ShapeDtypeStruct((b, seq, n_cols), BF16),
        compiler_params=pltpu.CompilerParams(
            dimension_semantics=("parallel", "arbitrary"), vmem_limit_bytes=VMEM_LIMIT),
        name="proj_perm",
    )(u_slab, *([w_all] * N_PERM_BLOCKS))


def _bcast_rows(x, block, row):
    n = x.shape[0] // block
    parts = [jnp.broadcast_to(x[i * block + row:i * block + row + 1, :], (block, x.shape[1]))
             for i in range(n)]
    return parts[0] if n == 1 else jnp.concatenate(parts, axis=0)


def _hgrn2_kernel(q_ref, i_ref, f_ref, og_ref, lbp_ref, gain_ref, *refs):
    n_side = (len(refs) - 2) // 2
    o_ref, st_ref = refs[n_side], refs[-1]
    _side_cast(refs[:n_side], refs[n_side + 1:-1])
    c_len = HG_CHUNK

    @pl.when(pl.program_id(1) == 0)
    def _():
        st_ref[...] = jnp.zeros_like(st_ref)

    lbp = lbp_ref[...]
    e = jnp.exp(lbp - jnp.max(lbp, axis=0, keepdims=True))
    lb_all = e[0:1, :] / jnp.sum(e, axis=0, keepdims=True)
    gain_all = gain_ref[...]

    t_idx = lax.broadcasted_iota(jnp.int32, (c_len, c_len), 0)
    s_idx = lax.broadcasted_iota(jnp.int32, (c_len, c_len), 1)
    halves = [1 << i for i in range(c_len.bit_length() - 1)]
    level_masks = [((t_idx >> b.bit_length()) == (s_idx >> b.bit_length()))
                   & ((t_idx & b) != 0) & ((s_idx & b) == 0) for b in halves]
    diag_mask = t_idx == s_idx
    ones_rhs = jnp.ones((HG_DIM, c_len), BF16)
    sub = lax.broadcasted_iota(jnp.int32, (c_len, HG_WIDTH), 0) & (SUBLANES - 1)
    pair = sub & 6

    def chunk(c):
        rows = slice(c * c_len, (c + 1) * c_len)
        q = q_ref[0, rows, :].astype(F32)
        v = i_ref[0, rows, :]
        f = lb_all + (1.0 - lb_all) * _sigmoid(f_ref[0, rows, :])
        k = 1.0 - f

        operands = [(q * f, k)]
        odd = (sub & 1) == 1
        pq = f * jnp.where(odd, pltpu.roll(f, 1, axis=0), 1.0)
        sk = jnp.where(odd, 1.0, pltpu.roll(f, c_len - 1, axis=0))
        operands.append((q * pq, k * sk))
        r1, r3, r5, r7 = (_bcast_rows(pq, SUBLANES, r) for r in (1, 3, 5, 7))
        sk = sk * jnp.where(pair == 0, r3, jnp.where(pair == 4, r7, 1.0))
        pq = pq * jnp.where(pair == 2, r1, jnp.where(pair == 6, r5, 1.0))
        operands.append((q * pq, k * sk))
        r3, r7 = _bcast_rows(pq, SUBLANES, 3), _bcast_rows(pq, SUBLANES, 7)
        sk = sk * jnp.where(sub < 4, r7, 1.0)
        pq = pq * jnp.where(sub >= 4, r3, 1.0)
        b = SUBLANES
        while b < c_len:
            n = c_len // b
            tot = _bcast_rows(pq, b, b - 1)
            zero = jnp.zeros((b, HG_WIDTH), F32)

            def blk(x, i, b=b):
                return x[i * b:(i + 1) * b]

            operands.append((
                jnp.concatenate([zero if i % 2 == 0 else blk(q, i) * blk(pq, i) for i in range(n)], axis=0),
                jnp.concatenate([blk(k, i) * blk(sk, i) if i % 2 == 0 else zero for i in range(n)], axis=0)))
            pq, sk = (
                jnp.concatenate([blk(pq, i) if i % 2 == 0 else blk(pq, i) * blk(tot, i - 1)
                                 for i in range(n)], axis=0),
                jnp.concatenate([blk(sk, i) * blk(tot, i + 1) if i % 2 == 0 else blk(sk, i)
                                 for i in range(n)], axis=0))
            b *= 2
        operands = [(a.astype(BF16), kk.astype(BF16)) for a, kk in operands]
        qk = (q * k).astype(BF16)
        qg = (q * pq).astype(BF16)
        kg = (k * sk).astype(BF16)
        chunk_decay = pq[c_len - 1:c_len, :]

        outs = []
        for h in range(HG_HEADS):
            cols = slice(h * HG_DIM, (h + 1) * HG_DIM)
            a = jnp.where(diag_mask, _dot(qk[:, cols], ones_rhs), 0.0)
            for (ql, kl), mask in zip(operands, level_masks):
                a = jnp.where(mask, _dot_nt(ql[:, cols], kl[:, cols]), a)
            st = st_ref[h]
            o = _dot(a.astype(BF16), v[:, cols]) + _dot_nt(qg[:, cols], st.astype(BF16))
            st_ref[h] = st * chunk_decay[:, cols] + _dot_tn(v[:, cols], kg[:, cols])
            outs.append(o * lax.rsqrt(jnp.mean(o * o, axis=-1, keepdims=True) + EPS))
        og = og_ref[0, rows, :]
        o_ref[0, rows, :] = (jnp.concatenate(outs, axis=1) * gain_all * (og * _sigmoid(og))).astype(BF16)

    for c in range(ROW_TILE // c_len):
        chunk(c)


def _hgrn2(proj_a, proj_b, lower_bounds, out_gain, side_weights):
    b, seq, _ = proj_a.shape
    blk = (1, ROW_TILE, HG_WIDTH)
    grid = (b, seq // ROW_TILE)
    side_in, side_out, side_shape = _side_cast_specs(side_weights, grid)
    return pl.pallas_call(
        _hgrn2_kernel,
        grid=grid,
        in_specs=[
            pl.BlockSpec(blk, lambda i, j: (i, j, 0)),
            pl.BlockSpec(blk, lambda i, j: (i, j, 1)),
            pl.BlockSpec(blk, lambda i, j: (i, j, 0)),
            pl.BlockSpec(blk, lambda i, j: (i, j, 1)),
            _resident(lower_bounds.shape),
            _resident((1, HG_WIDTH)),
        ] + side_in,
        out_specs=[pl.BlockSpec(blk, lambda i, j: (i, j, 0))] + side_out,
        out_shape=[jax.ShapeDtypeStruct((b, seq, HG_WIDTH), BF16)] + side_shape,
        scratch_shapes=[pltpu.VMEM((HG_HEADS, HG_DIM, HG_DIM), F32)],
        compiler_params=pltpu.CompilerParams(
            dimension_semantics=("arbitrary", "arbitrary"), vmem_limit_bytes=VMEM_LIMIT),
        name="hgrn2",
    )(proj_a, proj_a, proj_b, proj_b, lower_bounds, out_gain, *side_weights)


def _att_kernel(slope_ref, *refs, seq, n_side):
    qkv_refs = refs[:3 * N_GROUPS]
    y_ref = refs[3 * N_GROUPS + n_side]
    _side_cast(refs[3 * N_GROUPS:3 * N_GROUPS + n_side],
               refs[3 * N_GROUPS + n_side + 1:3 * N_GROUPS + 2 * n_side + 1])
    scr = refs[3 * N_GROUPS + 2 * n_side + 1:]
    o_scr, l_scr = scr[0:N_GROUPS], scr[N_GROUPS:2 * N_GROUPS]
    y_scr, bias_scr = scr[2 * N_GROUPS], scr[2 * N_GROUPS + 1]
    blk = ATT_BLOCK
    n_blocks = seq // blk
    merge_d = ATT_DILATIONS[1]
    assert all(d % merge_d == 0 for d in ATT_DILATIONS[1:])

    qi = lax.broadcasted_iota(jnp.int32, (blk, 2 * blk), 0)
    kj = lax.broadcasted_iota(jnp.int32, (blk, 2 * blk), 1)
    dist = qi + blk - kj
    in_window = (dist >= 0) & (dist <= blk)
    dist_f = dist.astype(F32)
    for gh in range(2 * N_GROUPS):
        alibi = -slope_ref[0, gh:gh + 1, :] * dist_f
        bias_scr[gh] = jnp.where(in_window, alibi, NEG_INF)

    lane = lax.broadcasted_iota(jnp.int32, (blk, LANES), 1)
    first_head = lane < ATT_HEAD_DIM
    ones2 = jnp.ones((2 * blk, LANES), BF16)

    for g in range(N_GROUPS):
        d = ATT_DILATIONS[g]
        per_class = n_blocks // d
        q_ref, k_ref, v_ref = qkv_refs[3 * g:3 * g + 3]

        for idx in range(n_blocks):
            n, r = idx % per_class, idx // per_class
            rows = slice(idx * blk, (idx + 1) * blk)
            krows = rows if n == 0 else slice((idx - 1) * blk, (idx + 1) * blk)
            q = q_ref[0, rows, :]
            k = k_ref[0, krows, :]
            v_aug = jnp.concatenate([v_ref[0, krows, :], ones2[:k.shape[0]]], axis=1)
            pvs, ms = [], []
            for hh in range(2):
                qm = jnp.where(first_head if hh == 0 else ~first_head, q, jnp.zeros_like(q))
                if n == 0:
                    s = _dot_nt(qm, k) + bias_scr[2 * g + hh, :, blk:]
                    m = jnp.max(s, axis=-1, keepdims=True)
                else:
                    s = _dot_nt(qm, k) + bias_scr[2 * g + hh]
                    m = jnp.max(jnp.maximum(s[:, :blk], s[:, blk:]), axis=-1, keepdims=True)
                pvs.append(_dot(jnp.exp(s - m).astype(BF16), v_aug))
                ms.append(jnp.broadcast_to(m, (blk, LANES)))
            o = jnp.where(first_head, pvs[0][:, :LANES], pvs[1][:, :LANES])
            den = jnp.where(first_head, pvs[0][:, LANES:], pvs[1][:, LANES:])
            mx = jnp.where(first_head, ms[0], ms[1])
            if d == 1:
                dst = rows
            else:
                dst = pl.ds((r % merge_d) * (seq // merge_d) + n * blk * (d // merge_d) + r // merge_d,
                            blk, stride=d // merge_d)
            o_scr[g][dst, :] = o / den
            l_scr[g][dst, :] = mx + jnp.log(den)

    rc = 256
    for c4 in range(merge_d):
        for j0 in range(0, seq // merge_d, rc):
            rows = slice(c4 * (seq // merge_d) + j0, c4 * (seq // merge_d) + j0 + rc)
            tokens = pl.ds(c4 + merge_d * j0, rc, stride=merge_d)
            ls = [l_scr[0][tokens, :]] + [l_scr[g][rows, :] for g in range(1, N_GROUPS)]
            os_ = [o_scr[0][tokens, :]] + [o_scr[g][rows, :] for g in range(1, N_GROUPS)]
            m = jnp.maximum(jnp.maximum(ls[0], ls[1]), ls[2])
            ws = [jnp.exp(lg - m) for lg in ls]
            num = ws[0] * os_[0] + ws[1] * os_[1] + ws[2] * os_[2]
            y_scr[tokens, :] = num / (ws[0] + ws[1] + ws[2])
    for c in range(seq // rc):
        rows = slice(c * rc, (c + 1) * rc)
        y_ref[0, rows, :] = y_scr[rows, :].astype(BF16)


def _att(proj_nat, proj_perm, slopes, side_weights):
    b, seq, _ = proj_nat.shape
    pairs = ATT_WIDTH // LANES
    per_tensor = ATT_WIDTH // LANES
    grid = (b, pairs)
    side_in, side_out, side_shape = _side_cast_specs(side_weights, grid)

    def qkv_spec(g, t):
        off = (2 * HG_WIDTH // LANES + t * per_tensor) if g == 0 else (3 * (g - 1) + t) * per_tensor
        return pl.BlockSpec((1, seq, LANES), lambda i, j, off=off: (i, 0, off + j))

    return pl.pallas_call(
        functools.partial(_att_kernel, seq=seq, n_side=len(side_weights)),
        grid=grid,
        in_specs=[pl.BlockSpec((1, SUBLANES, 2 * ATT_BLOCK), lambda i, j: (j, 0, 0))]
        + [qkv_spec(g, t) for g in range(N_GROUPS) for t in range(3)] + side_in,
        out_specs=[pl.BlockSpec((1, seq, LANES), lambda i, j: (i, 0, j))] + side_out,
        out_shape=[jax.ShapeDtypeStruct((b, seq, ATT_WIDTH), BF16)] + side_shape,
        scratch_shapes=[pltpu.VMEM((seq, LANES), F32) for _ in range(2 * N_GROUPS + 1)]
        + [pltpu.VMEM((2 * N_GROUPS, ATT_BLOCK, 2 * ATT_BLOCK), F32)],
        compiler_params=pltpu.CompilerParams(
            dimension_semantics=("arbitrary", "arbitrary"), vmem_limit_bytes=VMEM_LIMIT),
        name="att",
    )(slopes, *([proj_nat] * 3 + [proj_perm] * (3 * (N_GROUPS - 1))), *side_weights)


def _tail_kernel(h1_ref, yhg_ref, yatt_ref, ghg_ref, gatt_ref, wa_ref, wb_ref, wo_ref,
                 g2_ref, wgu_ref, wd_ref, gf_ref, o_ref):
    merged = (_sigmoid(ghg_ref[...]) * _dot(yhg_ref[...], wa_ref[...])
              + _sigmoid(gatt_ref[...]) * _dot(yatt_ref[...], wb_ref[...]))
    h2 = h1_ref[...] + _dot(merged.astype(BF16), wo_ref[...])
    xn = _rmsnorm(h2, g2_ref[...]).astype(BF16)
    h3 = h2 + 0.5 * _swiglu(xn, wgu_ref, wd_ref)
    o_ref[...] = _rmsnorm(h3, gf_ref[...])


def _tail(h1, y_hg, y_att, proj_b, wa, wb, wo, g2, wgu, wd, gf):
    t = h1.shape[0]
    row = lambda width, col=0: pl.BlockSpec((ROW_TILE, width), lambda i, col=col: (i, col))
    return pl.pallas_call(
        _tail_kernel,
        grid=(t // ROW_TILE,),
        in_specs=[
            row(D_MODEL), row(HG_WIDTH), row(ATT_WIDTH),
            row(D_MODEL, 1), row(D_MODEL, 2),
            _resident((HG_WIDTH, D_MODEL)), _resident((ATT_WIDTH, D_MODEL)), _resident((D_MODEL, D_MODEL)),
            _resident((1, D_MODEL)), _resident((D_MODEL, 2 * D_FF)), _resident((D_FF, D_MODEL)),
            _resident((1, D_MODEL)),
        ],
        out_specs=row(D_MODEL),
        out_shape=jax.ShapeDtypeStruct((t, D_MODEL), F32),
        compiler_params=pltpu.CompilerParams(
            dimension_semantics=("parallel",), vmem_limit_bytes=VMEM_LIMIT),
        name="tail",
    )(h1, y_hg, y_att, proj_b, proj_b, wa, wb, wo, g2, wgu, wd, gf)


def _alibi_table():
    n_heads = N_GROUPS * ATT_HEADS
    slopes = jnp.exp2(-ALIBI_MAX * jnp.arange(1, n_heads + 1, dtype=F32) / n_heads)
    slopes = slopes.reshape(N_GROUPS, ATT_HEADS // 2, 2) * jnp.asarray(ATT_DILATIONS, F32)[:, None, None]
    table = jnp.zeros((ATT_HEADS // 2, SUBLANES, 2 * ATT_BLOCK), F32)
    table = table.at[:, :2 * N_GROUPS, :].set(
        jnp.broadcast_to(slopes.transpose(1, 0, 2).reshape(ATT_HEADS // 2, 2 * N_GROUPS, 1),
                         (ATT_HEADS // 2, 2 * N_GROUPS, 2 * ATT_BLOCK)))
    return table


def kernel(x, ffn1_norm, ffn1_w_gate_up, ffn1_w_down, mix_norm, w_in, hg_lower_bounds, hg_out_norm,
           w_branch_hg, w_branch_att, w_out, ffn2_norm, ffn2_w_gate_up, ffn2_w_down, final_norm):
    b, seq, d = x.shape
    assert d == D_MODEL and seq % (ATT_BLOCK * ATT_DILATIONS[-1]) == 0 and (b * seq) % ROW_TILE == 0
    x2d = x.reshape(b * seq, d)

    h1, w_all = _ffn1(x2d, ffn1_norm[0:1], ffn1_w_gate_up[0].astype(BF16), ffn1_w_down[0].astype(BF16), w_in[0])
    proj_a, proj_b, u_slab = _proj_nat(h1, mix_norm[0:1], w_all)
    proj_perm = _proj_perm(u_slab, w_all, b, seq)
    proj_a = proj_a.reshape(b, seq, -1)
    y_hg, wa, wb, wo = _hgrn2(proj_a, proj_b.reshape(b, seq, -1), hg_lower_bounds.astype(F32), hg_out_norm[0:1],
                              [w_branch_hg[0], w_branch_att[0], w_out[0]])
    y_att, wgu2, wd2 = _att(proj_a, proj_perm, _alibi_table(), [ffn2_w_gate_up[0], ffn2_w_down[0]])
    out = _tail(h1, y_hg.reshape(b * seq, -1), y_att.reshape(b * seq, -1), proj_b, wa, wb, wo,
                ffn2_norm[0:1], wgu2, wd2, final_norm.reshape(1, d))
    return out.reshape(b, seq, d)
```

```python
import functools

import jax
import jax.numpy as jnp
from jax import lax
from jax.experimental import pallas as pl
from jax.experimental.pallas import tpu as pltpu

F32 = jnp.float32
BF16 = jnp.bfloat16

D_MODEL = 1024
D_FF = 2816
HG_HEADS = 4
HG_DIM = 128
HG_WIDTH = HG_HEADS * HG_DIM
ATT_DILATIONS = (1, 4, 16)
ATT_BLOCK = 128
ATT_HEADS = 8
ATT_HEAD_DIM = 64
ATT_WIDTH = ATT_HEADS * ATT_HEAD_DIM
N_GROUPS = 3
ALIBI_MAX = 8.0
EPS = 1e-6
NEG_INF = -1e30
LOG2_E = 1.4426950408889634
Q_SCALE = ATT_HEAD_DIM ** -0.5 * LOG2_E

LANES = 128
SUBLANES = 8
MXU_DIM = 256
VMEM_LIMIT = 56 * 1024 * 1024

ROW_TILE = 512
COL_BLOCK = 512
NAT_BF16_COLS = 2 * HG_WIDTH + 3 * ATT_WIDTH
NAT_F32_COLS = 2 * HG_WIDTH + 2 * D_MODEL
HG_CHUNK = 64


def _sigmoid(x):
    return 1.0 / (1.0 + jnp.exp(-x))


def _rmsnorm(x, gain):
    ms = jnp.mean(x * x, axis=-1, keepdims=True)
    return x * lax.rsqrt(ms + EPS) * gain


def _dot(a, b):
    return jnp.dot(a, b, preferred_element_type=F32)


def _dot_nt(a, b):
    return lax.dot_general(a, b, (((1,), (1,)), ((), ())), preferred_element_type=F32)


def _dot_tn(a, b):
    return lax.dot_general(a, b, (((0,), (0,)), ((), ())), preferred_element_type=F32)


def _swiglu(xn_bf16, wgu_ref, wd_ref):
    split = (D_FF // (2 * MXU_DIM) + 1) * MXU_DIM
    y = None
    for lo, hi in ((0, split), (split, D_FF)):
        a = _dot(xn_bf16, wgu_ref[:, lo:hi])
        b = _dot(xn_bf16, wgu_ref[:, D_FF + lo:D_FF + hi])
        act = (a * _sigmoid(a) * b).astype(BF16)
        part = _dot(act, wd_ref[lo:hi, :])
        y = part if y is None else y + part
    return y


W_IN_SRC_OF_DST = (0, 2, 4, 5, 6, 1, 3, 13, 14, 15, 16, 7, 8, 9, 10, 11, 12)
W_IN_Q_BLOCKS = (4, 7, 10)
N_NAT_BLOCKS = (NAT_BF16_COLS + NAT_F32_COLS) // COL_BLOCK
N_PERM_BLOCKS = len(W_IN_SRC_OF_DST) - N_NAT_BLOCKS


def _ffn1_kernel(x_ref, g_ref, wgu_ref, wd_ref, win_ref, o_ref, wall_ref):
    for half in range(2):
        rows = slice(half * (ROW_TILE // 2), (half + 1) * (ROW_TILE // 2))
        x = x_ref[rows, :]
        xn = _rmsnorm(x, g_ref[...]).astype(BF16)
        o_ref[rows, :] = x + 0.5 * _swiglu(xn, wgu_ref, wd_ref)

    w = win_ref[...]
    wall_ref[...] = jnp.concatenate(
        [w[:, s * COL_BLOCK:(s + 1) * COL_BLOCK] * (Q_SCALE if s in W_IN_Q_BLOCKS else 1.0)
         for s in W_IN_SRC_OF_DST], axis=1).astype(BF16)


def _resident(shape):
    return pl.BlockSpec(shape, lambda *_: (0,) * len(shape), pipeline_mode=pl.Buffered(1))


BF16_ROWS = 2 * SUBLANES


def _side_cast_specs(weights, grid):
    n_steps = 1
    for g in grid:
        n_steps *= g
    in_specs, out_specs, out_shape = [], [], []
    for w in weights:
        rows, cols = w.shape
        per_step = BF16_ROWS
        while rows % per_step or per_step * n_steps < rows:
            per_step += BF16_ROWS
        last = rows // per_step - 1

        def index(*ids, last=last):
            step = ids[0]
            for g, i in zip(grid[1:], ids[1:]):
                step = step * g + i
            return (jnp.minimum(step, last), 0)

        in_specs.append(pl.BlockSpec((per_step, cols), index))
        out_specs.append(pl.BlockSpec((per_step, cols), index))
        out_shape.append(jax.ShapeDtypeStruct((rows, cols), BF16))
    return in_specs, out_specs, out_shape


def _side_cast(in_refs, out_refs):
    for i_ref, o_ref in zip(in_refs, out_refs):
        o_ref[...] = i_ref[...].astype(BF16)


def _ffn1(x2d, gain, wgu, wd, w_in):
    t = x2d.shape[0]
    grid = (t // ROW_TILE,)
    side_in, side_out, side_shape = _side_cast_specs([w_in], grid)
    return pl.pallas_call(
        _ffn1_kernel,
        grid=grid,
        in_specs=[
            pl.BlockSpec((ROW_TILE, D_MODEL), lambda i: (i, 0)),
            _resident((1, D_MODEL)),
            _resident((D_MODEL, 2 * D_FF)),
            _resident((D_FF, D_MODEL)),
        ] + side_in,
        out_specs=[pl.BlockSpec((ROW_TILE, D_MODEL), lambda i: (i, 0))] + side_out,
        out_shape=[jax.ShapeDtypeStruct((t, D_MODEL), F32)] + side_shape,
        compiler_params=pltpu.CompilerParams(
            dimension_semantics=("arbitrary",), vmem_limit_bytes=VMEM_LIMIT),
        name="ffn1",
    )(x2d, gain, wgu, wd, w_in)


def _proj_nat_kernel(h_ref, g_ref, w_ref, oa_ref, ob_ref, slab_ref):
    for half in range(2):
        rows = slice(half * (ROW_TILE // 2), (half + 1) * (ROW_TILE // 2))
        u = _rmsnorm(h_ref[rows, :], g_ref[...])
        for s in range(D_MODEL // LANES):
            slab_ref[s, rows, :] = u[:, s * LANES:(s + 1) * LANES]
        ub = u.astype(BF16)
        for c in range(NAT_BF16_COLS // COL_BLOCK):
            cols = slice(c * COL_BLOCK, (c + 1) * COL_BLOCK)
            oa_ref[rows, cols] = _dot(ub, w_ref[:, cols]).astype(BF16)
        for c in range(NAT_F32_COLS // COL_BLOCK):
            ob_ref[rows, c * COL_BLOCK:(c + 1) * COL_BLOCK] = _dot(
                ub, w_ref[:, NAT_BF16_COLS + c * COL_BLOCK:NAT_BF16_COLS + (c + 1) * COL_BLOCK])


def _proj_nat(h1, gain, w_nat):
    t = h1.shape[0]
    n_slabs = D_MODEL // LANES
    return pl.pallas_call(
        _proj_nat_kernel,
        grid=(t // ROW_TILE,),
        in_specs=[
            pl.BlockSpec((ROW_TILE, D_MODEL), lambda i: (i, 0)),
            _resident((1, D_MODEL)),
            _resident((D_MODEL, NAT_BF16_COLS + NAT_F32_COLS)),
        ],
        out_specs=[
            pl.BlockSpec((ROW_TILE, NAT_BF16_COLS), lambda i: (i, 0)),
            pl.BlockSpec((ROW_TILE, NAT_F32_COLS), lambda i: (i, 0)),
            pl.BlockSpec((n_slabs, ROW_TILE, LANES), lambda i: (0, i, 0)),
        ],
        out_shape=[
            jax.ShapeDtypeStruct((t, NAT_BF16_COLS), BF16),
            jax.ShapeDtypeStruct((t, NAT_F32_COLS), F32),
            jax.ShapeDtypeStruct((n_slabs, t, LANES), F32),
        ],
        compiler_params=pltpu.CompilerParams(
            dimension_semantics=("parallel",), vmem_limit_bytes=VMEM_LIMIT),
        name="proj_nat",
    )(h1, gain, w_nat)


def _proj_perm_kernel(slab_ref, *refs, seq):
    w_refs, o_ref = refs[:-1], refs[-1]
    t = pl.program_id(1)
    n_slabs = D_MODEL // LANES
    rows_out = o_ref.shape[1]

    def gather(g):
        d = ATT_DILATIONS[g]
        run = seq // d
        per_tile = rows_out // run if run < rows_out else 1
        pieces = []
        for c in range(per_tile):
            if run >= rows_out:
                r = (t * rows_out) // run
                start = r + d * ((t * rows_out) % run)
                n_rows = rows_out
            else:
                start = t * per_tile + c
                n_rows = run
            pieces.append(jnp.concatenate(
                [slab_ref[s, pl.ds(start, n_rows, stride=d), :].astype(BF16) for s in range(n_slabs)], axis=1))
        return pieces[0] if len(pieces) == 1 else jnp.concatenate(pieces, axis=0)

    lhs = {g: gather(g) for g in (1, 2)}
    per_group = 3 * ATT_WIDTH // COL_BLOCK
    for g in (1, 2):
        for c in range(per_group):
            blk = (g - 1) * per_group + c
            o_ref[0, :, blk * COL_BLOCK:(blk + 1) * COL_BLOCK] = _dot(lhs[g], w_refs[blk][...]).astype(BF16)


def _proj_perm(u_slab, w_all, b, seq):
    n_slabs = D_MODEL // LANES
    n_cols = N_PERM_BLOCKS * COL_BLOCK
    return pl.pallas_call(
        functools.partial(_proj_perm_kernel, seq=seq),
        grid=(b, seq // ROW_TILE),
        in_specs=[pl.BlockSpec((n_slabs, seq, LANES), lambda i, j: (0, i, 0))]
        + [pl.BlockSpec((D_MODEL, COL_BLOCK), lambda i, j, c=c: (0, N_NAT_BLOCKS + c), pipeline_mode=pl.Buffered(1))
           for c in range(N_PERM_BLOCKS)],
        out_specs=pl.BlockSpec((1, ROW_TILE, n_cols), lambda i, j: (i, j, 0)),
        out_shape=jax.ShapeDtypeStruct((b, seq, n_cols), BF16),
        compiler_params=pltpu.CompilerParams(
            dimension_semantics=("parallel", "arbitrary"), vmem_limit_bytes=VMEM_LIMIT),
        name="proj_perm",
    )(u_slab, *([w_all] * N_PERM_BLOCKS))


def _bcast_rows(x, block, row):
    n = x.shape[0] // block
    parts = [jnp.broadcast_to(x[i * block + row:i * block + row + 1, :], (block, x.shape[1]))
             for i in range(n)]
    return parts[0] if n == 1 else jnp.concatenate(parts, axis=0)


def _hgrn2_kernel(q_ref, i_ref, f_ref, og_ref, lbp_ref, gain_ref, *refs):
    n_side = (len(refs) - 2) // 2
    o_ref, st_ref = refs[n_side], refs[-1]
    _side_cast(refs[:n_side], refs[n_side + 1:-1])
    c_len = HG_CHUNK

    @pl.when(pl.program_id(1) == 0)
    def _():
        st_ref[...] = jnp.zeros_like(st_ref)

    lbp = lbp_ref[...]
    e = jnp.exp(lbp - jnp.max(lbp, axis=0, keepdims=True))
    lb_all = e[0:1, :] / jnp.sum(e, axis=0, keepdims=True)
    gain_all = gain_ref[...]

    t_idx = lax.broadcasted_iota(jnp.int32, (c_len, c_len), 0)
    s_idx = lax.broadcasted_iota(jnp.int32, (c_len, c_len), 1)
    halves = [1 << i for i in range(c_len.bit_length() - 1)]
    level_masks = [((t_idx >> b.bit_length()) == (s_idx >> b.bit_length()))
                   & ((t_idx & b) != 0) & ((s_idx & b) == 0) for b in halves]
    diag_mask = t_idx == s_idx
    ones_rhs = jnp.ones((HG_DIM, c_len), BF16)
    sub = lax.broadcasted_iota(jnp.int32, (c_len, HG_WIDTH), 0) & (SUBLANES - 1)
    pair = sub & 6

    def chunk(c):
        rows = slice(c * c_len, (c + 1) * c_len)
        q = q_ref[0, rows, :].astype(F32)
        v = i_ref[0, rows, :]
        f = lb_all + (1.0 - lb_all) * _sigmoid(f_ref[0, rows, :])
        k = 1.0 - f

        operands = [(q * f, k)]
        odd = (sub & 1) == 1
        pq = f * jnp.where(odd, pltpu.roll(f, 1, axis=0), 1.0)
        sk = jnp.where(odd, 1.0, pltpu.roll(f, c_len - 1, axis=0))
        operands.append((q * pq, k * sk))
        r1, r3, r5, r7 = (_bcast_rows(pq, SUBLANES, r) for r in (1, 3, 5, 7))
        sk = sk * jnp.where(pair == 0, r3, jnp.where(pair == 4, r7, 1.0))
        pq = pq * jnp.where(pair == 2, r1, jnp.where(pair == 6, r5, 1.0))
        operands.append((q * pq, k * sk))
        r3, r7 = _bcast_rows(pq, SUBLANES, 3), _bcast_rows(pq, SUBLANES, 7)
        sk = sk * jnp.where(sub < 4, r7, 1.0)
        pq = pq * jnp.where(sub >= 4, r3, 1.0)
        b = SUBLANES
        while b < c_len:
            n = c_len // b
            tot = _bcast_rows(pq, b, b - 1)
            zero = jnp.zeros((b, HG_WIDTH), F32)

            def blk(x, i, b=b):
                return x[i * b:(i + 1) * b]

            operands.append((
                jnp.concatenate([zero if i % 2 == 0 else blk(q, i) * blk(pq, i) for i in range(n)], axis=0),
                jnp.concatenate([blk(k, i) * blk(sk, i) if i % 2 == 0 else zero for i in range(n)], axis=0)))
            pq, sk = (
                jnp.concatenate([blk(pq, i) if i % 2 == 0 else blk(pq, i) * blk(tot, i - 1)
                                 for i in range(n)], axis=0),
                jnp.concatenate([blk(sk, i) * blk(tot, i + 1) if i % 2 == 0 else blk(sk, i)
                                 for i in range(n)], axis=0))
            b *= 2
        operands = [(a.astype(BF16), kk.astype(BF16)) for a, kk in operands]
        qk = (q * k).astype(BF16)
        qg = (q * pq).astype(BF16)
        kg = (k * sk).astype(BF16)
        chunk_decay = pq[c_len - 1:c_len, :]

        outs = []
        for h in range(HG_HEADS):
            cols = slice(h * HG_DIM, (h + 1) * HG_DIM)
            a = jnp.where(diag_mask, _dot(qk[:, cols], ones_rhs), 0.0)
            for (ql, kl), mask in zip(operands, level_masks):
                a = jnp.where(mask, _dot_nt(ql[:, cols], kl[:, cols]), a)
            st = st_ref[h]
            o = _dot(a.astype(BF16), v[:, cols]) + _dot_nt(qg[:, cols], st.astype(BF16))
            st_ref[h] = st * chunk_decay[:, cols] + _dot_tn(v[:, cols], kg[:, cols])
            outs.append(o * lax.rsqrt(jnp.mean(o * o, axis=-1, keepdims=True) + EPS))
        og = og_ref[0, rows, :]
        o_ref[0, rows, :] = (jnp.concatenate(outs, axis=1) * gain_all * (og * _sigmoid(og))).astype(BF16)

    for c in range(ROW_TILE // c_len):
        chunk(c)


def _hgrn2(proj_a, proj_b, lower_bounds, out_gain, side_weights):
    b, seq, _ = proj_a.shape
    blk = (1, ROW_TILE, HG_WIDTH)
    grid = (b, seq // ROW_TILE)
    side_in, side_out, side_shape = _side_cast_specs(side_weights, grid)
    return pl.pallas_call(
        _hgrn2_kernel,
        grid=grid,
        in_specs=[
            pl.BlockSpec(blk, lambda i, j: (i, j, 0)),
            pl.BlockSpec(blk, lambda i, j: (i, j, 1)),
            pl.BlockSpec(blk, lambda i, j: (i, j, 0)),
            pl.BlockSpec(blk, lambda i, j: (i, j, 1)),
            _resident(lower_bounds.shape),
            _resident((1, HG_WIDTH)),
        ] + side_in,
        out_specs=[pl.BlockSpec(blk, lambda i, j: (i, j, 0))] + side_out,
        out_shape=[jax.ShapeDtypeStruct((b, seq, HG_WIDTH), BF16)] + side_shape,
        scratch_shapes=[pltpu.VMEM((HG_HEADS, HG_DIM, HG_DIM), F32)],
        compiler_params=pltpu.CompilerParams(
            dimension_semantics=("arbitrary", "arbitrary"), vmem_limit_bytes=VMEM_LIMIT),
        name="hgrn2",
    )(proj_a, proj_a, proj_b, proj_b, lower_bounds, out_gain, *side_weights)


def _att_kernel(slope_ref, *refs, seq, n_side):
    qkv_refs = refs[:3 * N_GROUPS]
    y_ref = refs[3 * N_GROUPS + n_side]
    _side_cast(refs[3 * N_GROUPS:3 * N_GROUPS + n_side],
               refs[3 * N_GROUPS + n_side + 1:3 * N_GROUPS + 2 * n_side + 1])
    scr = refs[3 * N_GROUPS + 2 * n_side + 1:]
    o_scr, l_scr = scr[0:N_GROUPS], scr[N_GROUPS:2 * N_GROUPS]
    y_scr, bias_scr = scr[2 * N_GROUPS], scr[2 * N_GROUPS + 1]
    blk = ATT_BLOCK
    n_blocks = seq // blk
    merge_d = ATT_DILATIONS[1]
    assert all(d % merge_d == 0 for d in ATT_DILATIONS[1:])

    qi = lax.broadcasted_iota(jnp.int32, (blk, 2 * blk), 0)
    kj = lax.broadcasted_iota(jnp.int32, (blk, 2 * blk), 1)
    dist = qi + blk - kj
    in_window = (dist >= 0) & (dist <= blk)
    dist_f = dist.astype(F32)
    for gh in range(2 * N_GROUPS):
        alibi = -slope_ref[0, gh:gh + 1, :] * dist_f
        bias_scr[gh] = jnp.where(in_window, alibi, NEG_INF)

    lane = lax.broadcasted_iota(jnp.int32, (blk, LANES), 1)
    first_head = lane < ATT_HEAD_DIM
    ones2 = jnp.ones((2 * blk, LANES), BF16)

    for g in range(N_GROUPS):
        d = ATT_DILATIONS[g]
        per_class = n_blocks // d
        q_ref, k_ref, v_ref = qkv_refs[3 * g:3 * g + 3]

        for idx in range(n_blocks):
            n, r = idx % per_class, idx // per_class
            rows = slice(idx * blk, (idx + 1) * blk)
            krows = rows if n == 0 else slice((idx - 1) * blk, (idx + 1) * blk)
            q = q_ref[0, rows, :]
            k = k_ref[0, krows, :]
            v_aug = jnp.concatenate([v_ref[0, krows, :], ones2[:k.shape[0]]], axis=1)
            pvs, ms = [], []
            for hh in range(2):
                qm = jnp.where(first_head if hh == 0 else ~first_head, q, jnp.zeros_like(q))
                if n == 0:
                    s = _dot_nt(qm, k) + bias_scr[2 * g + hh, :, blk:]
                    m = jnp.max(s, axis=-1, keepdims=True)
                else:
                    s = _dot_nt(qm, k) + bias_scr[2 * g + hh]
                    m = jnp.max(jnp.maximum(s[:, :blk], s[:, blk:]), axis=-1, keepdims=True)
                pvs.append(_dot(jnp.exp2(s - m).astype(BF16), v_aug))
                ms.append(jnp.broadcast_to(m, (blk, LANES)))
            o = jnp.where(first_head, pvs[0][:, :LANES], pvs[1][:, :LANES])
            den = jnp.where(first_head, pvs[0][:, LANES:], pvs[1][:, LANES:])
            mx = jnp.where(first_head, ms[0], ms[1])
            if d == 1:
                dst = rows
            else:
                dst = pl.ds((r % merge_d) * (seq // merge_d) + n * blk * (d // merge_d) + r // merge_d,
                            blk, stride=d // merge_d)
            o_scr[g][dst, :] = o / den
            l_scr[g][dst, :] = mx + jnp.log2(den)

    rc = 256
    for c4 in range(merge_d):
        for j0 in range(0, seq // merge_d, rc):
            rows = slice(c4 * (seq // merge_d) + j0, c4 * (seq // merge_d) + j0 + rc)
            tokens = pl.ds(c4 + merge_d * j0, rc, stride=merge_d)
            ls = [l_scr[0][tokens, :]] + [l_scr[g][rows, :] for g in range(1, N_GROUPS)]
            os_ = [o_scr[0][tokens, :]] + [o_scr[g][rows, :] for g in range(1, N_GROUPS)]
            m = jnp.maximum(jnp.maximum(ls[0], ls[1]), ls[2])
            ws = [jnp.exp2(lg - m) for lg in ls]
            num = ws[0] * os_[0] + ws[1] * os_[1] + ws[2] * os_[2]
            y_scr[tokens, :] = num / (ws[0] + ws[1] + ws[2])
    for c in range(seq // rc):
        rows = slice(c * rc, (c + 1) * rc)
        y_ref[0, rows, :] = y_scr[rows, :].astype(BF16)


def _att(proj_nat, proj_perm, slopes, side_weights):
    b, seq, _ = proj_nat.shape
    pairs = ATT_WIDTH // LANES
    per_tensor = ATT_WIDTH // LANES
    grid = (b, pairs)
    side_in, side_out, side_shape = _side_cast_specs(side_weights, grid)

    def qkv_spec(g, t):
        off = (2 * HG_WIDTH // LANES + t * per_tensor) if g == 0 else (3 * (g - 1) + t) * per_tensor
        return pl.BlockSpec((1, seq, LANES), lambda i, j, off=off: (i, 0, off + j))

    return pl.pallas_call(
        functools.partial(_att_kernel, seq=seq, n_side=len(side_weights)),
        grid=grid,
        in_specs=[pl.BlockSpec((1, SUBLANES, 2 * ATT_BLOCK), lambda i, j: (j, 0, 0))]
        + [qkv_spec(g, t) for g in range(N_GROUPS) for t in range(3)] + side_in,
        out_specs=[pl.BlockSpec((1, seq, LANES), lambda i, j: (i, 0, j))] + side_out,
        out_shape=[jax.ShapeDtypeStruct((b, seq, ATT_WIDTH), BF16)] + side_shape,
        scratch_shapes=[pltpu.VMEM((seq, LANES), F32) for _ in range(2 * N_GROUPS + 1)]
        + [pltpu.VMEM((2 * N_GROUPS, ATT_BLOCK, 2 * ATT_BLOCK), F32)],
        compiler_params=pltpu.CompilerParams(
            dimension_semantics=("arbitrary", "arbitrary"), vmem_limit_bytes=VMEM_LIMIT),
        name="att",
    )(slopes, *([proj_nat] * 3 + [proj_perm] * (3 * (N_GROUPS - 1))), *side_weights)


def _tail_kernel(h1_ref, yhg_ref, yatt_ref, ghg_ref, gatt_ref, wa_ref, wb_ref, wo_ref,
                 g2_ref, wgu_ref, wd_ref, gf_ref, o_ref):
    merged = (_sigmoid(ghg_ref[...]) * _dot(yhg_ref[...], wa_ref[...])
              + _sigmoid(gatt_ref[...]) * _dot(yatt_ref[...], wb_ref[...]))
    h2 = h1_ref[...] + _dot(merged.astype(BF16), wo_ref[...])
    xn = _rmsnorm(h2, g2_ref[...]).astype(BF16)
    h3 = h2 + 0.5 * _swiglu(xn, wgu_ref, wd_ref)
    o_ref[...] = _rmsnorm(h3, gf_ref[...])


def _tail(h1, y_hg, y_att, proj_b, wa, wb, wo, g2, wgu, wd, gf):
    t = h1.shape[0]
    row = lambda width, col=0: pl.BlockSpec((ROW_TILE, width), lambda i, col=col: (i, col))
    return pl.pallas_call(
        _tail_kernel,
        grid=(t // ROW_TILE,),
        in_specs=[
            row(D_MODEL), row(HG_WIDTH), row(ATT_WIDTH),
            row(D_MODEL, 1), row(D_MODEL, 2),
            _resident((HG_WIDTH, D_MODEL)), _resident((ATT_WIDTH, D_MODEL)), _resident((D_MODEL, D_MODEL)),
            _resident((1, D_MODEL)), _resident((D_MODEL, 2 * D_FF)), _resident((D_FF, D_MODEL)),
            _resident((1, D_MODEL)),
        ],
        out_specs=row(D_MODEL),
        out_shape=jax.ShapeDtypeStruct((t, D_MODEL), F32),
        compiler_params=pltpu.CompilerParams(
            dimension_semantics=("parallel",), vmem_limit_bytes=VMEM_LIMIT),
        name="tail",
    )(h1, y_hg, y_att, proj_b, proj_b, wa, wb, wo, g2, wgu, wd, gf)


def _alibi_table():
    n_heads = N_GROUPS * ATT_HEADS
    slopes = jnp.exp2(-ALIBI_MAX * jnp.arange(1, n_heads + 1, dtype=F32) / n_heads)
    slopes = slopes.reshape(N_GROUPS, ATT_HEADS // 2, 2) * jnp.asarray(ATT_DILATIONS, F32)[:, None, None] * LOG2_E
    table = jnp.zeros((ATT_HEADS // 2, SUBLANES, 2 * ATT_BLOCK), F32)
    table = table.at[:, :2 * N_GROUPS, :].set(
        jnp.broadcast_to(slopes.transpose(1, 0, 2).reshape(ATT_HEADS // 2, 2 * N_GROUPS, 1),
                         (ATT_HEADS // 2, 2 * N_GROUPS, 2 * ATT_BLOCK)))
    return table


def kernel(x, ffn1_norm, ffn1_w_gate_up, ffn1_w_down, mix_norm, w_in, hg_lower_bounds, hg_out_norm,
           w_branch_hg, w_branch_att, w_out, ffn2_norm, ffn2_w_gate_up, ffn2_w_down, final_norm):
    b, seq, d = x.shape
    assert d == D_MODEL and seq % (ATT_BLOCK * ATT_DILATIONS[-1]) == 0 and (b * seq) % ROW_TILE == 0
    x2d = x.reshape(b * seq, d)

    h1, w_all = _ffn1(x2d, ffn1_norm[0:1], ffn1_w_gate_up[0].astype(BF16), ffn1_w_down[0].astype(BF16), w_in[0])
    proj_a, proj_b, u_slab = _proj_nat(h1, mix_norm[0:1], w_all)
    proj_perm = _proj_perm(u_slab, w_all, b, seq)
    proj_a = proj_a.reshape(b, seq, -1)
    y_hg, wa, wb, wo = _hgrn2(proj_a, proj_b.reshape(b, seq, -1), hg_lower_bounds.astype(F32), hg_out_norm[0:1],
                              [w_branch_hg[0], w_branch_att[0], w_out[0]])
    y_att, wgu2, wd2 = _att(proj_a, proj_perm, _alibi_table(), [ffn2_w_gate_up[0], ffn2_w_down[0]])
    out = _tail(h1, y_hg.reshape(b * seq, -1), y_att.reshape(b * seq, -1), proj_b, wa, wb, wo,
                ffn2_norm[0:1], wgu2, wd2, final_norm.reshape(1, d))
    return out.reshape(b, seq, d)
```

```python
import functools

import jax
import jax.numpy as jnp
from jax import lax
from jax.experimental import pallas as pl
from jax.experimental.pallas import tpu as pltpu

F32 = jnp.float32
BF16 = jnp.bfloat16

D_MODEL = 1024
D_FF = 2816
HG_HEADS = 4
HG_DIM = 128
HG_WIDTH = HG_HEADS * HG_DIM
ATT_DILATIONS = (1, 4, 16)
ATT_BLOCK = 128
ATT_HEADS = 8
ATT_HEAD_DIM = 64
ATT_WIDTH = ATT_HEADS * ATT_HEAD_DIM
N_GROUPS = 3
ALIBI_MAX = 8.0
EPS = 1e-6
NEG_INF = -1e30
LOG2_E = 1.4426950408889634
Q_SCALE = ATT_HEAD_DIM ** -0.5 * LOG2_E

LANES = 128
SUBLANES = 8
MXU_DIM = 256
VMEM_LIMIT = 56 * 1024 * 1024

ROW_TILE = 512
COL_BLOCK = 512
NAT_BF16_COLS = 2 * HG_WIDTH + 3 * ATT_WIDTH
NAT_F32_COLS = 2 * HG_WIDTH + 2 * D_MODEL
HG_CHUNK = 128


def _sigmoid(x):
    return 1.0 / (1.0 + jnp.exp(-x))


def _rmsnorm(x, gain):
    ms = jnp.mean(x * x, axis=-1, keepdims=True)
    return x * lax.rsqrt(ms + EPS) * gain


def _dot(a, b):
    return jnp.dot(a, b, preferred_element_type=F32)


def _dot_nt(a, b):
    return lax.dot_general(a, b, (((1,), (1,)), ((), ())), preferred_element_type=F32)


def _dot_tn(a, b):
    return lax.dot_general(a, b, (((0,), (0,)), ((), ())), preferred_element_type=F32)


def _swiglu(xn_bf16, wgu_ref, wd_ref):
    split = (D_FF // (2 * MXU_DIM) + 1) * MXU_DIM
    y = None
    for lo, hi in ((0, split), (split, D_FF)):
        a = _dot(xn_bf16, wgu_ref[:, lo:hi])
        b = _dot(xn_bf16, wgu_ref[:, D_FF + lo:D_FF + hi])
        act = (a * _sigmoid(a) * b).astype(BF16)
        part = _dot(act, wd_ref[lo:hi, :])
        y = part if y is None else y + part
    return y


W_IN_SRC_OF_DST = (0, 2, 4, 5, 6, 1, 3, 13, 14, 15, 16, 7, 8, 9, 10, 11, 12)
W_IN_Q_BLOCKS = (4, 7, 10)
N_NAT_BLOCKS = (NAT_BF16_COLS + NAT_F32_COLS) // COL_BLOCK
N_PERM_BLOCKS = len(W_IN_SRC_OF_DST) - N_NAT_BLOCKS


def _ffn1_kernel(x_ref, g_ref, wgu_ref, wd_ref, win_ref, o_ref, wall_ref):
    for half in range(2):
        rows = slice(half * (ROW_TILE // 2), (half + 1) * (ROW_TILE // 2))
        x = x_ref[rows, :]
        xn = _rmsnorm(x, g_ref[...]).astype(BF16)
        o_ref[rows, :] = x + 0.5 * _swiglu(xn, wgu_ref, wd_ref)

    w = win_ref[...]
    wall_ref[...] = jnp.concatenate(
        [w[:, s * COL_BLOCK:(s + 1) * COL_BLOCK] * (Q_SCALE if s in W_IN_Q_BLOCKS else 1.0)
         for s in W_IN_SRC_OF_DST], axis=1).astype(BF16)


def _resident(shape):
    return pl.BlockSpec(shape, lambda *_: (0,) * len(shape), pipeline_mode=pl.Buffered(1))


BF16_ROWS = 2 * SUBLANES


def _side_cast_specs(weights, grid):
    n_steps = 1
    for g in grid:
        n_steps *= g
    in_specs, out_specs, out_shape = [], [], []
    for w in weights:
        rows, cols = w.shape
        per_step = BF16_ROWS
        while rows % per_step or per_step * n_steps < rows:
            per_step += BF16_ROWS
        last = rows // per_step - 1

        def index(*ids, last=last):
            step = ids[0]
            for g, i in zip(grid[1:], ids[1:]):
                step = step * g + i
            return (jnp.minimum(step, last), 0)

        in_specs.append(pl.BlockSpec((per_step, cols), index))
        out_specs.append(pl.BlockSpec((per_step, cols), index))
        out_shape.append(jax.ShapeDtypeStruct((rows, cols), BF16))
    return in_specs, out_specs, out_shape


def _side_cast(in_refs, out_refs):
    for i_ref, o_ref in zip(in_refs, out_refs):
        o_ref[...] = i_ref[...].astype(BF16)


def _ffn1(x2d, gain, wgu, wd, w_in):
    t = x2d.shape[0]
    grid = (t // ROW_TILE,)
    side_in, side_out, side_shape = _side_cast_specs([w_in], grid)
    return pl.pallas_call(
        _ffn1_kernel,
        grid=grid,
        in_specs=[
            pl.BlockSpec((ROW_TILE, D_MODEL), lambda i: (i, 0)),
            _resident((1, D_MODEL)),
            _resident((D_MODEL, 2 * D_FF)),
            _resident((D_FF, D_MODEL)),
        ] + side_in,
        out_specs=[pl.BlockSpec((ROW_TILE, D_MODEL), lambda i: (i, 0))] + side_out,
        out_shape=[jax.ShapeDtypeStruct((t, D_MODEL), F32)] + side_shape,
        compiler_params=pltpu.CompilerParams(
            dimension_semantics=("arbitrary",), vmem_limit_bytes=VMEM_LIMIT),
        name="ffn1",
    )(x2d, gain, wgu, wd, w_in)


def _proj_nat_kernel(h_ref, g_ref, w_ref, oa_ref, ob_ref, slab_ref):
    for half in range(2):
        rows = slice(half * (ROW_TILE // 2), (half + 1) * (ROW_TILE // 2))
        u = _rmsnorm(h_ref[rows, :], g_ref[...])
        for s in range(D_MODEL // LANES):
            slab_ref[s, rows, :] = u[:, s * LANES:(s + 1) * LANES]
        ub = u.astype(BF16)
        for c in range(NAT_BF16_COLS // COL_BLOCK):
            cols = slice(c * COL_BLOCK, (c + 1) * COL_BLOCK)
            oa_ref[rows, cols] = _dot(ub, w_ref[:, cols]).astype(BF16)
        for c in range(NAT_F32_COLS // COL_BLOCK):
            ob_ref[rows, c * COL_BLOCK:(c + 1) * COL_BLOCK] = _dot(
                ub, w_ref[:, NAT_BF16_COLS + c * COL_BLOCK:NAT_BF16_COLS + (c + 1) * COL_BLOCK])


def _proj_nat(h1, gain, w_nat):
    t = h1.shape[0]
    n_slabs = D_MODEL // LANES
    return pl.pallas_call(
        _proj_nat_kernel,
        grid=(t // ROW_TILE,),
        in_specs=[
            pl.BlockSpec((ROW_TILE, D_MODEL), lambda i: (i, 0)),
            _resident((1, D_MODEL)),
            _resident((D_MODEL, NAT_BF16_COLS + NAT_F32_COLS)),
        ],
        out_specs=[
            pl.BlockSpec((ROW_TILE, NAT_BF16_COLS), lambda i: (i, 0)),
            pl.BlockSpec((ROW_TILE, NAT_F32_COLS), lambda i: (i, 0)),
            pl.BlockSpec((n_slabs, ROW_TILE, LANES), lambda i: (0, i, 0)),
        ],
        out_shape=[
            jax.ShapeDtypeStruct((t, NAT_BF16_COLS), BF16),
            jax.ShapeDtypeStruct((t, NAT_F32_COLS), F32),
            jax.ShapeDtypeStruct((n_slabs, t, LANES), F32),
        ],
        compiler_params=pltpu.CompilerParams(
            dimension_semantics=("parallel",), vmem_limit_bytes=VMEM_LIMIT),
        name="proj_nat",
    )(h1, gain, w_nat)


def _proj_perm_kernel(slab_ref, *refs, seq):
    w_refs, o_ref = refs[:-1], refs[-1]
    t = pl.program_id(1)
    n_slabs = D_MODEL // LANES
    rows_out = o_ref.shape[1]

    def gather(g):
        d = ATT_DILATIONS[g]
        run = seq // d
        per_tile = rows_out // run if run < rows_out else 1
        pieces = []
        for c in range(per_tile):
            if run >= rows_out:
                r = (t * rows_out) // run
                start = r + d * ((t * rows_out) % run)
                n_rows = rows_out
            else:
                start = t * per_tile + c
                n_rows = run
            pieces.append(jnp.concatenate(
                [slab_ref[s, pl.ds(start, n_rows, stride=d), :].astype(BF16) for s in range(n_slabs)], axis=1))
        return pieces[0] if len(pieces) == 1 else jnp.concatenate(pieces, axis=0)

    lhs = {g: gather(g) for g in (1, 2)}
    per_group = 3 * ATT_WIDTH // COL_BLOCK
    for g in (1, 2):
        for c in range(per_group):
            blk = (g - 1) * per_group + c
            o_ref[0, :, blk * COL_BLOCK:(blk + 1) * COL_BLOCK] = _dot(lhs[g], w_refs[blk][...]).astype(BF16)


def _proj_perm(u_slab, w_all, b, seq):
    n_slabs = D_MODEL // LANES
    n_cols = N_PERM_BLOCKS * COL_BLOCK
    return pl.pallas_call(
        functools.partial(_proj_perm_kernel, seq=seq),
        grid=(b, seq // ROW_TILE),
        in_specs=[pl.BlockSpec((n_slabs, seq, LANES), lambda i, j: (0, i, 0))]
        + [pl.BlockSpec((D_MODEL, COL_BLOCK), lambda i, j, c=c: (0, N_NAT_BLOCKS + c), pipeline_mode=pl.Buffered(1))
           for c in range(N_PERM_BLOCKS)],
        out_specs=pl.BlockSpec((1, ROW_TILE, n_cols), lambda i, j: (i, j, 0)),
        out_shape=jax.ShapeDtypeStruct((b, seq, n_cols), BF16),
        compiler_params=pltpu.CompilerParams(
            dimension_semantics=("parallel", "arbitrary"), vmem_limit_bytes=VMEM_LIMIT),
        name="proj_perm",
    )(u_slab, *([w_all] * N_PERM_BLOCKS))


def _bcast_rows(x, block, row):
    n = x.shape[0] // block
    parts = [jnp.broadcast_to(x[i * block + row:i * block + row + 1, :], (block, x.shape[1]))
             for i in range(n)]
    return parts[0] if n == 1 else jnp.concatenate(parts, axis=0)


def _hgrn2_kernel(q_ref, i_ref, f_ref, og_ref, lbp_ref, gain_ref, *refs):
    n_side = (len(refs) - 2) // 2
    o_ref, st_ref = refs[n_side], refs[-1]
    _side_cast(refs[:n_side], refs[n_side + 1:-1])
    c_len = HG_CHUNK

    @pl.when(pl.program_id(1) == 0)
    def _():
        st_ref[...] = jnp.zeros_like(st_ref)

    lbp = lbp_ref[...]
    e = jnp.exp(lbp - jnp.max(lbp, axis=0, keepdims=True))
    lb_all = e[0:1, :] / jnp.sum(e, axis=0, keepdims=True)
    gain_all = gain_ref[...]

    t_idx = lax.broadcasted_iota(jnp.int32, (c_len, c_len), 0)
    s_idx = lax.broadcasted_iota(jnp.int32, (c_len, c_len), 1)
    halves = [1 << i for i in range(c_len.bit_length() - 1)]
    level_masks = [((t_idx >> b.bit_length()) == (s_idx >> b.bit_length()))
                   & ((t_idx & b) != 0) & ((s_idx & b) == 0) for b in halves]
    diag_mask = t_idx == s_idx
    ones_rhs = jnp.ones((HG_DIM, c_len), BF16)
    sub = lax.broadcasted_iota(jnp.int32, (c_len, HG_WIDTH), 0) & (SUBLANES - 1)
    pair = sub & 6

    def chunk(c):
        rows = slice(c * c_len, (c + 1) * c_len)
        q = q_ref[0, rows, :].astype(F32)
        v = i_ref[0, rows, :]
        f = lb_all + (1.0 - lb_all) * _sigmoid(f_ref[0, rows, :])
        k = 1.0 - f

        operands = [(q * f, k)]
        odd = (sub & 1) == 1
        pq = f * jnp.where(odd, pltpu.roll(f, 1, axis=0), 1.0)
        sk = jnp.where(odd, 1.0, pltpu.roll(f, c_len - 1, axis=0))
        operands.append((q * pq, k * sk))
        r1, r3, r5, r7 = (_bcast_rows(pq, SUBLANES, r) for r in (1, 3, 5, 7))
        sk = sk * jnp.where(pair == 0, r3, jnp.where(pair == 4, r7, 1.0))
        pq = pq * jnp.where(pair == 2, r1, jnp.where(pair == 6, r5, 1.0))
        operands.append((q * pq, k * sk))
        r3, r7 = _bcast_rows(pq, SUBLANES, 3), _bcast_rows(pq, SUBLANES, 7)
        sk = sk * jnp.where(sub < 4, r7, 1.0)
        pq = pq * jnp.where(sub >= 4, r3, 1.0)
        b = SUBLANES
        while b < c_len:
            n = c_len // b
            tot = _bcast_rows(pq, b, b - 1)
            zero = jnp.zeros((b, HG_WIDTH), F32)

            def blk(x, i, b=b):
                return x[i * b:(i + 1) * b]

            operands.append((
                jnp.concatenate([zero if i % 2 == 0 else blk(q, i) * blk(pq, i) for i in range(n)], axis=0),
                jnp.concatenate([blk(k, i) * blk(sk, i) if i % 2 == 0 else zero for i in range(n)], axis=0)))
            pq, sk = (
                jnp.concatenate([blk(pq, i) if i % 2 == 0 else blk(pq, i) * blk(tot, i - 1)
                                 for i in range(n)], axis=0),
                jnp.concatenate([blk(sk, i) * blk(tot, i + 1) if i % 2 == 0 else blk(sk, i)
                                 for i in range(n)], axis=0))
            b *= 2
        operands = [(a.astype(BF16), kk.astype(BF16)) for a, kk in operands]
        qk = (q * k).astype(BF16)
        qg = (q * pq).astype(BF16)
        kg = (k * sk).astype(BF16)
        chunk_decay = pq[c_len - 1:c_len, :]

        outs = []
        for h in range(HG_HEADS):
            cols = slice(h * HG_DIM, (h + 1) * HG_DIM)
            a = jnp.where(diag_mask, _dot(qk[:, cols], ones_rhs), 0.0)
            for (ql, kl), mask in zip(operands, level_masks):
                a = jnp.where(mask, _dot_nt(ql[:, cols], kl[:, cols]), a)
            st = st_ref[h]
            o = _dot(a.astype(BF16), v[:, cols]) + _dot_nt(qg[:, cols], st.astype(BF16))
            st_ref[h] = st * chunk_decay[:, cols] + _dot_tn(v[:, cols], kg[:, cols])
            outs.append(o * lax.rsqrt(jnp.mean(o * o, axis=-1, keepdims=True) + EPS))
        og = og_ref[0, rows, :]
        o_ref[0, rows, :] = (jnp.concatenate(outs, axis=1) * gain_all * (og * _sigmoid(og))).astype(BF16)

    for c in range(ROW_TILE // c_len):
        chunk(c)


def _hgrn2(proj_a, proj_b, lower_bounds, out_gain, side_weights):
    b, seq, _ = proj_a.shape
    blk = (1, ROW_TILE, HG_WIDTH)
    grid = (b, seq // ROW_TILE)
    side_in, side_out, side_shape = _side_cast_specs(side_weights, grid)
    return pl.pallas_call(
        _hgrn2_kernel,
        grid=grid,
        in_specs=[
            pl.BlockSpec(blk, lambda i, j: (i, j, 0)),
            pl.BlockSpec(blk, lambda i, j: (i, j, 1)),
            pl.BlockSpec(blk, lambda i, j: (i, j, 0)),
            pl.BlockSpec(blk, lambda i, j: (i, j, 1)),
            _resident(lower_bounds.shape),
            _resident((1, HG_WIDTH)),
        ] + side_in,
        out_specs=[pl.BlockSpec(blk, lambda i, j: (i, j, 0))] + side_out,
        out_shape=[jax.ShapeDtypeStruct((b, seq, HG_WIDTH), BF16)] + side_shape,
        scratch_shapes=[pltpu.VMEM((HG_HEADS, HG_DIM, HG_DIM), F32)],
        compiler_params=pltpu.CompilerParams(
            dimension_semantics=("arbitrary", "arbitrary"), vmem_limit_bytes=VMEM_LIMIT),
        name="hgrn2",
    )(proj_a, proj_a, proj_b, proj_b, lower_bounds, out_gain, *side_weights)


def _att_kernel(slope_ref, *refs, seq, n_side):
    qkv_refs = refs[:3 * N_GROUPS]
    y_ref = refs[3 * N_GROUPS + n_side]
    _side_cast(refs[3 * N_GROUPS:3 * N_GROUPS + n_side],
               refs[3 * N_GROUPS + n_side + 1:3 * N_GROUPS + 2 * n_side + 1])
    scr = refs[3 * N_GROUPS + 2 * n_side + 1:]
    o_scr, l_scr = scr[0:N_GROUPS], scr[N_GROUPS:2 * N_GROUPS]
    y_scr, bias_scr = scr[2 * N_GROUPS], scr[2 * N_GROUPS + 1]
    blk = ATT_BLOCK
    n_blocks = seq // blk
    merge_d = ATT_DILATIONS[1]
    assert all(d % merge_d == 0 for d in ATT_DILATIONS[1:])

    qi = lax.broadcasted_iota(jnp.int32, (blk, 2 * blk), 0)
    kj = lax.broadcasted_iota(jnp.int32, (blk, 2 * blk), 1)
    dist = qi + blk - kj
    in_window = (dist >= 0) & (dist <= blk)
    dist_f = dist.astype(F32)
    for gh in range(2 * N_GROUPS):
        alibi = -slope_ref[0, gh:gh + 1, :] * dist_f
        bias_scr[gh] = jnp.where(in_window, alibi, NEG_INF)

    lane = lax.broadcasted_iota(jnp.int32, (blk, LANES), 1)
    first_head = lane < ATT_HEAD_DIM
    ones2 = jnp.ones((2 * blk, LANES), BF16)

    for g in range(N_GROUPS):
        d = ATT_DILATIONS[g]
        per_class = n_blocks // d
        q_ref, k_ref, v_ref = qkv_refs[3 * g:3 * g + 3]

        for idx in range(n_blocks):
            n, r = idx % per_class, idx // per_class
            rows = slice(idx * blk, (idx + 1) * blk)
            krows = rows if n == 0 else slice((idx - 1) * blk, (idx + 1) * blk)
            q = q_ref[0, rows, :]
            k = k_ref[0, krows, :]
            v_aug = jnp.concatenate([v_ref[0, krows, :], ones2[:k.shape[0]]], axis=1)
            pvs, ms = [], []
            for hh in range(2):
                qm = jnp.where(first_head if hh == 0 else ~first_head, q, jnp.zeros_like(q))
                if n == 0:
                    s = _dot_nt(qm, k) + bias_scr[2 * g + hh, :, blk:]
                    m = jnp.max(s, axis=-1, keepdims=True)
                else:
                    s = _dot_nt(qm, k) + bias_scr[2 * g + hh]
                    m = jnp.max(jnp.maximum(s[:, :blk], s[:, blk:]), axis=-1, keepdims=True)
                pvs.append(_dot(jnp.exp2(s - m).astype(BF16), v_aug))
                ms.append(jnp.broadcast_to(m, (blk, LANES)))
            o = jnp.where(first_head, pvs[0][:, :LANES], pvs[1][:, :LANES])
            den = jnp.where(first_head, pvs[0][:, LANES:], pvs[1][:, LANES:])
            mx = jnp.where(first_head, ms[0], ms[1])
            if d == 1:
                dst = rows
            else:
                dst = pl.ds((r % merge_d) * (seq // merge_d) + n * blk * (d // merge_d) + r // merge_d,
                            blk, stride=d // merge_d)
            o_scr[g][dst, :] = o / den
            l_scr[g][dst, :] = mx + jnp.log2(den)

    rc = 256
    for c4 in range(merge_d):
        for j0 in range(0, seq // merge_d, rc):
            rows = slice(c4 * (seq // merge_d) + j0, c4 * (seq // merge_d) + j0 + rc)
            tokens = pl.ds(c4 + merge_d * j0, rc, stride=merge_d)
            ls = [l_scr[0][tokens, :]] + [l_scr[g][rows, :] for g in range(1, N_GROUPS)]
            os_ = [o_scr[0][tokens, :]] + [o_scr[g][rows, :] for g in range(1, N_GROUPS)]
            m = jnp.maximum(jnp.maximum(ls[0], ls[1]), ls[2])
            ws = [jnp.exp2(lg - m) for lg in ls]
            num = ws[0] * os_[0] + ws[1] * os_[1] + ws[2] * os_[2]
            y_scr[tokens, :] = num / (ws[0] + ws[1] + ws[2])
    for c in range(seq // rc):
        rows = slice(c * rc, (c + 1) * rc)
        y_ref[0, rows, :] = y_scr[rows, :].astype(BF16)


def _att(proj_nat, proj_perm, slopes, side_weights):
    b, seq, _ = proj_nat.shape
    pairs = ATT_WIDTH // LANES
    per_tensor = ATT_WIDTH // LANES
    grid = (b, pairs)
    side_in, side_out, side_shape = _side_cast_specs(side_weights, grid)

    def qkv_spec(g, t):
        off = (2 * HG_WIDTH // LANES + t * per_tensor) if g == 0 else (3 * (g - 1) + t) * per_tensor
        return pl.BlockSpec((1, seq, LANES), lambda i, j, off=off: (i, 0, off + j))

    return pl.pallas_call(
        functools.partial(_att_kernel, seq=seq, n_side=len(side_weights)),
        grid=grid,
        in_specs=[pl.BlockSpec((1, SUBLANES, 2 * ATT_BLOCK), lambda i, j: (j, 0, 0))]
        + [qkv_spec(g, t) for g in range(N_GROUPS) for t in range(3)] + side_in,
        out_specs=[pl.BlockSpec((1, seq, LANES), lambda i, j: (i, 0, j))] + side_out,
        out_shape=[jax.ShapeDtypeStruct((b, seq, ATT_WIDTH), BF16)] + side_shape,
        scratch_shapes=[pltpu.VMEM((seq, LANES), F32) for _ in range(2 * N_GROUPS + 1)]
        + [pltpu.VMEM((2 * N_GROUPS, ATT_BLOCK, 2 * ATT_BLOCK), F32)],
        compiler_params=pltpu.CompilerParams(
            dimension_semantics=("arbitrary", "arbitrary"), vmem_limit_bytes=VMEM_LIMIT),
        name="att",
    )(slopes, *([proj_nat] * 3 + [proj_perm] * (3 * (N_GROUPS - 1))), *side_weights)


def _tail_kernel(h1_ref, yhg_ref, yatt_ref, ghg_ref, gatt_ref, wa_ref, wb_ref, wo_ref,
                 g2_ref, wgu_ref, wd_ref, gf_ref, o_ref):
    merged = (_sigmoid(ghg_ref[...]) * _dot(yhg_ref[...], wa_ref[...])
              + _sigmoid(gatt_ref[...]) * _dot(yatt_ref[...], wb_ref[...]))
    h2 = h1_ref[...] + _dot(merged.astype(BF16), wo_ref[...])
    xn = _rmsnorm(h2, g2_ref[...]).astype(BF16)
    h3 = h2 + 0.5 * _swiglu(xn, wgu_ref, wd_ref)
    o_ref[...] = _rmsnorm(h3, gf_ref[...])


def _tail(h1, y_hg, y_att, proj_b, wa, wb, wo, g2, wgu, wd, gf):
    t = h1.shape[0]
    row = lambda width, col=0: pl.BlockSpec((ROW_TILE, width), lambda i, col=col: (i, col))
    return pl.pallas_call(
        _tail_kernel,
        grid=(t // ROW_TILE,),
        in_specs=[
            row(D_MODEL), row(HG_WIDTH), row(ATT_WIDTH),
            row(D_MODEL, 1), row(D_MODEL, 2),
            _resident((HG_WIDTH, D_MODEL)), _resident((ATT_WIDTH, D_MODEL)), _resident((D_MODEL, D_MODEL)),
            _resident((1, D_MODEL)), _resident((D_MODEL, 2 * D_FF)), _resident((D_FF, D_MODEL)),
            _resident((1, D_MODEL)),
        ],
        out_specs=row(D_MODEL),
        out_shape=jax.ShapeDtypeStruct((t, D_MODEL), F32),
        compiler_params=pltpu.CompilerParams(
            dimension_semantics=("parallel",), vmem_limit_bytes=VMEM_LIMIT),
        name="tail",
    )(h1, y_hg, y_att, proj_b, proj_b, wa, wb, wo, g2, wgu, wd, gf)


def _alibi_table():
    n_heads = N_GROUPS * ATT_HEADS
    slopes = jnp.exp2(-ALIBI_MAX * jnp.arange(1, n_heads + 1, dtype=F32) / n_heads)
    slopes = slopes.reshape(N_GROUPS, ATT_HEADS // 2, 2) * jnp.asarray(ATT_DILATIONS, F32)[:, None, None] * LOG2_E
    table = jnp.zeros((ATT_HEADS // 2, SUBLANES, 2 * ATT_BLOCK), F32)
    table = table.at[:, :2 * N_GROUPS, :].set(
        jnp.broadcast_to(slopes.transpose(1, 0, 2).reshape(ATT_HEADS // 2, 2 * N_GROUPS, 1),
                         (ATT_HEADS // 2, 2 * N_GROUPS, 2 * ATT_BLOCK)))
    return table


def kernel(x, ffn1_norm, ffn1_w_gate_up, ffn1_w_down, mix_norm, w_in, hg_lower_bounds, hg_out_norm,
           w_branch_hg, w_branch_att, w_out, ffn2_norm, ffn2_w_gate_up, ffn2_w_down, final_norm):
    b, seq, d = x.shape
    assert d == D_MODEL and seq % (ATT_BLOCK * ATT_DILATIONS[-1]) == 0 and (b * seq) % ROW_TILE == 0
    x2d = x.reshape(b * seq, d)

    h1, w_all = _ffn1(x2d, ffn1_norm[0:1], ffn1_w_gate_up[0].astype(BF16), ffn1_w_down[0].astype(BF16), w_in[0])
    proj_a, proj_b, u_slab = _proj_nat(h1, mix_norm[0:1], w_all)
    proj_perm = _proj_perm(u_slab, w_all, b, seq)
    proj_a = proj_a.reshape(b, seq, -1)
    y_hg, wa, wb, wo = _hgrn2(proj_a, proj_b.reshape(b, seq, -1), hg_lower_bounds.astype(F32), hg_out_norm[0:1],
                              [w_branch_hg[0], w_branch_att[0], w_out[0]])
    y_att, wgu2, wd2 = _att(proj_a, proj_perm, _alibi_table(), [ffn2_w_gate_up[0], ffn2_w_down[0]])
    out = _tail(h1, y_hg.reshape(b * seq, -1), y_att.reshape(b * seq, -1), proj_b, wa, wb, wo,
                ffn2_norm[0:1], wgu2, wd2, final_norm.reshape(1, d))
    return out.reshape(b, seq, d)
```

```python
import functools

import jax
import jax.numpy as jnp
from jax import lax
from jax.experimental import pallas as pl
from jax.experimental.pallas import tpu as pltpu

F32 = jnp.float32
BF16 = jnp.bfloat16

D_MODEL = 1024
D_FF = 2816
HG_HEADS = 4
HG_DIM = 128
HG_WIDTH = HG_HEADS * HG_DIM
ATT_DILATIONS = (1, 4, 16)
ATT_BLOCK = 128
ATT_HEADS = 8
ATT_HEAD_DIM = 64
ATT_WIDTH = ATT_HEADS * ATT_HEAD_DIM
N_GROUPS = 3
ALIBI_MAX = 8.0
EPS = 1e-6
NEG_INF = -1e30
LOG2_E = 1.4426950408889634
Q_SCALE = ATT_HEAD_DIM ** -0.5 * LOG2_E

LANES = 128
SUBLANES = 8
MXU_DIM = 256
VMEM_LIMIT = 56 * 1024 * 1024

ROW_TILE = 512
COL_BLOCK = 512
NAT_BF16_COLS = 2 * HG_WIDTH + 3 * ATT_WIDTH
NAT_F32_COLS = 2 * HG_WIDTH + 2 * D_MODEL
HG_CHUNK = 128


def _sigmoid(x):
    return 1.0 / (1.0 + jnp.exp(-x))


def _rmsnorm(x, gain):
    ms = jnp.mean(x * x, axis=-1, keepdims=True)
    return x * lax.rsqrt(ms + EPS) * gain


def _dot(a, b):
    return jnp.dot(a, b, preferred_element_type=F32)


def _dot_nt(a, b):
    return lax.dot_general(a, b, (((1,), (1,)), ((), ())), preferred_element_type=F32)


def _dot_tn(a, b):
    return lax.dot_general(a, b, (((0,), (0,)), ((), ())), preferred_element_type=F32)


def _swiglu(xn_bf16, wgu_ref, wd_ref):
    split = (D_FF // (2 * MXU_DIM) + 1) * MXU_DIM
    y = None
    for lo, hi in ((0, split), (split, D_FF)):
        a = _dot(xn_bf16, wgu_ref[:, lo:hi])
        b = _dot(xn_bf16, wgu_ref[:, D_FF + lo:D_FF + hi])
        act = (a * _sigmoid(a) * b).astype(BF16)
        part = _dot(act, wd_ref[lo:hi, :])
        y = part if y is None else y + part
    return y


W_IN_SRC_OF_DST = (0, 2, 4, 5, 6, 1, 3, 13, 14, 15, 16, 7, 8, 9, 10, 11, 12)
W_IN_Q_BLOCKS = (4, 7, 10)
N_NAT_BLOCKS = (NAT_BF16_COLS + NAT_F32_COLS) // COL_BLOCK
N_PERM_BLOCKS = len(W_IN_SRC_OF_DST) - N_NAT_BLOCKS


def _ffn1_kernel(x_ref, g_ref, wgu_ref, wd_ref, win_ref, o_ref, wall_ref):
    for half in range(2):
        rows = slice(half * (ROW_TILE // 2), (half + 1) * (ROW_TILE // 2))
        x = x_ref[rows, :]
        xn = _rmsnorm(x, g_ref[...]).astype(BF16)
        o_ref[rows, :] = x + 0.5 * _swiglu(xn, wgu_ref, wd_ref)

    w = win_ref[...]
    wall_ref[...] = jnp.concatenate(
        [w[:, s * COL_BLOCK:(s + 1) * COL_BLOCK] * (Q_SCALE if s in W_IN_Q_BLOCKS else 1.0)
         for s in W_IN_SRC_OF_DST], axis=1).astype(BF16)


def _resident(shape):
    return pl.BlockSpec(shape, lambda *_: (0,) * len(shape), pipeline_mode=pl.Buffered(1))


BF16_ROWS = 2 * SUBLANES


def _side_cast_specs(weights, grid):
    n_steps = 1
    for g in grid:
        n_steps *= g
    in_specs, out_specs, out_shape = [], [], []
    for w in weights:
        rows, cols = w.shape
        per_step = BF16_ROWS
        while rows % per_step or per_step * n_steps < rows:
            per_step += BF16_ROWS
        last = rows // per_step - 1

        def index(*ids, last=last):
            step = ids[0]
            for g, i in zip(grid[1:], ids[1:]):
                step = step * g + i
            return (jnp.minimum(step, last), 0)

        in_specs.append(pl.BlockSpec((per_step, cols), index))
        out_specs.append(pl.BlockSpec((per_step, cols), index))
        out_shape.append(jax.ShapeDtypeStruct((rows, cols), BF16))
    return in_specs, out_specs, out_shape


def _side_cast(in_refs, out_refs):
    for i_ref, o_ref in zip(in_refs, out_refs):
        o_ref[...] = i_ref[...].astype(BF16)


def _ffn1(x2d, gain, wgu, wd, w_in):
    t = x2d.shape[0]
    grid = (t // ROW_TILE,)
    side_in, side_out, side_shape = _side_cast_specs([w_in], grid)
    return pl.pallas_call(
        _ffn1_kernel,
        grid=grid,
        in_specs=[
            pl.BlockSpec((ROW_TILE, D_MODEL), lambda i: (i, 0)),
            _resident((1, D_MODEL)),
            _resident((D_MODEL, 2 * D_FF)),
            _resident((D_FF, D_MODEL)),
        ] + side_in,
        out_specs=[pl.BlockSpec((ROW_TILE, D_MODEL), lambda i: (i, 0))] + side_out,
        out_shape=[jax.ShapeDtypeStruct((t, D_MODEL), F32)] + side_shape,
        compiler_params=pltpu.CompilerParams(
            dimension_semantics=("arbitrary",), vmem_limit_bytes=VMEM_LIMIT),
        name="ffn1",
    )(x2d, gain, wgu, wd, w_in)


NAT_OUTPUTS = ((1, BF16), (1, BF16), (3, BF16), (1, F32), (1, F32), (4, F32))


def _proj_nat_kernel(h_ref, g_ref, w_ref, *refs):
    out_refs, slab_ref = refs[:-1], refs[-1]
    for half in range(2):
        rows = slice(half * (ROW_TILE // 2), (half + 1) * (ROW_TILE // 2))
        u = _rmsnorm(h_ref[rows, :], g_ref[...])
        for s in range(D_MODEL // LANES):
            slab_ref[s, rows, :] = u[:, s * LANES:(s + 1) * LANES]
        ub = u.astype(BF16)
        c = 0
        for (n_blocks, dtype), o_ref in zip(NAT_OUTPUTS, out_refs):
            for j in range(n_blocks):
                o_ref[rows, j * COL_BLOCK:(j + 1) * COL_BLOCK] = _dot(
                    ub, w_ref[:, c * COL_BLOCK:(c + 1) * COL_BLOCK]).astype(dtype)
                c += 1


def _proj_nat(h1, gain, w_nat):
    t = h1.shape[0]
    n_slabs = D_MODEL // LANES
    return pl.pallas_call(
        _proj_nat_kernel,
        grid=(t // ROW_TILE,),
        in_specs=[
            pl.BlockSpec((ROW_TILE, D_MODEL), lambda i: (i, 0)),
            _resident((1, D_MODEL)),
            _resident((D_MODEL, NAT_BF16_COLS + NAT_F32_COLS)),
        ],
        out_specs=[pl.BlockSpec((ROW_TILE, n * COL_BLOCK), lambda i: (i, 0)) for n, _ in NAT_OUTPUTS]
        + [pl.BlockSpec((n_slabs, ROW_TILE, LANES), lambda i: (0, i, 0))],
        out_shape=[jax.ShapeDtypeStruct((t, n * COL_BLOCK), dtype) for n, dtype in NAT_OUTPUTS]
        + [jax.ShapeDtypeStruct((n_slabs, t, LANES), F32)],
        compiler_params=pltpu.CompilerParams(
            dimension_semantics=("parallel",), vmem_limit_bytes=VMEM_LIMIT),
        name="proj_nat",
    )(h1, gain, w_nat)


def _proj_perm_kernel(slab_ref, *refs, seq):
    w_refs, o_ref = refs[:-1], refs[-1]
    t = pl.program_id(1)
    n_slabs = D_MODEL // LANES
    rows_out = o_ref.shape[1]

    def gather(g):
        d = ATT_DILATIONS[g]
        run = seq // d
        per_tile = rows_out // run if run < rows_out else 1
        pieces = []
        for c in range(per_tile):
            if run >= rows_out:
                r = (t * rows_out) // run
                start = r + d * ((t * rows_out) % run)
                n_rows = rows_out
            else:
                start = t * per_tile + c
                n_rows = run
            pieces.append(jnp.concatenate(
                [slab_ref[s, pl.ds(start, n_rows, stride=d), :].astype(BF16) for s in range(n_slabs)], axis=1))
        return pieces[0] if len(pieces) == 1 else jnp.concatenate(pieces, axis=0)

    lhs = {g: gather(g) for g in (1, 2)}
    per_group = 3 * ATT_WIDTH // COL_BLOCK
    for g in (1, 2):
        for c in range(per_group):
            blk = (g - 1) * per_group + c
            o_ref[0, :, blk * COL_BLOCK:(blk + 1) * COL_BLOCK] = _dot(lhs[g], w_refs[blk][...]).astype(BF16)


def _proj_perm(u_slab, w_all, b, seq):
    n_slabs = D_MODEL // LANES
    n_cols = N_PERM_BLOCKS * COL_BLOCK
    return pl.pallas_call(
        functools.partial(_proj_perm_kernel, seq=seq),
        grid=(b, seq // ROW_TILE),
        in_specs=[pl.BlockSpec((n_slabs, seq, LANES), lambda i, j: (0, i, 0))]
        + [pl.BlockSpec((D_MODEL, COL_BLOCK), lambda i, j, c=c: (0, N_NAT_BLOCKS + c), pipeline_mode=pl.Buffered(1))
           for c in range(N_PERM_BLOCKS)],
        out_specs=pl.BlockSpec((1, ROW_TILE, n_cols), lambda i, j: (i, j, 0)),
        out_shape=jax.ShapeDtypeStruct((b, seq, n_cols), BF16),
        compiler_params=pltpu.CompilerParams(
            dimension_semantics=("parallel", "arbitrary"), vmem_limit_bytes=VMEM_LIMIT),
        name="proj_perm",
    )(u_slab, *([w_all] * N_PERM_BLOCKS))


def _bcast_rows(x, block, row):
    n = x.shape[0] // block
    parts = [jnp.broadcast_to(x[i * block + row:i * block + row + 1, :], (block, x.shape[1]))
             for i in range(n)]
    return parts[0] if n == 1 else jnp.concatenate(parts, axis=0)


def _hgrn2_kernel(q_ref, i_ref, f_ref, og_ref, lbp_ref, gain_ref, *refs):
    n_side = (len(refs) - 2) // 2
    o_ref, st_ref = refs[n_side], refs[-1]
    _side_cast(refs[:n_side], refs[n_side + 1:-1])
    c_len = HG_CHUNK

    @pl.when(pl.program_id(1) == 0)
    def _():
        st_ref[...] = jnp.zeros_like(st_ref)

    lbp = lbp_ref[...]
    e = jnp.exp(lbp - jnp.max(lbp, axis=0, keepdims=True))
    lb_all = e[0:1, :] / jnp.sum(e, axis=0, keepdims=True)
    gain_all = gain_ref[...]

    t_idx = lax.broadcasted_iota(jnp.int32, (c_len, c_len), 0)
    s_idx = lax.broadcasted_iota(jnp.int32, (c_len, c_len), 1)
    halves = [1 << i for i in range(c_len.bit_length() - 1)]
    level_masks = [((t_idx >> b.bit_length()) == (s_idx >> b.bit_length()))
                   & ((t_idx & b) != 0) & ((s_idx & b) == 0) for b in halves]
    diag_mask = t_idx == s_idx
    ones_rhs = jnp.ones((HG_DIM, c_len), BF16)
    sub = lax.broadcasted_iota(jnp.int32, (c_len, HG_WIDTH), 0) & (SUBLANES - 1)
    pair = sub & 6

    def chunk(c):
        rows = slice(c * c_len, (c + 1) * c_len)
        q = q_ref[0, rows, :].astype(F32)
        v = i_ref[0, rows, :]
        f = lb_all + (1.0 - lb_all) * _sigmoid(f_ref[0, rows, :])
        k = 1.0 - f

        operands = [(q * f, k)]
        odd = (sub & 1) == 1
        pq = f * jnp.where(odd, pltpu.roll(f, 1, axis=0), 1.0)
        sk = jnp.where(odd, 1.0, pltpu.roll(f, c_len - 1, axis=0))
        operands.append((q * pq, k * sk))
        r1, r3, r5, r7 = (_bcast_rows(pq, SUBLANES, r) for r in (1, 3, 5, 7))
        sk = sk * jnp.where(pair == 0, r3, jnp.where(pair == 4, r7, 1.0))
        pq = pq * jnp.where(pair == 2, r1, jnp.where(pair == 6, r5, 1.0))
        operands.append((q * pq, k * sk))
        r3, r7 = _bcast_rows(pq, SUBLANES, 3), _bcast_rows(pq, SUBLANES, 7)
        sk = sk * jnp.where(sub < 4, r7, 1.0)
        pq = pq * jnp.where(sub >= 4, r3, 1.0)
        b = SUBLANES
        while b < c_len:
            n = c_len // b
            tot = _bcast_rows(pq, b, b - 1)
            zero = jnp.zeros((b, HG_WIDTH), F32)

            def blk(x, i, b=b):
                return x[i * b:(i + 1) * b]

            operands.append((
                jnp.concatenate([zero if i % 2 == 0 else blk(q, i) * blk(pq, i) for i in range(n)], axis=0),
                jnp.concatenate([blk(k, i) * blk(sk, i) if i % 2 == 0 else zero for i in range(n)], axis=0)))
            pq, sk = (
                jnp.concatenate([blk(pq, i) if i % 2 == 0 else blk(pq, i) * blk(tot, i - 1)
                                 for i in range(n)], axis=0),
                jnp.concatenate([blk(sk, i) * blk(tot, i + 1) if i % 2 == 0 else blk(sk, i)
                                 for i in range(n)], axis=0))
            b *= 2
        operands = [(a.astype(BF16), kk.astype(BF16)) for a, kk in operands]
        qk = (q * k).astype(BF16)
        qg = (q * pq).astype(BF16)
        kg = (k * sk).astype(BF16)
        chunk_decay = pq[c_len - 1:c_len, :]

        outs = []
        for h in range(HG_HEADS):
            cols = slice(h * HG_DIM, (h + 1) * HG_DIM)
            a = jnp.where(diag_mask, _dot(qk[:, cols], ones_rhs), 0.0)
            for (ql, kl), mask in zip(operands, level_masks):
                a = jnp.where(mask, _dot_nt(ql[:, cols], kl[:, cols]), a)
            st = st_ref[h]
            o = _dot(a.astype(BF16), v[:, cols]) + _dot_nt(qg[:, cols], st.astype(BF16))
            st_ref[h] = st * chunk_decay[:, cols] + _dot_tn(v[:, cols], kg[:, cols])
            outs.append(o * lax.rsqrt(jnp.mean(o * o, axis=-1, keepdims=True) + EPS))
        og = og_ref[0, rows, :]
        o_ref[0, rows, :] = (jnp.concatenate(outs, axis=1) * gain_all * (og * _sigmoid(og))).astype(BF16)

    for c in range(ROW_TILE // c_len):
        chunk(c)


def _hgrn2(hg_q, hg_i, hg_f, hg_og, lower_bounds, out_gain, side_weights):
    b, seq, _ = hg_q.shape
    blk = (1, ROW_TILE, HG_WIDTH)
    grid = (b, seq // ROW_TILE)
    side_in, side_out, side_shape = _side_cast_specs(side_weights, grid)
    return pl.pallas_call(
        _hgrn2_kernel,
        grid=grid,
        in_specs=[
            pl.BlockSpec(blk, lambda i, j: (i, j, 0)),
            pl.BlockSpec(blk, lambda i, j: (i, j, 0)),
            pl.BlockSpec(blk, lambda i, j: (i, j, 0)),
            pl.BlockSpec(blk, lambda i, j: (i, j, 0)),
            _resident(lower_bounds.shape),
            _resident((1, HG_WIDTH)),
        ] + side_in,
        out_specs=[pl.BlockSpec(blk, lambda i, j: (i, j, 0))] + side_out,
        out_shape=[jax.ShapeDtypeStruct((b, seq, HG_WIDTH), BF16)] + side_shape,
        scratch_shapes=[pltpu.VMEM((HG_HEADS, HG_DIM, HG_DIM), F32)],
        compiler_params=pltpu.CompilerParams(
            dimension_semantics=("arbitrary", "arbitrary"), vmem_limit_bytes=VMEM_LIMIT),
        name="hgrn2",
    )(hg_q, hg_i, hg_f, hg_og, lower_bounds, out_gain, *side_weights)


def _att_kernel(slope_ref, *refs, seq, n_side):
    qkv_refs = refs[:3 * N_GROUPS]
    y_ref = refs[3 * N_GROUPS + n_side]
    _side_cast(refs[3 * N_GROUPS:3 * N_GROUPS + n_side],
               refs[3 * N_GROUPS + n_side + 1:3 * N_GROUPS + 2 * n_side + 1])
    scr = refs[3 * N_GROUPS + 2 * n_side + 1:]
    o_scr, l_scr = scr[0:N_GROUPS], scr[N_GROUPS:2 * N_GROUPS]
    y_scr, bias_scr = scr[2 * N_GROUPS], scr[2 * N_GROUPS + 1]
    blk = ATT_BLOCK
    n_blocks = seq // blk
    merge_d = ATT_DILATIONS[1]
    assert all(d % merge_d == 0 for d in ATT_DILATIONS[1:])

    qi = lax.broadcasted_iota(jnp.int32, (blk, 2 * blk), 0)
    kj = lax.broadcasted_iota(jnp.int32, (blk, 2 * blk), 1)
    dist = qi + blk - kj
    in_window = (dist >= 0) & (dist <= blk)
    dist_f = dist.astype(F32)
    for gh in range(2 * N_GROUPS):
        alibi = -slope_ref[0, gh:gh + 1, :] * dist_f
        bias_scr[gh] = jnp.where(in_window, alibi, NEG_INF)

    lane = lax.broadcasted_iota(jnp.int32, (blk, LANES), 1)
    first_head = lane < ATT_HEAD_DIM
    ones2 = jnp.ones((2 * blk, LANES), BF16)

    for g in range(N_GROUPS):
        d = ATT_DILATIONS[g]
        per_class = n_blocks // d
        q_ref, k_ref, v_ref = qkv_refs[3 * g:3 * g + 3]

        for idx in range(n_blocks):
            n, r = idx % per_class, idx // per_class
            rows = slice(idx * blk, (idx + 1) * blk)
            krows = rows if n == 0 else slice((idx - 1) * blk, (idx + 1) * blk)
            q = q_ref[0, rows, :]
            k = k_ref[0, krows, :]
            v_aug = jnp.concatenate([v_ref[0, krows, :], ones2[:k.shape[0]]], axis=1)
            pvs, ms = [], []
            for hh in range(2):
                qm = jnp.where(first_head if hh == 0 else ~first_head, q, jnp.zeros_like(q))
                if n == 0:
                    s = _dot_nt(qm, k) + bias_scr[2 * g + hh, :, blk:]
                    m = jnp.max(s, axis=-1, keepdims=True)
                else:
                    s = _dot_nt(qm, k) + bias_scr[2 * g + hh]
                    m = jnp.max(jnp.maximum(s[:, :blk], s[:, blk:]), axis=-1, keepdims=True)
                pvs.append(_dot(jnp.exp2(s - m).astype(BF16), v_aug))
                ms.append(jnp.broadcast_to(m, (blk, LANES)))
            o = jnp.where(first_head, pvs[0][:, :LANES], pvs[1][:, :LANES])
            den = jnp.where(first_head, pvs[0][:, LANES:], pvs[1][:, LANES:])
            mx = jnp.where(first_head, ms[0], ms[1])
            if d == 1:
                dst = rows
            else:
                dst = pl.ds((r % merge_d) * (seq // merge_d) + n * blk * (d // merge_d) + r // merge_d,
                            blk, stride=d // merge_d)
            o_scr[g][dst, :] = o / den
            l_scr[g][dst, :] = mx + jnp.log2(den)

    rc = 256
    for c4 in range(merge_d):
        for j0 in range(0, seq // merge_d, rc):
            rows = slice(c4 * (seq // merge_d) + j0, c4 * (seq // merge_d) + j0 + rc)
            tokens = pl.ds(c4 + merge_d * j0, rc, stride=merge_d)
            ls = [l_scr[0][tokens, :]] + [l_scr[g][rows, :] for g in range(1, N_GROUPS)]
            os_ = [o_scr[0][tokens, :]] + [o_scr[g][rows, :] for g in range(1, N_GROUPS)]
            m = jnp.maximum(jnp.maximum(ls[0], ls[1]), ls[2])
            ws = [jnp.exp2(lg - m) for lg in ls]
            num = ws[0] * os_[0] + ws[1] * os_[1] + ws[2] * os_[2]
            y_scr[tokens, :] = num / (ws[0] + ws[1] + ws[2])
    for c in range(seq // rc):
        rows = slice(c * rc, (c + 1) * rc)
        y_ref[0, rows, :] = y_scr[rows, :].astype(BF16)


def _att(proj_nat, proj_perm, slopes, side_weights):
    b, seq, _ = proj_nat.shape
    pairs = ATT_WIDTH // LANES
    per_tensor = ATT_WIDTH // LANES
    grid = (b, pairs)
    side_in, side_out, side_shape = _side_cast_specs(side_weights, grid)

    def qkv_spec(g, t):
        off = t * per_tensor if g == 0 else (3 * (g - 1) + t) * per_tensor
        return pl.BlockSpec((1, seq, LANES), lambda i, j, off=off: (i, 0, off + j))

    return pl.pallas_call(
        functools.partial(_att_kernel, seq=seq, n_side=len(side_weights)),
        grid=grid,
        in_specs=[pl.BlockSpec((1, SUBLANES, 2 * ATT_BLOCK), lambda i, j: (j, 0, 0))]
        + [qkv_spec(g, t) for g in range(N_GROUPS) for t in range(3)] + side_in,
        out_specs=[pl.BlockSpec((1, seq, LANES), lambda i, j: (i, 0, j))] + side_out,
        out_shape=[jax.ShapeDtypeStruct((b, seq, ATT_WIDTH), BF16)] + side_shape,
        scratch_shapes=[pltpu.VMEM((seq, LANES), F32) for _ in range(2 * N_GROUPS + 1)]
        + [pltpu.VMEM((2 * N_GROUPS, ATT_BLOCK, 2 * ATT_BLOCK), F32)],
        compiler_params=pltpu.CompilerParams(
            dimension_semantics=("arbitrary", "arbitrary"), vmem_limit_bytes=VMEM_LIMIT),
        name="att",
    )(slopes, *([proj_nat] * 3 + [proj_perm] * (3 * (N_GROUPS - 1))), *side_weights)


def _tail_kernel(h1_ref, yhg_ref, yatt_ref, ghg_ref, gatt_ref, wa_ref, wb_ref, wo_ref,
                 g2_ref, wgu_ref, wd_ref, gf_ref, o_ref):
    merged = (_sigmoid(ghg_ref[...]) * _dot(yhg_ref[...], wa_ref[...])
              + _sigmoid(gatt_ref[...]) * _dot(yatt_ref[...], wb_ref[...]))
    h2 = h1_ref[...] + _dot(merged.astype(BF16), wo_ref[...])
    xn = _rmsnorm(h2, g2_ref[...]).astype(BF16)
    h3 = h2 + 0.5 * _swiglu(xn, wgu_ref, wd_ref)
    o_ref[...] = _rmsnorm(h3, gf_ref[...])


def _tail(h1, y_hg, y_att, gates, wa, wb, wo, g2, wgu, wd, gf):
    t = h1.shape[0]
    row = lambda width, col=0: pl.BlockSpec((ROW_TILE, width), lambda i, col=col: (i, col))
    return pl.pallas_call(
        _tail_kernel,
        grid=(t // ROW_TILE,),
        in_specs=[
            row(D_MODEL), row(HG_WIDTH), row(ATT_WIDTH),
            row(D_MODEL, 0), row(D_MODEL, 1),
            _resident((HG_WIDTH, D_MODEL)), _resident((ATT_WIDTH, D_MODEL)), _resident((D_MODEL, D_MODEL)),
            _resident((1, D_MODEL)), _resident((D_MODEL, 2 * D_FF)), _resident((D_FF, D_MODEL)),
            _resident((1, D_MODEL)),
        ],
        out_specs=row(D_MODEL),
        out_shape=jax.ShapeDtypeStruct((t, D_MODEL), F32),
        compiler_params=pltpu.CompilerParams(
            dimension_semantics=("parallel",), vmem_limit_bytes=VMEM_LIMIT),
        name="tail",
    )(h1, y_hg, y_att, gates, gates, wa, wb, wo, g2, wgu, wd, gf)


def _alibi_table():
    n_heads = N_GROUPS * ATT_HEADS
    slopes = jnp.exp2(-ALIBI_MAX * jnp.arange(1, n_heads + 1, dtype=F32) / n_heads)
    slopes = slopes.reshape(N_GROUPS, ATT_HEADS // 2, 2) * jnp.asarray(ATT_DILATIONS, F32)[:, None, None] * LOG2_E
    table = jnp.zeros((ATT_HEADS // 2, SUBLANES, 2 * ATT_BLOCK), F32)
    table = table.at[:, :2 * N_GROUPS, :].set(
        jnp.broadcast_to(slopes.transpose(1, 0, 2).reshape(ATT_HEADS // 2, 2 * N_GROUPS, 1),
                         (ATT_HEADS // 2, 2 * N_GROUPS, 2 * ATT_BLOCK)))
    return table


def kernel(x, ffn1_norm, ffn1_w_gate_up, ffn1_w_down, mix_norm, w_in, hg_lower_bounds, hg_out_norm,
           w_branch_hg, w_branch_att, w_out, ffn2_norm, ffn2_w_gate_up, ffn2_w_down, final_norm):
    b, seq, d = x.shape
    assert d == D_MODEL and seq % (ATT_BLOCK * ATT_DILATIONS[-1]) == 0 and (b * seq) % ROW_TILE == 0
    x2d = x.reshape(b * seq, d)

    h1, w_all = _ffn1(x2d, ffn1_norm[0:1], ffn1_w_gate_up[0].astype(BF16), ffn1_w_down[0].astype(BF16), w_in[0])
    hg_q, hg_i, qkv0, hg_f, hg_og, gates, u_slab = _proj_nat(h1, mix_norm[0:1], w_all)
    proj_perm = _proj_perm(u_slab, w_all, b, seq)
    per_seq = lambda a: a.reshape(b, seq, -1)
    y_hg, wa, wb, wo = _hgrn2(per_seq(hg_q), per_seq(hg_i), per_seq(hg_f), per_seq(hg_og),
                              hg_lower_bounds.astype(F32), hg_out_norm[0:1],
                              [w_branch_hg[0], w_branch_att[0], w_out[0]])
    y_att, wgu2, wd2 = _att(per_seq(qkv0), proj_perm, _alibi_table(), [ffn2_w_gate_up[0], ffn2_w_down[0]])
    out = _tail(h1, y_hg.reshape(b * seq, -1), y_att.reshape(b * seq, -1), gates, wa, wb, wo,
                ffn2_norm[0:1], wgu2, wd2, final_norm.reshape(1, d))
    return out.reshape(b, seq, d)
```

```python
import functools

import jax
import jax.numpy as jnp
from jax import lax
from jax.experimental import pallas as pl
from jax.experimental.pallas import tpu as pltpu

F32 = jnp.float32
BF16 = jnp.bfloat16

D_MODEL = 1024
D_FF = 2816
HG_HEADS = 4
HG_DIM = 128
HG_WIDTH = HG_HEADS * HG_DIM
ATT_DILATIONS = (1, 4, 16)
ATT_BLOCK = 128
ATT_HEADS = 8
ATT_HEAD_DIM = 64
ATT_WIDTH = ATT_HEADS * ATT_HEAD_DIM
N_GROUPS = 3
ALIBI_MAX = 8.0
EPS = 1e-6
NEG_INF = -1e30
LOG2_E = 1.4426950408889634
Q_SCALE = ATT_HEAD_DIM ** -0.5 * LOG2_E

LANES = 128
SUBLANES = 8
MXU_DIM = 256
VMEM_LIMIT = 56 * 1024 * 1024

ROW_TILE = 512
COL_BLOCK = 512
NAT_BF16_COLS = 2 * HG_WIDTH + 3 * ATT_WIDTH
NAT_F32_COLS = 2 * HG_WIDTH + 2 * D_MODEL
HG_CHUNK = 128


def _sigmoid(x):
    return 1.0 / (1.0 + jnp.exp(-x))


def _rmsnorm(x, gain):
    ms = jnp.mean(x * x, axis=-1, keepdims=True)
    return x * lax.rsqrt(ms + EPS) * gain


def _dot(a, b):
    return jnp.dot(a, b, preferred_element_type=F32)


def _dot_nt(a, b):
    return lax.dot_general(a, b, (((1,), (1,)), ((), ())), preferred_element_type=F32)


def _dot_tn(a, b):
    return lax.dot_general(a, b, (((0,), (0,)), ((), ())), preferred_element_type=F32)


def _swiglu(xn_bf16, wgu_ref, wd_ref):
    split = (D_FF // (2 * MXU_DIM) + 1) * MXU_DIM
    y = None
    for lo, hi in ((0, split), (split, D_FF)):
        a = _dot(xn_bf16, wgu_ref[:, lo:hi])
        b = _dot(xn_bf16, wgu_ref[:, D_FF + lo:D_FF + hi])
        act = (a * _sigmoid(a) * b).astype(BF16)
        part = _dot(act, wd_ref[lo:hi, :])
        y = part if y is None else y + part
    return y


W_IN_SRC_OF_DST = (0, 2, 4, 5, 6, 1, 3, 13, 14, 15, 16, 7, 8, 9, 10, 11, 12)
W_IN_Q_BLOCKS = (4, 7, 10)
N_NAT_BLOCKS = (NAT_BF16_COLS + NAT_F32_COLS) // COL_BLOCK
N_PERM_BLOCKS = len(W_IN_SRC_OF_DST) - N_NAT_BLOCKS


FFN1_LOAD_STEPS = 8


def _ffn1_kernel(x_ref, g_ref, wgu_f32_ref, wd_f32_ref, win_ref, o_ref, wall_ref, wgu_ref, wd_ref):
    step = pl.program_id(0)

    @pl.when(step < FFN1_LOAD_STEPS)
    def _load():
        for src, dst in ((wgu_f32_ref, wgu_ref), (wd_f32_ref, wd_ref)):
            n = src.shape[0]
            dst[pl.ds(pl.multiple_of(step * n, n), n), :] = src[...].astype(BF16)

    @pl.when(step >= FFN1_LOAD_STEPS)
    def _compute():
        for half in range(2):
            rows = slice(half * (ROW_TILE // 2), (half + 1) * (ROW_TILE // 2))
            x = x_ref[rows, :]
            xn = _rmsnorm(x, g_ref[...]).astype(BF16)
            o_ref[rows, :] = x + 0.5 * _swiglu(xn, wgu_ref, wd_ref)

        w = win_ref[...]
        wall_ref[...] = jnp.concatenate(
            [w[:, s * COL_BLOCK:(s + 1) * COL_BLOCK] * (Q_SCALE if s in W_IN_Q_BLOCKS else 1.0)
             for s in W_IN_SRC_OF_DST], axis=1).astype(BF16)


def _resident(shape):
    return pl.BlockSpec(shape, lambda *_: (0,) * len(shape), pipeline_mode=pl.Buffered(1))


BF16_ROWS = 2 * SUBLANES


def _side_cast_specs(weights, grid, first_step=0):
    n_steps = -first_step
    total = 1
    for g in grid:
        total *= g
    n_steps += total
    in_specs, out_specs, out_shape = [], [], []
    for w in weights:
        rows, cols = w.shape
        per_step = BF16_ROWS
        while rows % per_step or per_step * n_steps < rows:
            per_step += BF16_ROWS
        last = rows // per_step - 1

        def index(*ids, last=last):
            step = ids[0]
            for g, i in zip(grid[1:], ids[1:]):
                step = step * g + i
            return (jnp.clip(step - first_step, 0, last), 0)

        in_specs.append(pl.BlockSpec((per_step, cols), index))
        out_specs.append(pl.BlockSpec((per_step, cols), index))
        out_shape.append(jax.ShapeDtypeStruct((rows, cols), BF16))
    return in_specs, out_specs, out_shape


def _side_cast(in_refs, out_refs):
    for i_ref, o_ref in zip(in_refs, out_refs):
        o_ref[...] = i_ref[...].astype(BF16)


def _ffn1(x2d, gain, wgu, wd, w_in):
    t = x2d.shape[0]
    load = FFN1_LOAD_STEPS
    grid = (load + t // ROW_TILE,)
    side_in, side_out, side_shape = _side_cast_specs([w_in], grid, first_step=load)
    tile = lambda i: (jnp.maximum(i - load, 0), 0)
    piece = lambda i: (jnp.minimum(i, load - 1), 0)
    assert wgu.shape[0] % (load * BF16_ROWS) == 0 and wd.shape[0] % (load * BF16_ROWS) == 0
    return pl.pallas_call(
        _ffn1_kernel,
        grid=grid,
        in_specs=[
            pl.BlockSpec((ROW_TILE, D_MODEL), tile),
            _resident((1, D_MODEL)),
            pl.BlockSpec((wgu.shape[0] // load, wgu.shape[1]), piece),
            pl.BlockSpec((wd.shape[0] // load, wd.shape[1]), piece),
        ] + side_in,
        out_specs=[pl.BlockSpec((ROW_TILE, D_MODEL), tile)] + side_out,
        out_shape=[jax.ShapeDtypeStruct((t, D_MODEL), F32)] + side_shape,
        scratch_shapes=[pltpu.VMEM(wgu.shape, BF16), pltpu.VMEM(wd.shape, BF16)],
        compiler_params=pltpu.CompilerParams(
            dimension_semantics=("arbitrary",), vmem_limit_bytes=VMEM_LIMIT),
        name="ffn1",
    )(x2d, gain, wgu, wd, w_in)


NAT_OUTPUTS = ((1, BF16), (1, BF16), (3, BF16), (1, F32), (1, F32), (4, F32))


def _proj_nat_kernel(h_ref, g_ref, w_ref, *refs):
    out_refs, slab_ref = refs[:-1], refs[-1]
    for half in range(2):
        rows = slice(half * (ROW_TILE // 2), (half + 1) * (ROW_TILE // 2))
        u = _rmsnorm(h_ref[rows, :], g_ref[...])
        for s in range(D_MODEL // LANES):
            slab_ref[s, rows, :] = u[:, s * LANES:(s + 1) * LANES]
        ub = u.astype(BF16)
        c = 0
        for (n_blocks, dtype), o_ref in zip(NAT_OUTPUTS, out_refs):
            for j in range(n_blocks):
                o_ref[rows, j * COL_BLOCK:(j + 1) * COL_BLOCK] = _dot(
                    ub, w_ref[:, c * COL_BLOCK:(c + 1) * COL_BLOCK]).astype(dtype)
                c += 1


def _proj_nat(h1, gain, w_nat):
    t = h1.shape[0]
    n_slabs = D_MODEL // LANES
    return pl.pallas_call(
        _proj_nat_kernel,
        grid=(t // ROW_TILE,),
        in_specs=[
            pl.BlockSpec((ROW_TILE, D_MODEL), lambda i: (i, 0)),
            _resident((1, D_MODEL)),
            _resident((D_MODEL, NAT_BF16_COLS + NAT_F32_COLS)),
        ],
        out_specs=[pl.BlockSpec((ROW_TILE, n * COL_BLOCK), lambda i: (i, 0)) for n, _ in NAT_OUTPUTS]
        + [pl.BlockSpec((n_slabs, ROW_TILE, LANES), lambda i: (0, i, 0))],
        out_shape=[jax.ShapeDtypeStruct((t, n * COL_BLOCK), dtype) for n, dtype in NAT_OUTPUTS]
        + [jax.ShapeDtypeStruct((n_slabs, t, LANES), F32)],
        compiler_params=pltpu.CompilerParams(
            dimension_semantics=("parallel",), vmem_limit_bytes=VMEM_LIMIT),
        name="proj_nat",
    )(h1, gain, w_nat)


def _proj_perm_kernel(slab_ref, *refs, seq):
    w_refs, o_ref = refs[:-1], refs[-1]
    t = pl.program_id(1)
    n_slabs = D_MODEL // LANES
    rows_out = o_ref.shape[1]

    def gather(g):
        d = ATT_DILATIONS[g]
        run = seq // d
        per_tile = rows_out // run if run < rows_out else 1
        pieces = []
        for c in range(per_tile):
            if run >= rows_out:
                r = (t * rows_out) // run
                start = r + d * ((t * rows_out) % run)
                n_rows = rows_out
            else:
                start = t * per_tile + c
                n_rows = run
            pieces.append(jnp.concatenate(
                [slab_ref[s, pl.ds(start, n_rows, stride=d), :].astype(BF16) for s in range(n_slabs)], axis=1))
        return pieces[0] if len(pieces) == 1 else jnp.concatenate(pieces, axis=0)

    lhs = {g: gather(g) for g in (1, 2)}
    per_group = 3 * ATT_WIDTH // COL_BLOCK
    for g in (1, 2):
        for c in range(per_group):
            blk = (g - 1) * per_group + c
            o_ref[0, :, blk * COL_BLOCK:(blk + 1) * COL_BLOCK] = _dot(lhs[g], w_refs[blk][...]).astype(BF16)


def _proj_perm(u_slab, w_all, b, seq):
    n_slabs = D_MODEL // LANES
    n_cols = N_PERM_BLOCKS * COL_BLOCK
    return pl.pallas_call(
        functools.partial(_proj_perm_kernel, seq=seq),
        grid=(b, seq // ROW_TILE),
        in_specs=[pl.BlockSpec((n_slabs, seq, LANES), lambda i, j: (0, i, 0))]
        + [pl.BlockSpec((D_MODEL, COL_BLOCK), lambda i, j, c=c: (0, N_NAT_BLOCKS + c), pipeline_mode=pl.Buffered(1))
           for c in range(N_PERM_BLOCKS)],
        out_specs=pl.BlockSpec((1, ROW_TILE, n_cols), lambda i, j: (i, j, 0)),
        out_shape=jax.ShapeDtypeStruct((b, seq, n_cols), BF16),
        compiler_params=pltpu.CompilerParams(
            dimension_semantics=("parallel", "arbitrary"), vmem_limit_bytes=VMEM_LIMIT),
        name="proj_perm",
    )(u_slab, *([w_all] * N_PERM_BLOCKS))


def _bcast_rows(x, block, row):
    n = x.shape[0] // block
    parts = [jnp.broadcast_to(x[i * block + row:i * block + row + 1, :], (block, x.shape[1]))
             for i in range(n)]
    return parts[0] if n == 1 else jnp.concatenate(parts, axis=0)


def _hgrn2_kernel(q_ref, i_ref, f_ref, og_ref, lbp_ref, gain_ref, *refs):
    n_side = (len(refs) - 2) // 2
    o_ref, st_ref = refs[n_side], refs[-1]
    _side_cast(refs[:n_side], refs[n_side + 1:-1])
    c_len = HG_CHUNK

    @pl.when(pl.program_id(1) == 0)
    def _():
        st_ref[...] = jnp.zeros_like(st_ref)

    lbp = lbp_ref[...]
    e = jnp.exp(lbp - jnp.max(lbp, axis=0, keepdims=True))
    lb_all = e[0:1, :] / jnp.sum(e, axis=0, keepdims=True)
    gain_all = gain_ref[...]

    t_idx = lax.broadcasted_iota(jnp.int32, (c_len, c_len), 0)
    s_idx = lax.broadcasted_iota(jnp.int32, (c_len, c_len), 1)
    halves = [1 << i for i in range(c_len.bit_length() - 1)]
    level_masks = [((t_idx >> b.bit_length()) == (s_idx >> b.bit_length()))
                   & ((t_idx & b) != 0) & ((s_idx & b) == 0) for b in halves]
    diag_mask = t_idx == s_idx
    ones_rhs = jnp.ones((HG_DIM, c_len), BF16)
    sub = lax.broadcasted_iota(jnp.int32, (c_len, HG_WIDTH), 0) & (SUBLANES - 1)
    pair = sub & 6

    def chunk(c):
        rows = slice(c * c_len, (c + 1) * c_len)
        q = q_ref[0, rows, :].astype(F32)
        v = i_ref[0, rows, :]
        f = lb_all + (1.0 - lb_all) * _sigmoid(f_ref[0, rows, :])
        k = 1.0 - f

        operands = [(q * f, k)]
        odd = (sub & 1) == 1
        pq = f * jnp.where(odd, pltpu.roll(f, 1, axis=0), 1.0)
        sk = jnp.where(odd, 1.0, pltpu.roll(f, c_len - 1, axis=0))
        operands.append((q * pq, k * sk))
        r1, r3, r5, r7 = (_bcast_rows(pq, SUBLANES, r) for r in (1, 3, 5, 7))
        sk = sk * jnp.where(pair == 0, r3, jnp.where(pair == 4, r7, 1.0))
        pq = pq * jnp.where(pair == 2, r1, jnp.where(pair == 6, r5, 1.0))
        operands.append((q * pq, k * sk))
        r3, r7 = _bcast_rows(pq, SUBLANES, 3), _bcast_rows(pq, SUBLANES, 7)
        sk = sk * jnp.where(sub < 4, r7, 1.0)
        pq = pq * jnp.where(sub >= 4, r3, 1.0)
        b = SUBLANES
        while b < c_len:
            n = c_len // b
            tot = _bcast_rows(pq, b, b - 1)
            zero = jnp.zeros((b, HG_WIDTH), F32)

            def blk(x, i, b=b):
                return x[i * b:(i + 1) * b]

            operands.append((
                jnp.concatenate([zero if i % 2 == 0 else blk(q, i) * blk(pq, i) for i in range(n)], axis=0),
                jnp.concatenate([blk(k, i) * blk(sk, i) if i % 2 == 0 else zero for i in range(n)], axis=0)))
            pq, sk = (
                jnp.concatenate([blk(pq, i) if i % 2 == 0 else blk(pq, i) * blk(tot, i - 1)
                                 for i in range(n)], axis=0),
                jnp.concatenate([blk(sk, i) * blk(tot, i + 1) if i % 2 == 0 else blk(sk, i)
                                 for i in range(n)], axis=0))
            b *= 2
        operands = [(a.astype(BF16), kk.astype(BF16)) for a, kk in operands]
        qk = (q * k).astype(BF16)
        qg = (q * pq).astype(BF16)
        kg = (k * sk).astype(BF16)
        chunk_decay = pq[c_len - 1:c_len, :]

        outs = []
        for h in range(HG_HEADS):
            cols = slice(h * HG_DIM, (h + 1) * HG_DIM)
            a = jnp.where(diag_mask, _dot(qk[:, cols], ones_rhs), 0.0)
            for (ql, kl), mask in zip(operands, level_masks):
                a = jnp.where(mask, _dot_nt(ql[:, cols], kl[:, cols]), a)
            st = st_ref[h]
            o = _dot(a.astype(BF16), v[:, cols]) + _dot_nt(qg[:, cols], st.astype(BF16))
            st_ref[h] = st * chunk_decay[:, cols] + _dot_tn(v[:, cols], kg[:, cols])
            outs.append(o * lax.rsqrt(jnp.mean(o * o, axis=-1, keepdims=True) + EPS))
        og = og_ref[0, rows, :]
        o_ref[0, rows, :] = (jnp.concatenate(outs, axis=1) * gain_all * (og * _sigmoid(og))).astype(BF16)

    for c in range(ROW_TILE // c_len):
        chunk(c)


def _hgrn2(hg_q, hg_i, hg_f, hg_og, lower_bounds, out_gain, side_weights):
    b, seq, _ = hg_q.shape
    blk = (1, ROW_TILE, HG_WIDTH)
    grid = (b, seq // ROW_TILE)
    side_in, side_out, side_shape = _side_cast_specs(side_weights, grid)
    return pl.pallas_call(
        _hgrn2_kernel,
        grid=grid,
        in_specs=[
            pl.BlockSpec(blk, lambda i, j: (i, j, 0)),
            pl.BlockSpec(blk, lambda i, j: (i, j, 0)),
            pl.BlockSpec(blk, lambda i, j: (i, j, 0)),
            pl.BlockSpec(blk, lambda i, j: (i, j, 0)),
            _resident(lower_bounds.shape),
            _resident((1, HG_WIDTH)),
        ] + side_in,
        out_specs=[pl.BlockSpec(blk, lambda i, j: (i, j, 0))] + side_out,
        out_shape=[jax.ShapeDtypeStruct((b, seq, HG_WIDTH), BF16)] + side_shape,
        scratch_shapes=[pltpu.VMEM((HG_HEADS, HG_DIM, HG_DIM), F32)],
        compiler_params=pltpu.CompilerParams(
            dimension_semantics=("arbitrary", "arbitrary"), vmem_limit_bytes=VMEM_LIMIT),
        name="hgrn2",
    )(hg_q, hg_i, hg_f, hg_og, lower_bounds, out_gain, *side_weights)


def _att_kernel(slope_ref, *refs, seq, n_side):
    qkv_refs = refs[:3 * N_GROUPS]
    y_ref = refs[3 * N_GROUPS + n_side]
    _side_cast(refs[3 * N_GROUPS:3 * N_GROUPS + n_side],
               refs[3 * N_GROUPS + n_side + 1:3 * N_GROUPS + 2 * n_side + 1])
    scr = refs[3 * N_GROUPS + 2 * n_side + 1:]
    o_scr, l_scr = scr[0:N_GROUPS], scr[N_GROUPS:2 * N_GROUPS]
    y_scr, bias_scr = scr[2 * N_GROUPS], scr[2 * N_GROUPS + 1]
    blk = ATT_BLOCK
    n_blocks = seq // blk
    merge_d = ATT_DILATIONS[1]
    assert all(d % merge_d == 0 for d in ATT_DILATIONS[1:])

    qi = lax.broadcasted_iota(jnp.int32, (blk, 2 * blk), 0)
    kj = lax.broadcasted_iota(jnp.int32, (blk, 2 * blk), 1)
    dist = qi + blk - kj
    in_window = (dist >= 0) & (dist <= blk)
    dist_f = dist.astype(F32)
    for gh in range(2 * N_GROUPS):
        alibi = -slope_ref[0, gh:gh + 1, :] * dist_f
        bias_scr[gh] = jnp.where(in_window, alibi, NEG_INF)

    lane = lax.broadcasted_iota(jnp.int32, (blk, LANES), 1)
    first_head = lane < ATT_HEAD_DIM
    ones2 = jnp.ones((2 * blk, LANES), BF16)

    for g in range(N_GROUPS):
        d = ATT_DILATIONS[g]
        per_class = n_blocks // d
        q_ref, k_ref, v_ref = qkv_refs[3 * g:3 * g + 3]

        for idx in range(n_blocks):
            n, r = idx % per_class, idx // per_class
            rows = slice(idx * blk, (idx + 1) * blk)
            krows = rows if n == 0 else slice((idx - 1) * blk, (idx + 1) * blk)
            q = q_ref[0, rows, :]
            k = k_ref[0, krows, :]
            v_aug = jnp.concatenate([v_ref[0, krows, :], ones2[:k.shape[0]]], axis=1)
            pvs, ms = [], []
            for hh in range(2):
                qm = jnp.where(first_head if hh == 0 else ~first_head, q, jnp.zeros_like(q))
                if n == 0:
                    s = _dot_nt(qm, k) + bias_scr[2 * g + hh, :, blk:]
                    m = jnp.max(s, axis=-1, keepdims=True)
                else:
                    s = _dot_nt(qm, k) + bias_scr[2 * g + hh]
                    m = jnp.max(jnp.maximum(s[:, :blk], s[:, blk:]), axis=-1, keepdims=True)
                pvs.append(_dot(jnp.exp2(s - m).astype(BF16), v_aug))
                ms.append(jnp.broadcast_to(m, (blk, LANES)))
            o = jnp.where(first_head, pvs[0][:, :LANES], pvs[1][:, :LANES])
            den = jnp.where(first_head, pvs[0][:, LANES:], pvs[1][:, LANES:])
            mx = jnp.where(first_head, ms[0], ms[1])
            if d == 1:
                dst = rows
            else:
                dst = pl.ds((r % merge_d) * (seq // merge_d) + n * blk * (d // merge_d) + r // merge_d,
                            blk, stride=d // merge_d)
            o_scr[g][dst, :] = o / den
            l_scr[g][dst, :] = mx + jnp.log2(den)

    rc = 256
    for c4 in range(merge_d):
        for j0 in range(0, seq // merge_d, rc):
            rows = slice(c4 * (seq // merge_d) + j0, c4 * (seq // merge_d) + j0 + rc)
            tokens = pl.ds(c4 + merge_d * j0, rc, stride=merge_d)
            ls = [l_scr[0][tokens, :]] + [l_scr[g][rows, :] for g in range(1, N_GROUPS)]
            os_ = [o_scr[0][tokens, :]] + [o_scr[g][rows, :] for g in range(1, N_GROUPS)]
            m = jnp.maximum(jnp.maximum(ls[0], ls[1]), ls[2])
            ws = [jnp.exp2(lg - m) for lg in ls]
            num = ws[0] * os_[0] + ws[1] * os_[1] + ws[2] * os_[2]
            y_scr[tokens, :] = num / (ws[0] + ws[1] + ws[2])
    for c in range(seq // rc):
        rows = slice(c * rc, (c + 1) * rc)
        y_ref[0, rows, :] = y_scr[rows, :].astype(BF16)


def _att(proj_nat, proj_perm, slopes, side_weights):
    b, seq, _ = proj_nat.shape
    pairs = ATT_WIDTH // LANES
    per_tensor = ATT_WIDTH // LANES
    grid = (b, pairs)
    side_in, side_out, side_shape = _side_cast_specs(side_weights, grid)

    def qkv_spec(g, t):
        off = t * per_tensor if g == 0 else (3 * (g - 1) + t) * per_tensor
        return pl.BlockSpec((1, seq, LANES), lambda i, j, off=off: (i, 0, off + j))

    return pl.pallas_call(
        functools.partial(_att_kernel, seq=seq, n_side=len(side_weights)),
        grid=grid,
        in_specs=[pl.BlockSpec((1, SUBLANES, 2 * ATT_BLOCK), lambda i, j: (j, 0, 0))]
        + [qkv_spec(g, t) for g in range(N_GROUPS) for t in range(3)] + side_in,
        out_specs=[pl.BlockSpec((1, seq, LANES), lambda i, j: (i, 0, j))] + side_out,
        out_shape=[jax.ShapeDtypeStruct((b, seq, ATT_WIDTH), BF16)] + side_shape,
        scratch_shapes=[pltpu.VMEM((seq, LANES), F32) for _ in range(2 * N_GROUPS + 1)]
        + [pltpu.VMEM((2 * N_GROUPS, ATT_BLOCK, 2 * ATT_BLOCK), F32)],
        compiler_params=pltpu.CompilerParams(
            dimension_semantics=("arbitrary", "arbitrary"), vmem_limit_bytes=VMEM_LIMIT),
        name="att",
    )(slopes, *([proj_nat] * 3 + [proj_perm] * (3 * (N_GROUPS - 1))), *side_weights)


def _tail_kernel(h1_ref, yhg_ref, yatt_ref, ghg_ref, gatt_ref, wa_ref, wb_ref, wo_ref,
                 g2_ref, wgu_ref, wd_ref, gf_ref, o_ref):
    merged = (_sigmoid(ghg_ref[...]) * _dot(yhg_ref[...], wa_ref[...])
              + _sigmoid(gatt_ref[...]) * _dot(yatt_ref[...], wb_ref[...]))
    h2 = h1_ref[...] + _dot(merged.astype(BF16), wo_ref[...])
    xn = _rmsnorm(h2, g2_ref[...]).astype(BF16)
    h3 = h2 + 0.5 * _swiglu(xn, wgu_ref, wd_ref)
    o_ref[...] = _rmsnorm(h3, gf_ref[...])


def _tail(h1, y_hg, y_att, gates, wa, wb, wo, g2, wgu, wd, gf):
    t = h1.shape[0]
    row = lambda width, col=0: pl.BlockSpec((ROW_TILE, width), lambda i, col=col: (i, col))
    return pl.pallas_call(
        _tail_kernel,
        grid=(t // ROW_TILE,),
        in_specs=[
            row(D_MODEL), row(HG_WIDTH), row(ATT_WIDTH),
            row(D_MODEL, 0), row(D_MODEL, 1),
            _resident((HG_WIDTH, D_MODEL)), _resident((ATT_WIDTH, D_MODEL)), _resident((D_MODEL, D_MODEL)),
            _resident((1, D_MODEL)), _resident((D_MODEL, 2 * D_FF)), _resident((D_FF, D_MODEL)),
            _resident((1, D_MODEL)),
        ],
        out_specs=row(D_MODEL),
        out_shape=jax.ShapeDtypeStruct((t, D_MODEL), F32),
        compiler_params=pltpu.CompilerParams(
            dimension_semantics=("parallel",), vmem_limit_bytes=VMEM_LIMIT),
        name="tail",
    )(h1, y_hg, y_att, gates, gates, wa, wb, wo, g2, wgu, wd, gf)


def _alibi_table():
    n_heads = N_GROUPS * ATT_HEADS
    slopes = jnp.exp2(-ALIBI_MAX * jnp.arange(1, n_heads + 1, dtype=F32) / n_heads)
    slopes = slopes.reshape(N_GROUPS, ATT_HEADS // 2, 2) * jnp.asarray(ATT_DILATIONS, F32)[:, None, None] * LOG2_E
    table = jnp.zeros((ATT_HEADS // 2, SUBLANES, 2 * ATT_BLOCK), F32)
    table = table.at[:, :2 * N_GROUPS, :].set(
        jnp.broadcast_to(slopes.transpose(1, 0, 2).reshape(ATT_HEADS // 2, 2 * N_GROUPS, 1),
                         (ATT_HEADS // 2, 2 * N_GROUPS, 2 * ATT_BLOCK)))
    return table


def kernel(x, ffn1_norm, ffn1_w_gate_up, ffn1_w_down, mix_norm, w_in, hg_lower_bounds, hg_out_norm,
           w_branch_hg, w_branch_att, w_out, ffn2_norm, ffn2_w_gate_up, ffn2_w_down, final_norm):
    b, seq, d = x.shape
    assert d == D_MODEL and seq % (ATT_BLOCK * ATT_DILATIONS[-1]) == 0 and (b * seq) % ROW_TILE == 0
    x2d = x.reshape(b * seq, d)

    h1, w_all = _ffn1(x2d, ffn1_norm[0:1], ffn1_w_gate_up[0], ffn1_w_down[0], w_in[0])
    hg_q, hg_i, qkv0, hg_f, hg_og, gates, u_slab = _proj_nat(h1, mix_norm[0:1], w_all)
    proj_perm = _proj_perm(u_slab, w_all, b, seq)
    per_seq = lambda a: a.reshape(b, seq, -1)
    y_hg, wa, wb, wo = _hgrn2(per_seq(hg_q), per_seq(hg_i), per_seq(hg_f), per_seq(hg_og),
                              hg_lower_bounds.astype(F32), hg_out_norm[0:1],
                              [w_branch_hg[0], w_branch_att[0], w_out[0]])
    y_att, wgu2, wd2 = _att(per_seq(qkv0), proj_perm, _alibi_table(), [ffn2_w_gate_up[0], ffn2_w_down[0]])
    out = _tail(h1, y_hg.reshape(b * seq, -1), y_att.reshape(b * seq, -1), gates, wa, wb, wo,
                ffn2_norm[0:1], wgu2, wd2, final_norm.reshape(1, d))
    return out.reshape(b, seq, d)
```

```python
import functools

import jax
import jax.numpy as jnp
from jax import lax
from jax.experimental import pallas as pl
from jax.experimental.pallas import tpu as pltpu

F32 = jnp.float32
BF16 = jnp.bfloat16

D_MODEL = 1024
D_FF = 2816
HG_HEADS = 4
HG_DIM = 128
HG_WIDTH = HG_HEADS * HG_DIM
ATT_DILATIONS = (1, 4, 16)
ATT_BLOCK = 128
ATT_HEADS = 8
ATT_HEAD_DIM = 64
ATT_WIDTH = ATT_HEADS * ATT_HEAD_DIM
N_GROUPS = 3
ALIBI_MAX = 8.0
EPS = 1e-6
NEG_INF = -1e30
LOG2_E = 1.4426950408889634
Q_SCALE = ATT_HEAD_DIM ** -0.5 * LOG2_E

LANES = 128
SUBLANES = 8
MXU_DIM = 256
VMEM_LIMIT = 56 * 1024 * 1024

ROW_TILE = 512
COL_BLOCK = 512
NAT_BF16_COLS = 2 * HG_WIDTH + 3 * ATT_WIDTH
NAT_F32_COLS = 2 * HG_WIDTH + 2 * D_MODEL
HG_CHUNK = 128
HG_TILE = 1024


def _sigmoid(x):
    return 1.0 / (1.0 + jnp.exp(-x))


def _rmsnorm(x, gain):
    ms = jnp.mean(x * x, axis=-1, keepdims=True)
    return x * lax.rsqrt(ms + EPS) * gain


def _dot(a, b):
    return jnp.dot(a, b, preferred_element_type=F32)


def _dot_nt(a, b):
    return lax.dot_general(a, b, (((1,), (1,)), ((), ())), preferred_element_type=F32)


def _dot_tn(a, b):
    return lax.dot_general(a, b, (((0,), (0,)), ((), ())), preferred_element_type=F32)


def _swiglu(xn_bf16, wgu_ref, wd_ref):
    split = (D_FF // (2 * MXU_DIM) + 1) * MXU_DIM
    y = None
    for lo, hi in ((0, split), (split, D_FF)):
        a = _dot(xn_bf16, wgu_ref[:, lo:hi])
        b = _dot(xn_bf16, wgu_ref[:, D_FF + lo:D_FF + hi])
        act = (a * _sigmoid(a) * b).astype(BF16)
        part = _dot(act, wd_ref[lo:hi, :])
        y = part if y is None else y + part
    return y


W_IN_SRC_OF_DST = (0, 2, 4, 5, 6, 1, 3, 13, 14, 15, 16, 7, 8, 9, 10, 11, 12)
W_IN_Q_BLOCKS = (4, 7, 10)
N_NAT_BLOCKS = (NAT_BF16_COLS + NAT_F32_COLS) // COL_BLOCK
N_PERM_BLOCKS = len(W_IN_SRC_OF_DST) - N_NAT_BLOCKS


FFN1_LOAD_STEPS = 8


def _ffn1_kernel(x_ref, g_ref, wgu_f32_ref, wd_f32_ref, win_ref, o_ref, wall_ref, wgu_ref, wd_ref):
    step = pl.program_id(0)

    @pl.when(step < FFN1_LOAD_STEPS)
    def _load():
        for src, dst in ((wgu_f32_ref, wgu_ref), (wd_f32_ref, wd_ref)):
            n = src.shape[0]
            dst[pl.ds(pl.multiple_of(step * n, n), n), :] = src[...].astype(BF16)

    @pl.when(step >= FFN1_LOAD_STEPS)
    def _compute():
        for half in range(2):
            rows = slice(half * (ROW_TILE // 2), (half + 1) * (ROW_TILE // 2))
            x = x_ref[rows, :]
            xn = _rmsnorm(x, g_ref[...]).astype(BF16)
            o_ref[rows, :] = x + 0.5 * _swiglu(xn, wgu_ref, wd_ref)

        w = win_ref[...]
        wall_ref[...] = jnp.concatenate(
            [w[:, s * COL_BLOCK:(s + 1) * COL_BLOCK] * (Q_SCALE if s in W_IN_Q_BLOCKS else 1.0)
             for s in W_IN_SRC_OF_DST], axis=1).astype(BF16)


def _resident(shape):
    return pl.BlockSpec(shape, lambda *_: (0,) * len(shape), pipeline_mode=pl.Buffered(1))


BF16_ROWS = 2 * SUBLANES


def _side_cast_specs(weights, grid, first_step=0):
    n_steps = -first_step
    total = 1
    for g in grid:
        total *= g
    n_steps += total
    in_specs, out_specs, out_shape = [], [], []
    for w in weights:
        rows, cols = w.shape
        per_step = BF16_ROWS
        while rows % per_step or per_step * n_steps < rows:
            per_step += BF16_ROWS
        last = rows // per_step - 1

        def index(*ids, last=last):
            step = ids[0]
            for g, i in zip(grid[1:], ids[1:]):
                step = step * g + i
            return (jnp.clip(step - first_step, 0, last), 0)

        in_specs.append(pl.BlockSpec((per_step, cols), index))
        out_specs.append(pl.BlockSpec((per_step, cols), index))
        out_shape.append(jax.ShapeDtypeStruct((rows, cols), BF16))
    return in_specs, out_specs, out_shape


def _side_cast(in_refs, out_refs):
    for i_ref, o_ref in zip(in_refs, out_refs):
        o_ref[...] = i_ref[...].astype(BF16)


def _ffn1(x2d, gain, wgu, wd, w_in):
    t = x2d.shape[0]
    load = FFN1_LOAD_STEPS
    grid = (load + t // ROW_TILE,)
    side_in, side_out, side_shape = _side_cast_specs([w_in], grid, first_step=load)
    tile = lambda i: (jnp.maximum(i - load, 0), 0)
    piece = lambda i: (jnp.minimum(i, load - 1), 0)
    assert wgu.shape[0] % (load * BF16_ROWS) == 0 and wd.shape[0] % (load * BF16_ROWS) == 0
    return pl.pallas_call(
        _ffn1_kernel,
        grid=grid,
        in_specs=[
            pl.BlockSpec((ROW_TILE, D_MODEL), tile),
            _resident((1, D_MODEL)),
            pl.BlockSpec((wgu.shape[0] // load, wgu.shape[1]), piece),
            pl.BlockSpec((wd.shape[0] // load, wd.shape[1]), piece),
        ] + side_in,
        out_specs=[pl.BlockSpec((ROW_TILE, D_MODEL), tile)] + side_out,
        out_shape=[jax.ShapeDtypeStruct((t, D_MODEL), F32)] + side_shape,
        scratch_shapes=[pltpu.VMEM(wgu.shape, BF16), pltpu.VMEM(wd.shape, BF16)],
        compiler_params=pltpu.CompilerParams(
            dimension_semantics=("arbitrary",), vmem_limit_bytes=VMEM_LIMIT),
        name="ffn1",
    )(x2d, gain, wgu, wd, w_in)


NAT_OUTPUTS = ((1, BF16), (1, BF16), (3, BF16), (1, F32), (1, F32), (4, F32))


def _proj_nat_kernel(h_ref, g_ref, w_ref, *refs):
    out_refs, slab_ref = refs[:-1], refs[-1]
    for half in range(2):
        rows = slice(half * (ROW_TILE // 2), (half + 1) * (ROW_TILE // 2))
        u = _rmsnorm(h_ref[rows, :], g_ref[...])
        for s in range(D_MODEL // LANES):
            slab_ref[s, rows, :] = u[:, s * LANES:(s + 1) * LANES]
        ub = u.astype(BF16)
        c = 0
        for (n_blocks, dtype), o_ref in zip(NAT_OUTPUTS, out_refs):
            for j in range(n_blocks):
                o_ref[rows, j * COL_BLOCK:(j + 1) * COL_BLOCK] = _dot(
                    ub, w_ref[:, c * COL_BLOCK:(c + 1) * COL_BLOCK]).astype(dtype)
                c += 1


def _proj_nat(h1, gain, w_nat):
    t = h1.shape[0]
    n_slabs = D_MODEL // LANES
    return pl.pallas_call(
        _proj_nat_kernel,
        grid=(t // ROW_TILE,),
        in_specs=[
            pl.BlockSpec((ROW_TILE, D_MODEL), lambda i: (i, 0)),
            _resident((1, D_MODEL)),
            _resident((D_MODEL, NAT_BF16_COLS + NAT_F32_COLS)),
        ],
        out_specs=[pl.BlockSpec((ROW_TILE, n * COL_BLOCK), lambda i: (i, 0)) for n, _ in NAT_OUTPUTS]
        + [pl.BlockSpec((n_slabs, ROW_TILE, LANES), lambda i: (0, i, 0))],
        out_shape=[jax.ShapeDtypeStruct((t, n * COL_BLOCK), dtype) for n, dtype in NAT_OUTPUTS]
        + [jax.ShapeDtypeStruct((n_slabs, t, LANES), F32)],
        compiler_params=pltpu.CompilerParams(
            dimension_semantics=("parallel",), vmem_limit_bytes=VMEM_LIMIT),
        name="proj_nat",
    )(h1, gain, w_nat)


def _proj_perm_kernel(slab_ref, *refs, seq):
    w_refs, o_ref = refs[:-1], refs[-1]
    t = pl.program_id(1)
    n_slabs = D_MODEL // LANES
    rows_out = o_ref.shape[1]

    def gather(g):
        d = ATT_DILATIONS[g]
        run = seq // d
        per_tile = rows_out // run if run < rows_out else 1
        pieces = []
        for c in range(per_tile):
            if run >= rows_out:
                r = (t * rows_out) // run
                start = r + d * ((t * rows_out) % run)
                n_rows = rows_out
            else:
                start = t * per_tile + c
                n_rows = run
            pieces.append(jnp.concatenate(
                [slab_ref[s, pl.ds(start, n_rows, stride=d), :].astype(BF16) for s in range(n_slabs)], axis=1))
        return pieces[0] if len(pieces) == 1 else jnp.concatenate(pieces, axis=0)

    lhs = {g: gather(g) for g in (1, 2)}
    per_group = 3 * ATT_WIDTH // COL_BLOCK
    for g in (1, 2):
        for c in range(per_group):
            blk = (g - 1) * per_group + c
            o_ref[0, :, blk * COL_BLOCK:(blk + 1) * COL_BLOCK] = _dot(lhs[g], w_refs[blk][...]).astype(BF16)


def _proj_perm(u_slab, w_all, b, seq):
    n_slabs = D_MODEL // LANES
    n_cols = N_PERM_BLOCKS * COL_BLOCK
    return pl.pallas_call(
        functools.partial(_proj_perm_kernel, seq=seq),
        grid=(b, seq // ROW_TILE),
        in_specs=[pl.BlockSpec((n_slabs, seq, LANES), lambda i, j: (0, i, 0))]
        + [pl.BlockSpec((D_MODEL, COL_BLOCK), lambda i, j, c=c: (0, N_NAT_BLOCKS + c), pipeline_mode=pl.Buffered(1))
           for c in range(N_PERM_BLOCKS)],
        out_specs=pl.BlockSpec((1, ROW_TILE, n_cols), lambda i, j: (i, j, 0)),
        out_shape=jax.ShapeDtypeStruct((b, seq, n_cols), BF16),
        compiler_params=pltpu.CompilerParams(
            dimension_semantics=("parallel", "arbitrary"), vmem_limit_bytes=VMEM_LIMIT),
        name="proj_perm",
    )(u_slab, *([w_all] * N_PERM_BLOCKS))


def _bcast_rows(x, block, row):
    n = x.shape[0] // block
    parts = [jnp.broadcast_to(x[i * block + row:i * block + row + 1, :], (block, x.shape[1]))
             for i in range(n)]
    return parts[0] if n == 1 else jnp.concatenate(parts, axis=0)


def _hgrn2_kernel(q_ref, i_ref, f_ref, og_ref, lbp_ref, gain_ref, *refs):
    n_side = (len(refs) - 2) // 2
    o_ref, st_ref = refs[n_side], refs[-1]
    _side_cast(refs[:n_side], refs[n_side + 1:-1])
    c_len = HG_CHUNK

    @pl.when(pl.program_id(1) == 0)
    def _():
        st_ref[...] = jnp.zeros_like(st_ref)

    lbp = lbp_ref[...]
    e = jnp.exp(lbp - jnp.max(lbp, axis=0, keepdims=True))
    lb_all = e[0:1, :] / jnp.sum(e, axis=0, keepdims=True)
    gain_all = gain_ref[...]

    t_idx = lax.broadcasted_iota(jnp.int32, (c_len, c_len), 0)
    s_idx = lax.broadcasted_iota(jnp.int32, (c_len, c_len), 1)
    halves = [1 << i for i in range(c_len.bit_length() - 1)]
    level_masks = [((t_idx >> b.bit_length()) == (s_idx >> b.bit_length()))
                   & ((t_idx & b) != 0) & ((s_idx & b) == 0) for b in halves]
    diag_mask = t_idx == s_idx
    ones_rhs = jnp.ones((HG_DIM, c_len), BF16)
    sub = lax.broadcasted_iota(jnp.int32, (c_len, HG_WIDTH), 0) & (SUBLANES - 1)
    pair = sub & 6

    def chunk(c):
        rows = slice(c * c_len, (c + 1) * c_len)
        q = q_ref[0, rows, :].astype(F32)
        v = i_ref[0, rows, :]
        f = lb_all + (1.0 - lb_all) * _sigmoid(f_ref[0, rows, :])
        k = 1.0 - f

        operands = [(q * f, k)]
        odd = (sub & 1) == 1
        pq = f * jnp.where(odd, pltpu.roll(f, 1, axis=0), 1.0)
        sk = jnp.where(odd, 1.0, pltpu.roll(f, c_len - 1, axis=0))
        operands.append((q * pq, k * sk))
        r1, r3, r5, r7 = (_bcast_rows(pq, SUBLANES, r) for r in (1, 3, 5, 7))
        sk = sk * jnp.where(pair == 0, r3, jnp.where(pair == 4, r7, 1.0))
        pq = pq * jnp.where(pair == 2, r1, jnp.where(pair == 6, r5, 1.0))
        operands.append((q * pq, k * sk))
        r3, r7 = _bcast_rows(pq, SUBLANES, 3), _bcast_rows(pq, SUBLANES, 7)
        sk = sk * jnp.where(sub < 4, r7, 1.0)
        pq = pq * jnp.where(sub >= 4, r3, 1.0)
        b = SUBLANES
        while b < c_len:
            n = c_len // b
            tot = _bcast_rows(pq, b, b - 1)
            zero = jnp.zeros((b, HG_WIDTH), F32)

            def blk(x, i, b=b):
                return x[i * b:(i + 1) * b]

            operands.append((
                jnp.concatenate([zero if i % 2 == 0 else blk(q, i) * blk(pq, i) for i in range(n)], axis=0),
                jnp.concatenate([blk(k, i) * blk(sk, i) if i % 2 == 0 else zero for i in range(n)], axis=0)))
            pq, sk = (
                jnp.concatenate([blk(pq, i) if i % 2 == 0 else blk(pq, i) * blk(tot, i - 1)
                                 for i in range(n)], axis=0),
                jnp.concatenate([blk(sk, i) * blk(tot, i + 1) if i % 2 == 0 else blk(sk, i)
                                 for i in range(n)], axis=0))
            b *= 2
        operands = [(a.astype(BF16), kk.astype(BF16)) for a, kk in operands]
        qk = (q * k).astype(BF16)
        qg = (q * pq).astype(BF16)
        kg = (k * sk).astype(BF16)
        chunk_decay = pq[c_len - 1:c_len, :]

        outs = []
        for h in range(HG_HEADS):
            cols = slice(h * HG_DIM, (h + 1) * HG_DIM)
            a = jnp.where(diag_mask, _dot(qk[:, cols], ones_rhs), 0.0)
            for (ql, kl), mask in zip(operands, level_masks):
                a = jnp.where(mask, _dot_nt(ql[:, cols], kl[:, cols]), a)
            st = st_ref[h]
            o = _dot(a.astype(BF16), v[:, cols]) + _dot_nt(qg[:, cols], st.astype(BF16))
            st_ref[h] = st * chunk_decay[:, cols] + _dot_tn(v[:, cols], kg[:, cols])
            outs.append(o * lax.rsqrt(jnp.mean(o * o, axis=-1, keepdims=True) + EPS))
        og = og_ref[0, rows, :]
        o_ref[0, rows, :] = (jnp.concatenate(outs, axis=1) * gain_all * (og * _sigmoid(og))).astype(BF16)

    for c in range(HG_TILE // c_len):
        chunk(c)


def _hgrn2(hg_q, hg_i, hg_f, hg_og, lower_bounds, out_gain, side_weights):
    b, seq, _ = hg_q.shape
    blk = (1, HG_TILE, HG_WIDTH)
    grid = (b, seq // HG_TILE)
    side_in, side_out, side_shape = _side_cast_specs(side_weights, grid)
    return pl.pallas_call(
        _hgrn2_kernel,
        grid=grid,
        in_specs=[
            pl.BlockSpec(blk, lambda i, j: (i, j, 0)),
            pl.BlockSpec(blk, lambda i, j: (i, j, 0)),
            pl.BlockSpec(blk, lambda i, j: (i, j, 0)),
            pl.BlockSpec(blk, lambda i, j: (i, j, 0)),
            _resident(lower_bounds.shape),
            _resident((1, HG_WIDTH)),
        ] + side_in,
        out_specs=[pl.BlockSpec(blk, lambda i, j: (i, j, 0))] + side_out,
        out_shape=[jax.ShapeDtypeStruct((b, seq, HG_WIDTH), BF16)] + side_shape,
        scratch_shapes=[pltpu.VMEM((HG_HEADS, HG_DIM, HG_DIM), F32)],
        compiler_params=pltpu.CompilerParams(
            dimension_semantics=("arbitrary", "arbitrary"), vmem_limit_bytes=VMEM_LIMIT),
        name="hgrn2",
    )(hg_q, hg_i, hg_f, hg_og, lower_bounds, out_gain, *side_weights)


def _att_kernel(slope_ref, *refs, seq, n_side):
    qkv_refs = refs[:3 * N_GROUPS]
    y_ref = refs[3 * N_GROUPS + n_side]
    _side_cast(refs[3 * N_GROUPS:3 * N_GROUPS + n_side],
               refs[3 * N_GROUPS + n_side + 1:3 * N_GROUPS + 2 * n_side + 1])
    scr = refs[3 * N_GROUPS + 2 * n_side + 1:]
    o_scr, l_scr = scr[0:N_GROUPS], scr[N_GROUPS:2 * N_GROUPS]
    y_scr, bias_scr = scr[2 * N_GROUPS], scr[2 * N_GROUPS + 1]
    blk = ATT_BLOCK
    n_blocks = seq // blk
    merge_d = ATT_DILATIONS[1]
    assert all(d % merge_d == 0 for d in ATT_DILATIONS[1:])

    qi = lax.broadcasted_iota(jnp.int32, (blk, 2 * blk), 0)
    kj = lax.broadcasted_iota(jnp.int32, (blk, 2 * blk), 1)
    dist = qi + blk - kj
    in_window = (dist >= 0) & (dist <= blk)
    dist_f = dist.astype(F32)
    for gh in range(2 * N_GROUPS):
        alibi = -slope_ref[0, gh:gh + 1, :] * dist_f
        bias_scr[gh] = jnp.where(in_window, alibi, NEG_INF)

    lane = lax.broadcasted_iota(jnp.int32, (blk, LANES), 1)
    first_head = lane < ATT_HEAD_DIM
    ones2 = jnp.ones((2 * blk, LANES), BF16)

    for g in range(N_GROUPS):
        d = ATT_DILATIONS[g]
        per_class = n_blocks // d
        q_ref, k_ref, v_ref = qkv_refs[3 * g:3 * g + 3]

        for idx in range(n_blocks):
            n, r = idx % per_class, idx // per_class
            rows = slice(idx * blk, (idx + 1) * blk)
            krows = rows if n == 0 else slice((idx - 1) * blk, (idx + 1) * blk)
            q = q_ref[0, rows, :]
            k = k_ref[0, krows, :]
            v_aug = jnp.concatenate([v_ref[0, krows, :], ones2[:k.shape[0]]], axis=1)
            pvs, ms = [], []
            for hh in range(2):
                qm = jnp.where(first_head if hh == 0 else ~first_head, q, jnp.zeros_like(q))
                if n == 0:
                    s = _dot_nt(qm, k) + bias_scr[2 * g + hh, :, blk:]
                    m = jnp.max(s, axis=-1, keepdims=True)
                else:
                    s = _dot_nt(qm, k) + bias_scr[2 * g + hh]
                    m = jnp.max(jnp.maximum(s[:, :blk], s[:, blk:]), axis=-1, keepdims=True)
                pvs.append(_dot(jnp.exp2(s - m).astype(BF16), v_aug))
                ms.append(jnp.broadcast_to(m, (blk, LANES)))
            o = jnp.where(first_head, pvs[0][:, :LANES], pvs[1][:, :LANES])
            den = jnp.where(first_head, pvs[0][:, LANES:], pvs[1][:, LANES:])
            mx = jnp.where(first_head, ms[0], ms[1])
            if d == 1:
                dst = rows
            else:
                dst = pl.ds((r % merge_d) * (seq // merge_d) + n * blk * (d // merge_d) + r // merge_d,
                            blk, stride=d // merge_d)
            o_scr[g][dst, :] = o / den
            l_scr[g][dst, :] = mx + jnp.log2(den)

    rc = 256
    for c4 in range(merge_d):
        for j0 in range(0, seq // merge_d, rc):
            rows = slice(c4 * (seq // merge_d) + j0, c4 * (seq // merge_d) + j0 + rc)
            tokens = pl.ds(c4 + merge_d * j0, rc, stride=merge_d)
            ls = [l_scr[0][tokens, :]] + [l_scr[g][rows, :] for g in range(1, N_GROUPS)]
            os_ = [o_scr[0][tokens, :]] + [o_scr[g][rows, :] for g in range(1, N_GROUPS)]
            m = jnp.maximum(jnp.maximum(ls[0], ls[1]), ls[2])
            ws = [jnp.exp2(lg - m) for lg in ls]
            num = ws[0] * os_[0] + ws[1] * os_[1] + ws[2] * os_[2]
            y_scr[tokens, :] = num / (ws[0] + ws[1] + ws[2])
    for c in range(seq // rc):
        rows = slice(c * rc, (c + 1) * rc)
        y_ref[0, rows, :] = y_scr[rows, :].astype(BF16)


def _att(proj_nat, proj_perm, slopes, side_weights):
    b, seq, _ = proj_nat.shape
    pairs = ATT_WIDTH // LANES
    per_tensor = ATT_WIDTH // LANES
    grid = (b, pairs)
    side_in, side_out, side_shape = _side_cast_specs(side_weights, grid)

    def qkv_spec(g, t):
        off = t * per_tensor if g == 0 else (3 * (g - 1) + t) * per_tensor
        return pl.BlockSpec((1, seq, LANES), lambda i, j, off=off: (i, 0, off + j))

    return pl.pallas_call(
        functools.partial(_att_kernel, seq=seq, n_side=len(side_weights)),
        grid=grid,
        in_specs=[pl.BlockSpec((1, SUBLANES, 2 * ATT_BLOCK), lambda i, j: (j, 0, 0))]
        + [qkv_spec(g, t) for g in range(N_GROUPS) for t in range(3)] + side_in,
        out_specs=[pl.BlockSpec((1, seq, LANES), lambda i, j: (i, 0, j))] + side_out,
        out_shape=[jax.ShapeDtypeStruct((b, seq, ATT_WIDTH), BF16)] + side_shape,
        scratch_shapes=[pltpu.VMEM((seq, LANES), F32) for _ in range(2 * N_GROUPS + 1)]
        + [pltpu.VMEM((2 * N_GROUPS, ATT_BLOCK, 2 * ATT_BLOCK), F32)],
        compiler_params=pltpu.CompilerParams(
            dimension_semantics=("arbitrary", "arbitrary"), vmem_limit_bytes=VMEM_LIMIT),
        name="att",
    )(slopes, *([proj_nat] * 3 + [proj_perm] * (3 * (N_GROUPS - 1))), *side_weights)


def _tail_kernel(h1_ref, yhg_ref, yatt_ref, ghg_ref, gatt_ref, wa_ref, wb_ref, wo_ref,
                 g2_ref, wgu_ref, wd_ref, gf_ref, o_ref):
    merged = (_sigmoid(ghg_ref[...]) * _dot(yhg_ref[...], wa_ref[...])
              + _sigmoid(gatt_ref[...]) * _dot(yatt_ref[...], wb_ref[...]))
    h2 = h1_ref[...] + _dot(merged.astype(BF16), wo_ref[...])
    xn = _rmsnorm(h2, g2_ref[...]).astype(BF16)
    h3 = h2 + 0.5 * _swiglu(xn, wgu_ref, wd_ref)
    o_ref[...] = _rmsnorm(h3, gf_ref[...])


def _tail(h1, y_hg, y_att, gates, wa, wb, wo, g2, wgu, wd, gf):
    t = h1.shape[0]
    row = lambda width, col=0: pl.BlockSpec((ROW_TILE, width), lambda i, col=col: (i, col))
    return pl.pallas_call(
        _tail_kernel,
        grid=(t // ROW_TILE,),
        in_specs=[
            row(D_MODEL), row(HG_WIDTH), row(ATT_WIDTH),
            row(D_MODEL, 0), row(D_MODEL, 1),
            _resident((HG_WIDTH, D_MODEL)), _resident((ATT_WIDTH, D_MODEL)), _resident((D_MODEL, D_MODEL)),
            _resident((1, D_MODEL)), _resident((D_MODEL, 2 * D_FF)), _resident((D_FF, D_MODEL)),
            _resident((1, D_MODEL)),
        ],
        out_specs=row(D_MODEL),
        out_shape=jax.ShapeDtypeStruct((t, D_MODEL), F32),
        compiler_params=pltpu.CompilerParams(
            dimension_semantics=("parallel",), vmem_limit_bytes=VMEM_LIMIT),
        name="tail",
    )(h1, y_hg, y_att, gates, gates, wa, wb, wo, g2, wgu, wd, gf)


def _alibi_table():
    n_heads = N_GROUPS * ATT_HEADS
    slopes = jnp.exp2(-ALIBI_MAX * jnp.arange(1, n_heads + 1, dtype=F32) / n_heads)
    slopes = slopes.reshape(N_GROUPS, ATT_HEADS // 2, 2) * jnp.asarray(ATT_DILATIONS, F32)[:, None, None] * LOG2_E
    table = jnp.zeros((ATT_HEADS // 2, SUBLANES, 2 * ATT_BLOCK), F32)
    table = table.at[:, :2 * N_GROUPS, :].set(
        jnp.broadcast_to(slopes.transpose(1, 0, 2).reshape(ATT_HEADS // 2, 2 * N_GROUPS, 1),
                         (ATT_HEADS // 2, 2 * N_GROUPS, 2 * ATT_BLOCK)))
    return table


def kernel(x, ffn1_norm, ffn1_w_gate_up, ffn1_w_down, mix_norm, w_in, hg_lower_bounds, hg_out_norm,
           w_branch_hg, w_branch_att, w_out, ffn2_norm, ffn2_w_gate_up, ffn2_w_down, final_norm):
    b, seq, d = x.shape
    assert d == D_MODEL and seq % (ATT_BLOCK * ATT_DILATIONS[-1]) == 0 and (b * seq) % ROW_TILE == 0
    x2d = x.reshape(b * seq, d)

    h1, w_all = _ffn1(x2d, ffn1_norm[0:1], ffn1_w_gate_up[0], ffn1_w_down[0], w_in[0])
    hg_q, hg_i, qkv0, hg_f, hg_og, gates, u_slab = _proj_nat(h1, mix_norm[0:1], w_all)
    proj_perm = _proj_perm(u_slab, w_all, b, seq)
    per_seq = lambda a: a.reshape(b, seq, -1)
    y_hg, wa, wb, wo = _hgrn2(per_seq(hg_q), per_seq(hg_i), per_seq(hg_f), per_seq(hg_og),
                              hg_lower_bounds.astype(F32), hg_out_norm[0:1],
                              [w_branch_hg[0], w_branch_att[0], w_out[0]])
    y_att, wgu2, wd2 = _att(per_seq(qkv0), proj_perm, _alibi_table(), [ffn2_w_gate_up[0], ffn2_w_down[0]])
    out = _tail(h1, y_hg.reshape(b * seq, -1), y_att.reshape(b * seq, -1), gates, wa, wb, wo,
                ffn2_norm[0:1], wgu2, wd2, final_norm.reshape(1, d))
    return out.reshape(b, seq, d)
```

```python
import functools

import jax
import jax.numpy as jnp
from jax import lax
from jax.experimental import pallas as pl
from jax.experimental.pallas import tpu as pltpu

F32 = jnp.float32
BF16 = jnp.bfloat16

D_MODEL = 1024
D_FF = 2816
HG_HEADS = 4
HG_DIM = 128
HG_WIDTH = HG_HEADS * HG_DIM
ATT_DILATIONS = (1, 4, 16)
ATT_BLOCK = 128
ATT_HEADS = 8
ATT_HEAD_DIM = 64
ATT_WIDTH = ATT_HEADS * ATT_HEAD_DIM
N_GROUPS = 3
ALIBI_MAX = 8.0
EPS = 1e-6
NEG_INF = -1e30
LOG2_E = 1.4426950408889634
Q_SCALE = ATT_HEAD_DIM ** -0.5 * LOG2_E

LANES = 128
SUBLANES = 8
MXU_DIM = 256
VMEM_LIMIT = 56 * 1024 * 1024

ROW_TILE = 512
COL_BLOCK = 512
NAT_BF16_COLS = 2 * HG_WIDTH + 3 * ATT_WIDTH
NAT_F32_COLS = 2 * HG_WIDTH + 2 * D_MODEL
HG_CHUNK = 128
HG_TILE = 1024
PERM_TILE = 1024


def _sigmoid(x):
    return 1.0 / (1.0 + jnp.exp(-x))


def _rmsnorm(x, gain):
    ms = jnp.mean(x * x, axis=-1, keepdims=True)
    return x * lax.rsqrt(ms + EPS) * gain


def _dot(a, b):
    return jnp.dot(a, b, preferred_element_type=F32)


def _dot_nt(a, b):
    return lax.dot_general(a, b, (((1,), (1,)), ((), ())), preferred_element_type=F32)


def _dot_tn(a, b):
    return lax.dot_general(a, b, (((0,), (0,)), ((), ())), preferred_element_type=F32)


def _swiglu(xn_bf16, wgu_ref, wd_ref):
    split = (D_FF // (2 * MXU_DIM) + 1) * MXU_DIM
    y = None
    for lo, hi in ((0, split), (split, D_FF)):
        a = _dot(xn_bf16, wgu_ref[:, lo:hi])
        b = _dot(xn_bf16, wgu_ref[:, D_FF + lo:D_FF + hi])
        act = (a * _sigmoid(a) * b).astype(BF16)
        part = _dot(act, wd_ref[lo:hi, :])
        y = part if y is None else y + part
    return y


W_IN_SRC_OF_DST = (0, 2, 4, 5, 6, 1, 3, 13, 14, 15, 16, 7, 8, 9, 10, 11, 12)
W_IN_Q_BLOCKS = (4, 7, 10)
N_NAT_BLOCKS = (NAT_BF16_COLS + NAT_F32_COLS) // COL_BLOCK
N_PERM_BLOCKS = len(W_IN_SRC_OF_DST) - N_NAT_BLOCKS


FFN1_LOAD_STEPS = 8


def _ffn1_kernel(x_ref, g_ref, wgu_f32_ref, wd_f32_ref, win_ref, o_ref, wall_ref, wgu_ref, wd_ref):
    step = pl.program_id(0)

    @pl.when(step < FFN1_LOAD_STEPS)
    def _load():
        for src, dst in ((wgu_f32_ref, wgu_ref), (wd_f32_ref, wd_ref)):
            n = src.shape[0]
            dst[pl.ds(pl.multiple_of(step * n, n), n), :] = src[...].astype(BF16)

    @pl.when(step >= FFN1_LOAD_STEPS)
    def _compute():
        for half in range(2):
            rows = slice(half * (ROW_TILE // 2), (half + 1) * (ROW_TILE // 2))
            x = x_ref[rows, :]
            xn = _rmsnorm(x, g_ref[...]).astype(BF16)
            o_ref[rows, :] = x + 0.5 * _swiglu(xn, wgu_ref, wd_ref)

        w = win_ref[...]
        wall_ref[...] = jnp.concatenate(
            [w[:, s * COL_BLOCK:(s + 1) * COL_BLOCK] * (Q_SCALE if s in W_IN_Q_BLOCKS else 1.0)
             for s in W_IN_SRC_OF_DST], axis=1).astype(BF16)


def _resident(shape):
    return pl.BlockSpec(shape, lambda *_: (0,) * len(shape), pipeline_mode=pl.Buffered(1))


BF16_ROWS = 2 * SUBLANES


def _side_cast_specs(weights, grid, first_step=0):
    n_steps = -first_step
    total = 1
    for g in grid:
        total *= g
    n_steps += total
    in_specs, out_specs, out_shape = [], [], []
    for w in weights:
        rows, cols = w.shape
        per_step = BF16_ROWS
        while rows % per_step or per_step * n_steps < rows:
            per_step += BF16_ROWS
        last = rows // per_step - 1

        def index(*ids, last=last):
            step = ids[0]
            for g, i in zip(grid[1:], ids[1:]):
                step = step * g + i
            return (jnp.clip(step - first_step, 0, last), 0)

        in_specs.append(pl.BlockSpec((per_step, cols), index))
        out_specs.append(pl.BlockSpec((per_step, cols), index))
        out_shape.append(jax.ShapeDtypeStruct((rows, cols), BF16))
    return in_specs, out_specs, out_shape


def _side_cast(in_refs, out_refs):
    for i_ref, o_ref in zip(in_refs, out_refs):
        o_ref[...] = i_ref[...].astype(BF16)


def _ffn1(x2d, gain, wgu, wd, w_in):
    t = x2d.shape[0]
    load = FFN1_LOAD_STEPS
    grid = (load + t // ROW_TILE,)
    side_in, side_out, side_shape = _side_cast_specs([w_in], grid, first_step=load)
    tile = lambda i: (jnp.maximum(i - load, 0), 0)
    piece = lambda i: (jnp.minimum(i, load - 1), 0)
    assert wgu.shape[0] % (load * BF16_ROWS) == 0 and wd.shape[0] % (load * BF16_ROWS) == 0
    return pl.pallas_call(
        _ffn1_kernel,
        grid=grid,
        in_specs=[
            pl.BlockSpec((ROW_TILE, D_MODEL), tile),
            _resident((1, D_MODEL)),
            pl.BlockSpec((wgu.shape[0] // load, wgu.shape[1]), piece),
            pl.BlockSpec((wd.shape[0] // load, wd.shape[1]), piece),
        ] + side_in,
        out_specs=[pl.BlockSpec((ROW_TILE, D_MODEL), tile)] + side_out,
        out_shape=[jax.ShapeDtypeStruct((t, D_MODEL), F32)] + side_shape,
        scratch_shapes=[pltpu.VMEM(wgu.shape, BF16), pltpu.VMEM(wd.shape, BF16)],
        compiler_params=pltpu.CompilerParams(
            dimension_semantics=("arbitrary",), vmem_limit_bytes=VMEM_LIMIT),
        name="ffn1",
    )(x2d, gain, wgu, wd, w_in)


NAT_OUTPUTS = ((1, BF16), (1, BF16), (3, BF16), (1, F32), (1, F32), (4, F32))


def _proj_nat_kernel(h_ref, g_ref, w_ref, *refs):
    out_refs, slab_ref = refs[:-1], refs[-1]
    for half in range(2):
        rows = slice(half * (ROW_TILE // 2), (half + 1) * (ROW_TILE // 2))
        u = _rmsnorm(h_ref[rows, :], g_ref[...])
        for s in range(D_MODEL // LANES):
            slab_ref[s, rows, :] = u[:, s * LANES:(s + 1) * LANES]
        ub = u.astype(BF16)
        c = 0
        for (n_blocks, dtype), o_ref in zip(NAT_OUTPUTS, out_refs):
            for j in range(n_blocks):
                o_ref[rows, j * COL_BLOCK:(j + 1) * COL_BLOCK] = _dot(
                    ub, w_ref[:, c * COL_BLOCK:(c + 1) * COL_BLOCK]).astype(dtype)
                c += 1


def _proj_nat(h1, gain, w_nat):
    t = h1.shape[0]
    n_slabs = D_MODEL // LANES
    return pl.pallas_call(
        _proj_nat_kernel,
        grid=(t // ROW_TILE,),
        in_specs=[
            pl.BlockSpec((ROW_TILE, D_MODEL), lambda i: (i, 0)),
            _resident((1, D_MODEL)),
            _resident((D_MODEL, NAT_BF16_COLS + NAT_F32_COLS)),
        ],
        out_specs=[pl.BlockSpec((ROW_TILE, n * COL_BLOCK), lambda i: (i, 0)) for n, _ in NAT_OUTPUTS]
        + [pl.BlockSpec((n_slabs, ROW_TILE, LANES), lambda i: (0, i, 0))],
        out_shape=[jax.ShapeDtypeStruct((t, n * COL_BLOCK), dtype) for n, dtype in NAT_OUTPUTS]
        + [jax.ShapeDtypeStruct((n_slabs, t, LANES), F32)],
        compiler_params=pltpu.CompilerParams(
            dimension_semantics=("parallel",), vmem_limit_bytes=VMEM_LIMIT),
        name="proj_nat",
    )(h1, gain, w_nat)


def _proj_perm_kernel(slab_ref, *refs, seq):
    w_refs, o_ref = refs[:-1], refs[-1]
    t = pl.program_id(1)
    n_slabs = D_MODEL // LANES
    rows_out = o_ref.shape[1]

    def gather(g):
        d = ATT_DILATIONS[g]
        run = seq // d
        per_tile = rows_out // run if run < rows_out else 1
        pieces = []
        for c in range(per_tile):
            if run >= rows_out:
                r = (t * rows_out) // run
                start = r + d * ((t * rows_out) % run)
                n_rows = rows_out
            else:
                start = t * per_tile + c
                n_rows = run
            pieces.append(jnp.concatenate(
                [slab_ref[s, pl.ds(start, n_rows, stride=d), :].astype(BF16) for s in range(n_slabs)], axis=1))
        return pieces[0] if len(pieces) == 1 else jnp.concatenate(pieces, axis=0)

    lhs = {g: gather(g) for g in (1, 2)}
    per_group = 3 * ATT_WIDTH // COL_BLOCK
    for g in (1, 2):
        for c in range(per_group):
            blk = (g - 1) * per_group + c
            o_ref[0, :, blk * COL_BLOCK:(blk + 1) * COL_BLOCK] = _dot(lhs[g], w_refs[blk][...]).astype(BF16)


def _proj_perm(u_slab, w_all, b, seq):
    n_slabs = D_MODEL // LANES
    n_cols = N_PERM_BLOCKS * COL_BLOCK
    return pl.pallas_call(
        functools.partial(_proj_perm_kernel, seq=seq),
        grid=(b, seq // PERM_TILE),
        in_specs=[pl.BlockSpec((n_slabs, seq, LANES), lambda i, j: (0, i, 0))]
        + [pl.BlockSpec((D_MODEL, COL_BLOCK), lambda i, j, c=c: (0, N_NAT_BLOCKS + c), pipeline_mode=pl.Buffered(1))
           for c in range(N_PERM_BLOCKS)],
        out_specs=pl.BlockSpec((1, PERM_TILE, n_cols), lambda i, j: (i, j, 0)),
        out_shape=jax.ShapeDtypeStruct((b, seq, n_cols), BF16),
        compiler_params=pltpu.CompilerParams(
            dimension_semantics=("parallel", "arbitrary"), vmem_limit_bytes=VMEM_LIMIT),
        name="proj_perm",
    )(u_slab, *([w_all] * N_PERM_BLOCKS))


def _bcast_rows(x, block, row):
    n = x.shape[0] // block
    parts = [jnp.broadcast_to(x[i * block + row:i * block + row + 1, :], (block, x.shape[1]))
             for i in range(n)]
    return parts[0] if n == 1 else jnp.concatenate(parts, axis=0)


def _hgrn2_kernel(q_ref, i_ref, f_ref, og_ref, lbp_ref, gain_ref, *refs):
    n_side = (len(refs) - 2) // 2
    o_ref, st_ref = refs[n_side], refs[-1]
    _side_cast(refs[:n_side], refs[n_side + 1:-1])
    c_len = HG_CHUNK

    @pl.when(pl.program_id(1) == 0)
    def _():
        st_ref[...] = jnp.zeros_like(st_ref)

    lbp = lbp_ref[...]
    e = jnp.exp(lbp - jnp.max(lbp, axis=0, keepdims=True))
    lb_all = e[0:1, :] / jnp.sum(e, axis=0, keepdims=True)
    gain_all = gain_ref[...]

    t_idx = lax.broadcasted_iota(jnp.int32, (c_len, c_len), 0)
    s_idx = lax.broadcasted_iota(jnp.int32, (c_len, c_len), 1)
    halves = [1 << i for i in range(c_len.bit_length() - 1)]
    level_masks = [((t_idx >> b.bit_length()) == (s_idx >> b.bit_length()))
                   & ((t_idx & b) != 0) & ((s_idx & b) == 0) for b in halves]
    diag_mask = t_idx == s_idx
    ones_rhs = jnp.ones((HG_DIM, c_len), BF16)
    sub = lax.broadcasted_iota(jnp.int32, (c_len, HG_WIDTH), 0) & (SUBLANES - 1)
    pair = sub & 6

    def chunk(c):
        rows = slice(c * c_len, (c + 1) * c_len)
        q = q_ref[0, rows, :].astype(F32)
        v = i_ref[0, rows, :]
        f = lb_all + (1.0 - lb_all) * _sigmoid(f_ref[0, rows, :])
        k = 1.0 - f

        operands = [(q * f, k)]
        odd = (sub & 1) == 1
        pq = f * jnp.where(odd, pltpu.roll(f, 1, axis=0), 1.0)
        sk = jnp.where(odd, 1.0, pltpu.roll(f, c_len - 1, axis=0))
        operands.append((q * pq, k * sk))
        r1, r3, r5, r7 = (_bcast_rows(pq, SUBLANES, r) for r in (1, 3, 5, 7))
        sk = sk * jnp.where(pair == 0, r3, jnp.where(pair == 4, r7, 1.0))
        pq = pq * jnp.where(pair == 2, r1, jnp.where(pair == 6, r5, 1.0))
        operands.append((q * pq, k * sk))
        r3, r7 = _bcast_rows(pq, SUBLANES, 3), _bcast_rows(pq, SUBLANES, 7)
        sk = sk * jnp.where(sub < 4, r7, 1.0)
        pq = pq * jnp.where(sub >= 4, r3, 1.0)
        b = SUBLANES
        while b < c_len:
            n = c_len // b
            tot = _bcast_rows(pq, b, b - 1)
            zero = jnp.zeros((b, HG_WIDTH), F32)

            def blk(x, i, b=b):
                return x[i * b:(i + 1) * b]

            operands.append((
                jnp.concatenate([zero if i % 2 == 0 else blk(q, i) * blk(pq, i) for i in range(n)], axis=0),
                jnp.concatenate([blk(k, i) * blk(sk, i) if i % 2 == 0 else zero for i in range(n)], axis=0)))
            pq, sk = (
                jnp.concatenate([blk(pq, i) if i % 2 == 0 else blk(pq, i) * blk(tot, i - 1)
                                 for i in range(n)], axis=0),
                jnp.concatenate([blk(sk, i) * blk(tot, i + 1) if i % 2 == 0 else blk(sk, i)
                                 for i in range(n)], axis=0))
            b *= 2
        operands = [(a.astype(BF16), kk.astype(BF16)) for a, kk in operands]
        qk = (q * k).astype(BF16)
        qg = (q * pq).astype(BF16)
        kg = (k * sk).astype(BF16)
        chunk_decay = pq[c_len - 1:c_len, :]

        outs = []
        for h in range(HG_HEADS):
            cols = slice(h * HG_DIM, (h + 1) * HG_DIM)
            a = jnp.where(diag_mask, _dot(qk[:, cols], ones_rhs), 0.0)
            for (ql, kl), mask in zip(operands, level_masks):
                a = jnp.where(mask, _dot_nt(ql[:, cols], kl[:, cols]), a)
            st = st_ref[h]
            o = _dot(a.astype(BF16), v[:, cols]) + _dot_nt(qg[:, cols], st.astype(BF16))
            st_ref[h] = st * chunk_decay[:, cols] + _dot_tn(v[:, cols], kg[:, cols])
            outs.append(o * lax.rsqrt(jnp.mean(o * o, axis=-1, keepdims=True) + EPS))
        og = og_ref[0, rows, :]
        o_ref[0, rows, :] = (jnp.concatenate(outs, axis=1) * gain_all * (og * _sigmoid(og))).astype(BF16)

    for c in range(HG_TILE // c_len):
        chunk(c)


def _hgrn2(hg_q, hg_i, hg_f, hg_og, lower_bounds, out_gain, side_weights):
    b, seq, _ = hg_q.shape
    blk = (1, HG_TILE, HG_WIDTH)
    grid = (b, seq // HG_TILE)
    side_in, side_out, side_shape = _side_cast_specs(side_weights, grid)
    return pl.pallas_call(
        _hgrn2_kernel,
        grid=grid,
        in_specs=[
            pl.BlockSpec(blk, lambda i, j: (i, j, 0)),
            pl.BlockSpec(blk, lambda i, j: (i, j, 0)),
            pl.BlockSpec(blk, lambda i, j: (i, j, 0)),
            pl.BlockSpec(blk, lambda i, j: (i, j, 0)),
            _resident(lower_bounds.shape),
            _resident((1, HG_WIDTH)),
        ] + side_in,
        out_specs=[pl.BlockSpec(blk, lambda i, j: (i, j, 0))] + side_out,
        out_shape=[jax.ShapeDtypeStruct((b, seq, HG_WIDTH), BF16)] + side_shape,
        scratch_shapes=[pltpu.VMEM((HG_HEADS, HG_DIM, HG_DIM), F32)],
        compiler_params=pltpu.CompilerParams(
            dimension_semantics=("arbitrary", "arbitrary"), vmem_limit_bytes=VMEM_LIMIT),
        name="hgrn2",
    )(hg_q, hg_i, hg_f, hg_og, lower_bounds, out_gain, *side_weights)


def _att_kernel(slope_ref, *refs, seq, n_side):
    qkv_refs = refs[:3 * N_GROUPS]
    y_ref = refs[3 * N_GROUPS + n_side]
    _side_cast(refs[3 * N_GROUPS:3 * N_GROUPS + n_side],
               refs[3 * N_GROUPS + n_side + 1:3 * N_GROUPS + 2 * n_side + 1])
    scr = refs[3 * N_GROUPS + 2 * n_side + 1:]
    o_scr, l_scr = scr[0:N_GROUPS], scr[N_GROUPS:2 * N_GROUPS]
    y_scr, bias_scr = scr[2 * N_GROUPS], scr[2 * N_GROUPS + 1]
    blk = ATT_BLOCK
    n_blocks = seq // blk
    merge_d = ATT_DILATIONS[1]
    assert all(d % merge_d == 0 for d in ATT_DILATIONS[1:])

    qi = lax.broadcasted_iota(jnp.int32, (blk, 2 * blk), 0)
    kj = lax.broadcasted_iota(jnp.int32, (blk, 2 * blk), 1)
    dist = qi + blk - kj
    in_window = (dist >= 0) & (dist <= blk)
    dist_f = dist.astype(F32)
    for gh in range(2 * N_GROUPS):
        alibi = -slope_ref[0, gh:gh + 1, :] * dist_f
        bias_scr[gh] = jnp.where(in_window, alibi, NEG_INF)

    lane = lax.broadcasted_iota(jnp.int32, (blk, LANES), 1)
    first_head = lane < ATT_HEAD_DIM
    ones2 = jnp.ones((2 * blk, LANES), BF16)

    for g in range(N_GROUPS):
        d = ATT_DILATIONS[g]
        per_class = n_blocks // d
        q_ref, k_ref, v_ref = qkv_refs[3 * g:3 * g + 3]

        for idx in range(n_blocks):
            n, r = idx % per_class, idx // per_class
            rows = slice(idx * blk, (idx + 1) * blk)
            krows = rows if n == 0 else slice((idx - 1) * blk, (idx + 1) * blk)
            q = q_ref[0, rows, :]
            k = k_ref[0, krows, :]
            v_aug = jnp.concatenate([v_ref[0, krows, :], ones2[:k.shape[0]]], axis=1)
            pvs, ms = [], []
            for hh in range(2):
                qm = jnp.where(first_head if hh == 0 else ~first_head, q, jnp.zeros_like(q))
                if n == 0:
                    s = _dot_nt(qm, k) + bias_scr[2 * g + hh, :, blk:]
                    m = jnp.max(s, axis=-1, keepdims=True)
                else:
                    s = _dot_nt(qm, k) + bias_scr[2 * g + hh]
                    m = jnp.max(jnp.maximum(s[:, :blk], s[:, blk:]), axis=-1, keepdims=True)
                pvs.append(_dot(jnp.exp2(s - m).astype(BF16), v_aug))
                ms.append(jnp.broadcast_to(m, (blk, LANES)))
            o = jnp.where(first_head, pvs[0][:, :LANES], pvs[1][:, :LANES])
            den = jnp.where(first_head, pvs[0][:, LANES:], pvs[1][:, LANES:])
            mx = jnp.where(first_head, ms[0], ms[1])
            if d == 1:
                dst = rows
            else:
                dst = pl.ds((r % merge_d) * (seq // merge_d) + n * blk * (d // merge_d) + r // merge_d,
                            blk, stride=d // merge_d)
            o_scr[g][dst, :] = o / den
            l_scr[g][dst, :] = mx + jnp.log2(den)

    rc = 256
    for c4 in range(merge_d):
        for j0 in range(0, seq // merge_d, rc):
            rows = slice(c4 * (seq // merge_d) + j0, c4 * (seq // merge_d) + j0 + rc)
            tokens = pl.ds(c4 + merge_d * j0, rc, stride=merge_d)
            ls = [l_scr[0][tokens, :]] + [l_scr[g][rows, :] for g in range(1, N_GROUPS)]
            os_ = [o_scr[0][tokens, :]] + [o_scr[g][rows, :] for g in range(1, N_GROUPS)]
            m = jnp.maximum(jnp.maximum(ls[0], ls[1]), ls[2])
            ws = [jnp.exp2(lg - m) for lg in ls]
            num = ws[0] * os_[0] + ws[1] * os_[1] + ws[2] * os_[2]
            y_scr[tokens, :] = num / (ws[0] + ws[1] + ws[2])
    for c in range(seq // rc):
        rows = slice(c * rc, (c + 1) * rc)
        y_ref[0, rows, :] = y_scr[rows, :].astype(BF16)


def _att(proj_nat, proj_perm, slopes, side_weights):
    b, seq, _ = proj_nat.shape
    pairs = ATT_WIDTH // LANES
    per_tensor = ATT_WIDTH // LANES
    grid = (b, pairs)
    side_in, side_out, side_shape = _side_cast_specs(side_weights, grid)

    def qkv_spec(g, t):
        off = t * per_tensor if g == 0 else (3 * (g - 1) + t) * per_tensor
        return pl.BlockSpec((1, seq, LANES), lambda i, j, off=off: (i, 0, off + j))

    return pl.pallas_call(
        functools.partial(_att_kernel, seq=seq, n_side=len(side_weights)),
        grid=grid,
        in_specs=[pl.BlockSpec((1, SUBLANES, 2 * ATT_BLOCK), lambda i, j: (j, 0, 0))]
        + [qkv_spec(g, t) for g in range(N_GROUPS) for t in range(3)] + side_in,
        out_specs=[pl.BlockSpec((1, seq, LANES), lambda i, j: (i, 0, j))] + side_out,
        out_shape=[jax.ShapeDtypeStruct((b, seq, ATT_WIDTH), BF16)] + side_shape,
        scratch_shapes=[pltpu.VMEM((seq, LANES), F32) for _ in range(2 * N_GROUPS + 1)]
        + [pltpu.VMEM((2 * N_GROUPS, ATT_BLOCK, 2 * ATT_BLOCK), F32)],
        compiler_params=pltpu.CompilerParams(
            dimension_semantics=("arbitrary", "arbitrary"), vmem_limit_bytes=VMEM_LIMIT),
        name="att",
    )(slopes, *([proj_nat] * 3 + [proj_perm] * (3 * (N_GROUPS - 1))), *side_weights)


def _tail_kernel(h1_ref, yhg_ref, yatt_ref, ghg_ref, gatt_ref, wa_ref, wb_ref, wo_ref,
                 g2_ref, wgu_ref, wd_ref, gf_ref, o_ref):
    merged = (_sigmoid(ghg_ref[...]) * _dot(yhg_ref[...], wa_ref[...])
              + _sigmoid(gatt_ref[...]) * _dot(yatt_ref[...], wb_ref[...]))
    h2 = h1_ref[...] + _dot(merged.astype(BF16), wo_ref[...])
    xn = _rmsnorm(h2, g2_ref[...]).astype(BF16)
    h3 = h2 + 0.5 * _swiglu(xn, wgu_ref, wd_ref)
    o_ref[...] = _rmsnorm(h3, gf_ref[...])


def _tail(h1, y_hg, y_att, gates, wa, wb, wo, g2, wgu, wd, gf):
    t = h1.shape[0]
    row = lambda width, col=0: pl.BlockSpec((ROW_TILE, width), lambda i, col=col: (i, col))
    return pl.pallas_call(
        _tail_kernel,
        grid=(t // ROW_TILE,),
        in_specs=[
            row(D_MODEL), row(HG_WIDTH), row(ATT_WIDTH),
            row(D_MODEL, 0), row(D_MODEL, 1),
            _resident((HG_WIDTH, D_MODEL)), _resident((ATT_WIDTH, D_MODEL)), _resident((D_MODEL, D_MODEL)),
            _resident((1, D_MODEL)), _resident((D_MODEL, 2 * D_FF)), _resident((D_FF, D_MODEL)),
            _resident((1, D_MODEL)),
        ],
        out_specs=row(D_MODEL),
        out_shape=jax.ShapeDtypeStruct((t, D_MODEL), F32),
        compiler_params=pltpu.CompilerParams(
            dimension_semantics=("parallel",), vmem_limit_bytes=VMEM_LIMIT),
        name="tail",
    )(h1, y_hg, y_att, gates, gates, wa, wb, wo, g2, wgu, wd, gf)


def _alibi_table():
    n_heads = N_GROUPS * ATT_HEADS
    slopes = jnp.exp2(-ALIBI_MAX * jnp.arange(1, n_heads + 1, dtype=F32) / n_heads)
    slopes = slopes.reshape(N_GROUPS, ATT_HEADS // 2, 2) * jnp.asarray(ATT_DILATIONS, F32)[:, None, None] * LOG2_E
    table = jnp.zeros((ATT_HEADS // 2, SUBLANES, 2 * ATT_BLOCK), F32)
    table = table.at[:, :2 * N_GROUPS, :].set(
        jnp.broadcast_to(slopes.transpose(1, 0, 2).reshape(ATT_HEADS // 2, 2 * N_GROUPS, 1),
                         (ATT_HEADS // 2, 2 * N_GROUPS, 2 * ATT_BLOCK)))
    return table


def kernel(x, ffn1_norm, ffn1_w_gate_up, ffn1_w_down, mix_norm, w_in, hg_lower_bounds, hg_out_norm,
           w_branch_hg, w_branch_att, w_out, ffn2_norm, ffn2_w_gate_up, ffn2_w_down, final_norm):
    b, seq, d = x.shape
    assert d == D_MODEL and seq % (ATT_BLOCK * ATT_DILATIONS[-1]) == 0 and (b * seq) % ROW_TILE == 0
    x2d = x.reshape(b * seq, d)

    h1, w_all = _ffn1(x2d, ffn1_norm[0:1], ffn1_w_gate_up[0], ffn1_w_down[0], w_in[0])
    hg_q, hg_i, qkv0, hg_f, hg_og, gates, u_slab = _proj_nat(h1, mix_norm[0:1], w_all)
    proj_perm = _proj_perm(u_slab, w_all, b, seq)
    per_seq = lambda a: a.reshape(b, seq, -1)
    y_hg, wa, wb, wo = _hgrn2(per_seq(hg_q), per_seq(hg_i), per_seq(hg_f), per_seq(hg_og),
                              hg_lower_bounds.astype(F32), hg_out_norm[0:1],
                              [w_branch_hg[0], w_branch_att[0], w_out[0]])
    y_att, wgu2, wd2 = _att(per_seq(qkv0), proj_perm, _alibi_table(), [ffn2_w_gate_up[0], ffn2_w_down[0]])
    out = _tail(h1, y_hg.reshape(b * seq, -1), y_att.reshape(b * seq, -1), gates, wa, wb, wo,
                ffn2_norm[0:1], wgu2, wd2, final_norm.reshape(1, d))
    return out.reshape(b, seq, d)
```

```python
import functools

import jax
import jax.numpy as jnp
from jax import lax
from jax.experimental import pallas as pl
from jax.experimental.pallas import tpu as pltpu

F32 = jnp.float32
BF16 = jnp.bfloat16

D_MODEL = 1024
D_FF = 2816
HG_HEADS = 4
HG_DIM = 128
HG_WIDTH = HG_HEADS * HG_DIM
ATT_DILATIONS = (1, 4, 16)
ATT_BLOCK = 128
ATT_HEADS = 8
ATT_HEAD_DIM = 64
ATT_WIDTH = ATT_HEADS * ATT_HEAD_DIM
N_GROUPS = 3
ALIBI_MAX = 8.0
EPS = 1e-6
NEG_INF = -1e30
LOG2_E = 1.4426950408889634
Q_SCALE = ATT_HEAD_DIM ** -0.5 * LOG2_E

LANES = 128
SUBLANES = 8
MXU_DIM = 256
VMEM_LIMIT = 56 * 1024 * 1024

ROW_TILE = 512
COL_BLOCK = 512
NAT_BF16_COLS = 2 * HG_WIDTH + 3 * ATT_WIDTH
NAT_F32_COLS = 2 * HG_WIDTH + 2 * D_MODEL
HG_CHUNK = 128
HG_TILE = 1024
PERM_TILE = 1024
ATT_PAIRS_PER_STEP = 2


def _sigmoid(x):
    return 1.0 / (1.0 + jnp.exp(-x))


def _rmsnorm(x, gain):
    ms = jnp.mean(x * x, axis=-1, keepdims=True)
    return x * lax.rsqrt(ms + EPS) * gain


def _dot(a, b):
    return jnp.dot(a, b, preferred_element_type=F32)


def _dot_nt(a, b):
    return lax.dot_general(a, b, (((1,), (1,)), ((), ())), preferred_element_type=F32)


def _dot_tn(a, b):
    return lax.dot_general(a, b, (((0,), (0,)), ((), ())), preferred_element_type=F32)


def _swiglu(xn_bf16, wgu_ref, wd_ref):
    split = (D_FF // (2 * MXU_DIM) + 1) * MXU_DIM
    y = None
    for lo, hi in ((0, split), (split, D_FF)):
        a = _dot(xn_bf16, wgu_ref[:, lo:hi])
        b = _dot(xn_bf16, wgu_ref[:, D_FF + lo:D_FF + hi])
        act = (a * _sigmoid(a) * b).astype(BF16)
        part = _dot(act, wd_ref[lo:hi, :])
        y = part if y is None else y + part
    return y


W_IN_SRC_OF_DST = (0, 2, 4, 5, 6, 1, 3, 13, 14, 15, 16, 7, 8, 9, 10, 11, 12)
W_IN_Q_BLOCKS = (4, 7, 10)
N_NAT_BLOCKS = (NAT_BF16_COLS + NAT_F32_COLS) // COL_BLOCK
N_PERM_BLOCKS = len(W_IN_SRC_OF_DST) - N_NAT_BLOCKS


FFN1_LOAD_STEPS = 8


def _ffn1_kernel(x_ref, g_ref, wgu_f32_ref, wd_f32_ref, win_ref, o_ref, wall_ref, wgu_ref, wd_ref):
    step = pl.program_id(0)

    @pl.when(step < FFN1_LOAD_STEPS)
    def _load():
        for src, dst in ((wgu_f32_ref, wgu_ref), (wd_f32_ref, wd_ref)):
            n = src.shape[0]
            dst[pl.ds(pl.multiple_of(step * n, n), n), :] = src[...].astype(BF16)

    @pl.when(step >= FFN1_LOAD_STEPS)
    def _compute():
        for half in range(2):
            rows = slice(half * (ROW_TILE // 2), (half + 1) * (ROW_TILE // 2))
            x = x_ref[rows, :]
            xn = _rmsnorm(x, g_ref[...]).astype(BF16)
            o_ref[rows, :] = x + 0.5 * _swiglu(xn, wgu_ref, wd_ref)

        w = win_ref[...]
        wall_ref[...] = jnp.concatenate(
            [w[:, s * COL_BLOCK:(s + 1) * COL_BLOCK] * (Q_SCALE if s in W_IN_Q_BLOCKS else 1.0)
             for s in W_IN_SRC_OF_DST], axis=1).astype(BF16)


def _resident(shape):
    return pl.BlockSpec(shape, lambda *_: (0,) * len(shape), pipeline_mode=pl.Buffered(1))


BF16_ROWS = 2 * SUBLANES


def _side_cast_specs(weights, grid, first_step=0):
    n_steps = -first_step
    total = 1
    for g in grid:
        total *= g
    n_steps += total
    in_specs, out_specs, out_shape = [], [], []
    for w in weights:
        rows, cols = w.shape
        per_step = BF16_ROWS
        while rows % per_step or per_step * n_steps < rows:
            per_step += BF16_ROWS
        last = rows // per_step - 1

        def index(*ids, last=last):
            step = ids[0]
            for g, i in zip(grid[1:], ids[1:]):
                step = step * g + i
            return (jnp.clip(step - first_step, 0, last), 0)

        in_specs.append(pl.BlockSpec((per_step, cols), index))
        out_specs.append(pl.BlockSpec((per_step, cols), index))
        out_shape.append(jax.ShapeDtypeStruct((rows, cols), BF16))
    return in_specs, out_specs, out_shape


def _side_cast(in_refs, out_refs):
    for i_ref, o_ref in zip(in_refs, out_refs):
        o_ref[...] = i_ref[...].astype(BF16)


def _ffn1(x2d, gain, wgu, wd, w_in):
    t = x2d.shape[0]
    load = FFN1_LOAD_STEPS
    grid = (load + t // ROW_TILE,)
    side_in, side_out, side_shape = _side_cast_specs([w_in], grid, first_step=load)
    tile = lambda i: (jnp.maximum(i - load, 0), 0)
    piece = lambda i: (jnp.minimum(i, load - 1), 0)
    assert wgu.shape[0] % (load * BF16_ROWS) == 0 and wd.shape[0] % (load * BF16_ROWS) == 0
    return pl.pallas_call(
        _ffn1_kernel,
        grid=grid,
        in_specs=[
            pl.BlockSpec((ROW_TILE, D_MODEL), tile),
            _resident((1, D_MODEL)),
            pl.BlockSpec((wgu.shape[0] // load, wgu.shape[1]), piece),
            pl.BlockSpec((wd.shape[0] // load, wd.shape[1]), piece),
        ] + side_in,
        out_specs=[pl.BlockSpec((ROW_TILE, D_MODEL), tile)] + side_out,
        out_shape=[jax.ShapeDtypeStruct((t, D_MODEL), F32)] + side_shape,
        scratch_shapes=[pltpu.VMEM(wgu.shape, BF16), pltpu.VMEM(wd.shape, BF16)],
        compiler_params=pltpu.CompilerParams(
            dimension_semantics=("arbitrary",), vmem_limit_bytes=VMEM_LIMIT),
        name="ffn1",
    )(x2d, gain, wgu, wd, w_in)


NAT_OUTPUTS = ((1, BF16), (1, BF16), (3, BF16), (1, F32), (1, F32), (4, F32))


def _proj_nat_kernel(h_ref, g_ref, w_ref, *refs):
    out_refs, slab_ref = refs[:-1], refs[-1]
    for half in range(2):
        rows = slice(half * (ROW_TILE // 2), (half + 1) * (ROW_TILE // 2))
        u = _rmsnorm(h_ref[rows, :], g_ref[...])
        for s in range(D_MODEL // LANES):
            slab_ref[s, rows, :] = u[:, s * LANES:(s + 1) * LANES]
        ub = u.astype(BF16)
        c = 0
        for (n_blocks, dtype), o_ref in zip(NAT_OUTPUTS, out_refs):
            for j in range(n_blocks):
                o_ref[rows, j * COL_BLOCK:(j + 1) * COL_BLOCK] = _dot(
                    ub, w_ref[:, c * COL_BLOCK:(c + 1) * COL_BLOCK]).astype(dtype)
                c += 1


def _proj_nat(h1, gain, w_nat):
    t = h1.shape[0]
    n_slabs = D_MODEL // LANES
    return pl.pallas_call(
        _proj_nat_kernel,
        grid=(t // ROW_TILE,),
        in_specs=[
            pl.BlockSpec((ROW_TILE, D_MODEL), lambda i: (i, 0)),
            _resident((1, D_MODEL)),
            _resident((D_MODEL, NAT_BF16_COLS + NAT_F32_COLS)),
        ],
        out_specs=[pl.BlockSpec((ROW_TILE, n * COL_BLOCK), lambda i: (i, 0)) for n, _ in NAT_OUTPUTS]
        + [pl.BlockSpec((n_slabs, ROW_TILE, LANES), lambda i: (0, i, 0))],
        out_shape=[jax.ShapeDtypeStruct((t, n * COL_BLOCK), dtype) for n, dtype in NAT_OUTPUTS]
        + [jax.ShapeDtypeStruct((n_slabs, t, LANES), F32)],
        compiler_params=pltpu.CompilerParams(
            dimension_semantics=("parallel",), vmem_limit_bytes=VMEM_LIMIT),
        name="proj_nat",
    )(h1, gain, w_nat)


def _proj_perm_kernel(slab_ref, *refs, seq):
    w_refs, o_ref = refs[:-1], refs[-1]
    t = pl.program_id(1)
    n_slabs = D_MODEL // LANES
    rows_out = o_ref.shape[1]

    def gather(g):
        d = ATT_DILATIONS[g]
        run = seq // d
        per_tile = rows_out // run if run < rows_out else 1
        pieces = []
        for c in range(per_tile):
            if run >= rows_out:
                r = (t * rows_out) // run
                start = r + d * ((t * rows_out) % run)
                n_rows = rows_out
            else:
                start = t * per_tile + c
                n_rows = run
            pieces.append(jnp.concatenate(
                [slab_ref[s, pl.ds(start, n_rows, stride=d), :].astype(BF16) for s in range(n_slabs)], axis=1))
        return pieces[0] if len(pieces) == 1 else jnp.concatenate(pieces, axis=0)

    lhs = {g: gather(g) for g in (1, 2)}
    per_group = 3 * ATT_WIDTH // COL_BLOCK
    for g in (1, 2):
        for c in range(per_group):
            blk = (g - 1) * per_group + c
            o_ref[0, :, blk * COL_BLOCK:(blk + 1) * COL_BLOCK] = _dot(lhs[g], w_refs[blk][...]).astype(BF16)


def _proj_perm(u_slab, w_all, b, seq):
    n_slabs = D_MODEL // LANES
    n_cols = N_PERM_BLOCKS * COL_BLOCK
    return pl.pallas_call(
        functools.partial(_proj_perm_kernel, seq=seq),
        grid=(b, seq // PERM_TILE),
        in_specs=[pl.BlockSpec((n_slabs, seq, LANES), lambda i, j: (0, i, 0))]
        + [pl.BlockSpec((D_MODEL, COL_BLOCK), lambda i, j, c=c: (0, N_NAT_BLOCKS + c), pipeline_mode=pl.Buffered(1))
           for c in range(N_PERM_BLOCKS)],
        out_specs=pl.BlockSpec((1, PERM_TILE, n_cols), lambda i, j: (i, j, 0)),
        out_shape=jax.ShapeDtypeStruct((b, seq, n_cols), BF16),
        compiler_params=pltpu.CompilerParams(
            dimension_semantics=("parallel", "arbitrary"), vmem_limit_bytes=VMEM_LIMIT),
        name="proj_perm",
    )(u_slab, *([w_all] * N_PERM_BLOCKS))


def _bcast_rows(x, block, row):
    n = x.shape[0] // block
    parts = [jnp.broadcast_to(x[i * block + row:i * block + row + 1, :], (block, x.shape[1]))
             for i in range(n)]
    return parts[0] if n == 1 else jnp.concatenate(parts, axis=0)


def _hgrn2_kernel(q_ref, i_ref, f_ref, og_ref, lbp_ref, gain_ref, *refs):
    n_side = (len(refs) - 2) // 2
    o_ref, st_ref = refs[n_side], refs[-1]
    _side_cast(refs[:n_side], refs[n_side + 1:-1])
    c_len = HG_CHUNK

    @pl.when(pl.program_id(1) == 0)
    def _():
        st_ref[...] = jnp.zeros_like(st_ref)

    lbp = lbp_ref[...]
    e = jnp.exp(lbp - jnp.max(lbp, axis=0, keepdims=True))
    lb_all = e[0:1, :] / jnp.sum(e, axis=0, keepdims=True)
    gain_all = gain_ref[...]

    t_idx = lax.broadcasted_iota(jnp.int32, (c_len, c_len), 0)
    s_idx = lax.broadcasted_iota(jnp.int32, (c_len, c_len), 1)
    halves = [1 << i for i in range(c_len.bit_length() - 1)]
    level_masks = [((t_idx >> b.bit_length()) == (s_idx >> b.bit_length()))
                   & ((t_idx & b) != 0) & ((s_idx & b) == 0) for b in halves]
    diag_mask = t_idx == s_idx
    ones_rhs = jnp.ones((HG_DIM, c_len), BF16)
    sub = lax.broadcasted_iota(jnp.int32, (c_len, HG_WIDTH), 0) & (SUBLANES - 1)
    pair = sub & 6

    def chunk(c):
        rows = slice(c * c_len, (c + 1) * c_len)
        q = q_ref[0, rows, :].astype(F32)
        v = i_ref[0, rows, :]
        f = lb_all + (1.0 - lb_all) * _sigmoid(f_ref[0, rows, :])
        k = 1.0 - f

        operands = [(q * f, k)]
        odd = (sub & 1) == 1
        pq = f * jnp.where(odd, pltpu.roll(f, 1, axis=0), 1.0)
        sk = jnp.where(odd, 1.0, pltpu.roll(f, c_len - 1, axis=0))
        operands.append((q * pq, k * sk))
        r1, r3, r5, r7 = (_bcast_rows(pq, SUBLANES, r) for r in (1, 3, 5, 7))
        sk = sk * jnp.where(pair == 0, r3, jnp.where(pair == 4, r7, 1.0))
        pq = pq * jnp.where(pair == 2, r1, jnp.where(pair == 6, r5, 1.0))
        operands.append((q * pq, k * sk))
        r3, r7 = _bcast_rows(pq, SUBLANES, 3), _bcast_rows(pq, SUBLANES, 7)
        sk = sk * jnp.where(sub < 4, r7, 1.0)
        pq = pq * jnp.where(sub >= 4, r3, 1.0)
        b = SUBLANES
        while b < c_len:
            n = c_len // b
            tot = _bcast_rows(pq, b, b - 1)
            zero = jnp.zeros((b, HG_WIDTH), F32)

            def blk(x, i, b=b):
                return x[i * b:(i + 1) * b]

            operands.append((
                jnp.concatenate([zero if i % 2 == 0 else blk(q, i) * blk(pq, i) for i in range(n)], axis=0),
                jnp.concatenate([blk(k, i) * blk(sk, i) if i % 2 == 0 else zero for i in range(n)], axis=0)))
            pq, sk = (
                jnp.concatenate([blk(pq, i) if i % 2 == 0 else blk(pq, i) * blk(tot, i - 1)
                                 for i in range(n)], axis=0),
                jnp.concatenate([blk(sk, i) * blk(tot, i + 1) if i % 2 == 0 else blk(sk, i)
                                 for i in range(n)], axis=0))
            b *= 2
        operands = [(a.astype(BF16), kk.astype(BF16)) for a, kk in operands]
        qk = (q * k).astype(BF16)
        qg = (q * pq).astype(BF16)
        kg = (k * sk).astype(BF16)
        chunk_decay = pq[c_len - 1:c_len, :]

        outs = []
        for h in range(HG_HEADS):
            cols = slice(h * HG_DIM, (h + 1) * HG_DIM)
            a = jnp.where(diag_mask, _dot(qk[:, cols], ones_rhs), 0.0)
            for (ql, kl), mask in zip(operands, level_masks):
                a = jnp.where(mask, _dot_nt(ql[:, cols], kl[:, cols]), a)
            st = st_ref[h]
            o = _dot(a.astype(BF16), v[:, cols]) + _dot_nt(qg[:, cols], st.astype(BF16))
            st_ref[h] = st * chunk_decay[:, cols] + _dot_tn(v[:, cols], kg[:, cols])
            outs.append(o * lax.rsqrt(jnp.mean(o * o, axis=-1, keepdims=True) + EPS))
        og = og_ref[0, rows, :]
        o_ref[0, rows, :] = (jnp.concatenate(outs, axis=1) * gain_all * (og * _sigmoid(og))).astype(BF16)

    for c in range(HG_TILE // c_len):
        chunk(c)


def _hgrn2(hg_q, hg_i, hg_f, hg_og, lower_bounds, out_gain, side_weights):
    b, seq, _ = hg_q.shape
    blk = (1, HG_TILE, HG_WIDTH)
    grid = (b, seq // HG_TILE)
    side_in, side_out, side_shape = _side_cast_specs(side_weights, grid)
    return pl.pallas_call(
        _hgrn2_kernel,
        grid=grid,
        in_specs=[
            pl.BlockSpec(blk, lambda i, j: (i, j, 0)),
            pl.BlockSpec(blk, lambda i, j: (i, j, 0)),
            pl.BlockSpec(blk, lambda i, j: (i, j, 0)),
            pl.BlockSpec(blk, lambda i, j: (i, j, 0)),
            _resident(lower_bounds.shape),
            _resident((1, HG_WIDTH)),
        ] + side_in,
        out_specs=[pl.BlockSpec(blk, lambda i, j: (i, j, 0))] + side_out,
        out_shape=[jax.ShapeDtypeStruct((b, seq, HG_WIDTH), BF16)] + side_shape,
        scratch_shapes=[pltpu.VMEM((HG_HEADS, HG_DIM, HG_DIM), F32)],
        compiler_params=pltpu.CompilerParams(
            dimension_semantics=("arbitrary", "arbitrary"), vmem_limit_bytes=VMEM_LIMIT),
        name="hgrn2",
    )(hg_q, hg_i, hg_f, hg_og, lower_bounds, out_gain, *side_weights)


def _att_kernel(slope_ref, *refs, seq, n_side):
    qkv_refs = refs[:3 * N_GROUPS]
    y_ref = refs[3 * N_GROUPS + n_side]
    _side_cast(refs[3 * N_GROUPS:3 * N_GROUPS + n_side],
               refs[3 * N_GROUPS + n_side + 1:3 * N_GROUPS + 2 * n_side + 1])
    scratch = refs[3 * N_GROUPS + 2 * n_side + 1:]
    per_pair = len(scratch) // ATT_PAIRS_PER_STEP
    blk = ATT_BLOCK
    n_blocks = seq // blk
    merge_d = ATT_DILATIONS[1]
    assert all(d % merge_d == 0 for d in ATT_DILATIONS[1:])

    lane = lax.broadcasted_iota(jnp.int32, (blk, LANES), 1)
    first_head = lane < ATT_HEAD_DIM
    ones2 = jnp.ones((2 * blk, LANES), BF16)

    for pair in range(ATT_PAIRS_PER_STEP):
        lanes = slice(pair * LANES, (pair + 1) * LANES)
        scr = scratch[pair * per_pair:(pair + 1) * per_pair]
        o_scr, l_scr = scr[0:N_GROUPS], scr[N_GROUPS:2 * N_GROUPS]
        y_scr, bias_scr = scr[2 * N_GROUPS], scr[2 * N_GROUPS + 1]

        qi = lax.broadcasted_iota(jnp.int32, (blk, 2 * blk), 0)
        kj = lax.broadcasted_iota(jnp.int32, (blk, 2 * blk), 1)
        dist = qi + blk - kj
        in_window = (dist >= 0) & (dist <= blk)
        dist_f = dist.astype(F32)
        for gh in range(2 * N_GROUPS):
            alibi = -slope_ref[pair, gh:gh + 1, :] * dist_f
            bias_scr[gh] = jnp.where(in_window, alibi, NEG_INF)

        for g in range(N_GROUPS):
            d = ATT_DILATIONS[g]
            per_class = n_blocks // d
            q_ref, k_ref, v_ref = qkv_refs[3 * g:3 * g + 3]

            for idx in range(n_blocks):
                n, r = idx % per_class, idx // per_class
                rows = slice(idx * blk, (idx + 1) * blk)
                krows = rows if n == 0 else slice((idx - 1) * blk, (idx + 1) * blk)
                q = q_ref[0, rows, lanes]
                k = k_ref[0, krows, lanes]
                v_aug = jnp.concatenate([v_ref[0, krows, lanes], ones2[:k.shape[0]]], axis=1)
                pvs, ms = [], []
                for hh in range(2):
                    qm = jnp.where(first_head if hh == 0 else ~first_head, q, jnp.zeros_like(q))
                    if n == 0:
                        s = _dot_nt(qm, k) + bias_scr[2 * g + hh, :, blk:]
                        m = jnp.max(s, axis=-1, keepdims=True)
                    else:
                        s = _dot_nt(qm, k) + bias_scr[2 * g + hh]
                        m = jnp.max(jnp.maximum(s[:, :blk], s[:, blk:]), axis=-1, keepdims=True)
                    pvs.append(_dot(jnp.exp2(s - m).astype(BF16), v_aug))
                    ms.append(jnp.broadcast_to(m, (blk, LANES)))
                o = jnp.where(first_head, pvs[0][:, :LANES], pvs[1][:, :LANES])
                den = jnp.where(first_head, pvs[0][:, LANES:], pvs[1][:, LANES:])
                mx = jnp.where(first_head, ms[0], ms[1])
                if d == 1:
                    dst = rows
                else:
                    dst = pl.ds((r % merge_d) * (seq // merge_d) + n * blk * (d // merge_d) + r // merge_d,
                                blk, stride=d // merge_d)
                o_scr[g][dst, :] = o / den
                l_scr[g][dst, :] = mx + jnp.log2(den)

        rc = 256
        for c4 in range(merge_d):
            for j0 in range(0, seq // merge_d, rc):
                rows = slice(c4 * (seq // merge_d) + j0, c4 * (seq // merge_d) + j0 + rc)
                tokens = pl.ds(c4 + merge_d * j0, rc, stride=merge_d)
                ls = [l_scr[0][tokens, :]] + [l_scr[g][rows, :] for g in range(1, N_GROUPS)]
                os_ = [o_scr[0][tokens, :]] + [o_scr[g][rows, :] for g in range(1, N_GROUPS)]
                m = jnp.maximum(jnp.maximum(ls[0], ls[1]), ls[2])
                ws = [jnp.exp2(lg - m) for lg in ls]
                num = ws[0] * os_[0] + ws[1] * os_[1] + ws[2] * os_[2]
                y_scr[tokens, :] = num / (ws[0] + ws[1] + ws[2])
        for c in range(seq // rc):
            rows = slice(c * rc, (c + 1) * rc)
            y_ref[0, rows, lanes] = y_scr[rows, :].astype(BF16)


def _att(proj_nat, proj_perm, slopes, side_weights):
    b, seq, _ = proj_nat.shape
    per_step = ATT_PAIRS_PER_STEP * LANES
    per_tensor = ATT_WIDTH // per_step
    grid = (b, per_tensor)
    side_in, side_out, side_shape = _side_cast_specs(side_weights, grid)

    def qkv_spec(g, t):
        off = t * per_tensor if g == 0 else (3 * (g - 1) + t) * per_tensor
        return pl.BlockSpec((1, seq, per_step), lambda i, j, off=off: (i, 0, off + j))

    return pl.pallas_call(
        functools.partial(_att_kernel, seq=seq, n_side=len(side_weights)),
        grid=grid,
        in_specs=[pl.BlockSpec((ATT_PAIRS_PER_STEP, SUBLANES, 2 * ATT_BLOCK), lambda i, j: (j, 0, 0))]
        + [qkv_spec(g, t) for g in range(N_GROUPS) for t in range(3)] + side_in,
        out_specs=[pl.BlockSpec((1, seq, per_step), lambda i, j: (i, 0, j))] + side_out,
        out_shape=[jax.ShapeDtypeStruct((b, seq, ATT_WIDTH), BF16)] + side_shape,
        scratch_shapes=ATT_PAIRS_PER_STEP * ([pltpu.VMEM((seq, LANES), F32) for _ in range(2 * N_GROUPS + 1)]
                                             + [pltpu.VMEM((2 * N_GROUPS, ATT_BLOCK, 2 * ATT_BLOCK), F32)]),
        compiler_params=pltpu.CompilerParams(
            dimension_semantics=("arbitrary", "arbitrary"), vmem_limit_bytes=VMEM_LIMIT),
        name="att",
    )(slopes, *([proj_nat] * 3 + [proj_perm] * (3 * (N_GROUPS - 1))), *side_weights)


def _tail_kernel(h1_ref, yhg_ref, yatt_ref, ghg_ref, gatt_ref, wa_ref, wb_ref, wo_ref,
                 g2_ref, wgu_ref, wd_ref, gf_ref, o_ref):
    merged = (_sigmoid(ghg_ref[...]) * _dot(yhg_ref[...], wa_ref[...])
              + _sigmoid(gatt_ref[...]) * _dot(yatt_ref[...], wb_ref[...]))
    h2 = h1_ref[...] + _dot(merged.astype(BF16), wo_ref[...])
    xn = _rmsnorm(h2, g2_ref[...]).astype(BF16)
    h3 = h2 + 0.5 * _swiglu(xn, wgu_ref, wd_ref)
    o_ref[...] = _rmsnorm(h3, gf_ref[...])


def _tail(h1, y_hg, y_att, gates, wa, wb, wo, g2, wgu, wd, gf):
    t = h1.shape[0]
    row = lambda width, col=0: pl.BlockSpec((ROW_TILE, width), lambda i, col=col: (i, col))
    return pl.pallas_call(
        _tail_kernel,
        grid=(t // ROW_TILE,),
        in_specs=[
            row(D_MODEL), row(HG_WIDTH), row(ATT_WIDTH),
            row(D_MODEL, 0), row(D_MODEL, 1),
            _resident((HG_WIDTH, D_MODEL)), _resident((ATT_WIDTH, D_MODEL)), _resident((D_MODEL, D_MODEL)),
            _resident((1, D_MODEL)), _resident((D_MODEL, 2 * D_FF)), _resident((D_FF, D_MODEL)),
            _resident((1, D_MODEL)),
        ],
        out_specs=row(D_MODEL),
        out_shape=jax.ShapeDtypeStruct((t, D_MODEL), F32),
        compiler_params=pltpu.CompilerParams(
            dimension_semantics=("parallel",), vmem_limit_bytes=VMEM_LIMIT),
        name="tail",
    )(h1, y_hg, y_att, gates, gates, wa, wb, wo, g2, wgu, wd, gf)


def _alibi_table():
    n_heads = N_GROUPS * ATT_HEADS
    slopes = jnp.exp2(-ALIBI_MAX * jnp.arange(1, n_heads + 1, dtype=F32) / n_heads)
    slopes = slopes.reshape(N_GROUPS, ATT_HEADS // 2, 2) * jnp.asarray(ATT_DILATIONS, F32)[:, None, None] * LOG2_E
    table = jnp.zeros((ATT_HEADS // 2, SUBLANES, 2 * ATT_BLOCK), F32)
    table = table.at[:, :2 * N_GROUPS, :].set(
        jnp.broadcast_to(slopes.transpose(1, 0, 2).reshape(ATT_HEADS // 2, 2 * N_GROUPS, 1),
                         (ATT_HEADS // 2, 2 * N_GROUPS, 2 * ATT_BLOCK)))
    return table


def kernel(x, ffn1_norm, ffn1_w_gate_up, ffn1_w_down, mix_norm, w_in, hg_lower_bounds, hg_out_norm,
           w_branch_hg, w_branch_att, w_out, ffn2_norm, ffn2_w_gate_up, ffn2_w_down, final_norm):
    b, seq, d = x.shape
    assert d == D_MODEL and seq % (ATT_BLOCK * ATT_DILATIONS[-1]) == 0 and (b * seq) % ROW_TILE == 0
    x2d = x.reshape(b * seq, d)

    h1, w_all = _ffn1(x2d, ffn1_norm[0:1], ffn1_w_gate_up[0], ffn1_w_down[0], w_in[0])
    hg_q, hg_i, qkv0, hg_f, hg_og, gates, u_slab = _proj_nat(h1, mix_norm[0:1], w_all)
    proj_perm = _proj_perm(u_slab, w_all, b, seq)
    per_seq = lambda a: a.reshape(b, seq, -1)
    y_hg, wa, wb, wo = _hgrn2(per_seq(hg_q), per_seq(hg_i), per_seq(hg_f), per_seq(hg_og),
                              hg_lower_bounds.astype(F32), hg_out_norm[0:1],
                              [w_branch_hg[0], w_branch_att[0], w_out[0]])
    y_att, wgu2, wd2 = _att(per_seq(qkv0), proj_perm, _alibi_table(), [ffn2_w_gate_up[0], ffn2_w_down[0]])
    out = _tail(h1, y_hg.reshape(b * seq, -1), y_att.reshape(b * seq, -1), gates, wa, wb, wo,
                ffn2_norm[0:1], wgu2, wd2, final_norm.reshape(1, d))
    return out.reshape(b, seq, d)
```

```python
import functools

import jax
import jax.numpy as jnp
from jax import lax
from jax.experimental import pallas as pl
from jax.experimental.pallas import tpu as pltpu

F32 = jnp.float32
BF16 = jnp.bfloat16

D_MODEL = 1024
D_FF = 2816
HG_HEADS = 4
HG_DIM = 128
HG_WIDTH = HG_HEADS * HG_DIM
ATT_DILATIONS = (1, 4, 16)
ATT_BLOCK = 128
ATT_HEADS = 8
ATT_HEAD_DIM = 64
ATT_WIDTH = ATT_HEADS * ATT_HEAD_DIM
N_GROUPS = 3
ALIBI_MAX = 8.0
EPS = 1e-6
NEG_INF = -1e30
LOG2_E = 1.4426950408889634
Q_SCALE = ATT_HEAD_DIM ** -0.5 * LOG2_E

LANES = 128
SUBLANES = 8
MXU_DIM = 256
VMEM_LIMIT = 56 * 1024 * 1024

ROW_TILE = 512
FFN1_TILE = 1024
SUB_TILE = 256
COL_BLOCK = 512
NAT_BF16_COLS = 2 * HG_WIDTH + 3 * ATT_WIDTH
NAT_F32_COLS = 2 * HG_WIDTH + 2 * D_MODEL
HG_CHUNK = 128
HG_TILE = 1024
PERM_TILE = 1024
ATT_PAIRS_PER_STEP = 2


def _sigmoid(x):
    return 1.0 / (1.0 + jnp.exp(-x))


def _rmsnorm(x, gain):
    ms = jnp.mean(x * x, axis=-1, keepdims=True)
    return x * lax.rsqrt(ms + EPS) * gain


def _dot(a, b):
    return jnp.dot(a, b, preferred_element_type=F32)


def _dot_nt(a, b):
    return lax.dot_general(a, b, (((1,), (1,)), ((), ())), preferred_element_type=F32)


def _dot_tn(a, b):
    return lax.dot_general(a, b, (((0,), (0,)), ((), ())), preferred_element_type=F32)


def _swiglu(xn_bf16, wgu_ref, wd_ref):
    split = (D_FF // (2 * MXU_DIM) + 1) * MXU_DIM
    y = None
    for lo, hi in ((0, split), (split, D_FF)):
        a = _dot(xn_bf16, wgu_ref[:, lo:hi])
        b = _dot(xn_bf16, wgu_ref[:, D_FF + lo:D_FF + hi])
        act = (a * _sigmoid(a) * b).astype(BF16)
        part = _dot(act, wd_ref[lo:hi, :])
        y = part if y is None else y + part
    return y


W_IN_SRC_OF_DST = (0, 2, 4, 5, 6, 1, 3, 13, 14, 15, 16, 7, 8, 9, 10, 11, 12)
W_IN_Q_BLOCKS = (4, 7, 10)
N_NAT_BLOCKS = (NAT_BF16_COLS + NAT_F32_COLS) // COL_BLOCK
N_PERM_BLOCKS = len(W_IN_SRC_OF_DST) - N_NAT_BLOCKS


FFN1_LOAD_STEPS = 8


def _ffn1_kernel(x_ref, g_ref, wgu_f32_ref, wd_f32_ref, win_ref, o_ref, wall_ref, wgu_ref, wd_ref):
    step = pl.program_id(0)

    @pl.when(step < FFN1_LOAD_STEPS)
    def _load():
        for src, dst in ((wgu_f32_ref, wgu_ref), (wd_f32_ref, wd_ref)):
            n = src.shape[0]
            dst[pl.ds(pl.multiple_of(step * n, n), n), :] = src[...].astype(BF16)

    @pl.when(step >= FFN1_LOAD_STEPS)
    def _compute():
        for part in range(FFN1_TILE // SUB_TILE):
            rows = slice(part * SUB_TILE, (part + 1) * SUB_TILE)
            x = x_ref[rows, :]
            xn = _rmsnorm(x, g_ref[...]).astype(BF16)
            o_ref[rows, :] = x + 0.5 * _swiglu(xn, wgu_ref, wd_ref)

        w = win_ref[...]
        wall_ref[...] = jnp.concatenate(
            [w[:, s * COL_BLOCK:(s + 1) * COL_BLOCK] * (Q_SCALE if s in W_IN_Q_BLOCKS else 1.0)
             for s in W_IN_SRC_OF_DST], axis=1).astype(BF16)


def _resident(shape):
    return pl.BlockSpec(shape, lambda *_: (0,) * len(shape), pipeline_mode=pl.Buffered(1))


BF16_ROWS = 2 * SUBLANES


def _side_cast_specs(weights, grid, first_step=0):
    n_steps = -first_step
    total = 1
    for g in grid:
        total *= g
    n_steps += total
    in_specs, out_specs, out_shape = [], [], []
    for w in weights:
        rows, cols = w.shape
        per_step = BF16_ROWS
        while rows % per_step or per_step * n_steps < rows:
            per_step += BF16_ROWS
        last = rows // per_step - 1

        def index(*ids, last=last):
            step = ids[0]
            for g, i in zip(grid[1:], ids[1:]):
                step = step * g + i
            return (jnp.clip(step - first_step, 0, last), 0)

        in_specs.append(pl.BlockSpec((per_step, cols), index))
        out_specs.append(pl.BlockSpec((per_step, cols), index))
        out_shape.append(jax.ShapeDtypeStruct((rows, cols), BF16))
    return in_specs, out_specs, out_shape


def _side_cast(in_refs, out_refs):
    for i_ref, o_ref in zip(in_refs, out_refs):
        o_ref[...] = i_ref[...].astype(BF16)


def _ffn1(x2d, gain, wgu, wd, w_in):
    t = x2d.shape[0]
    load = FFN1_LOAD_STEPS
    grid = (load + t // FFN1_TILE,)
    side_in, side_out, side_shape = _side_cast_specs([w_in], grid, first_step=load)
    tile = lambda i: (jnp.maximum(i - load, 0), 0)
    piece = lambda i: (jnp.minimum(i, load - 1), 0)
    assert wgu.shape[0] % (load * BF16_ROWS) == 0 and wd.shape[0] % (load * BF16_ROWS) == 0
    return pl.pallas_call(
        _ffn1_kernel,
        grid=grid,
        in_specs=[
            pl.BlockSpec((FFN1_TILE, D_MODEL), tile),
            _resident((1, D_MODEL)),
            pl.BlockSpec((wgu.shape[0] // load, wgu.shape[1]), piece),
            pl.BlockSpec((wd.shape[0] // load, wd.shape[1]), piece),
        ] + side_in,
        out_specs=[pl.BlockSpec((FFN1_TILE, D_MODEL), tile)] + side_out,
        out_shape=[jax.ShapeDtypeStruct((t, D_MODEL), F32)] + side_shape,
        scratch_shapes=[pltpu.VMEM(wgu.shape, BF16), pltpu.VMEM(wd.shape, BF16)],
        compiler_params=pltpu.CompilerParams(
            dimension_semantics=("arbitrary",), vmem_limit_bytes=VMEM_LIMIT),
        name="ffn1",
    )(x2d, gain, wgu, wd, w_in)


NAT_OUTPUTS = ((1, BF16), (1, BF16), (3, BF16), (1, F32), (1, F32), (4, F32))


def _proj_nat_kernel(h_ref, g_ref, w_ref, *refs):
    out_refs, slab_ref = refs[:-1], refs[-1]
    for part in range(ROW_TILE // SUB_TILE):
        rows = slice(part * SUB_TILE, (part + 1) * SUB_TILE)
        u = _rmsnorm(h_ref[rows, :], g_ref[...])
        for s in range(D_MODEL // LANES):
            slab_ref[s, rows, :] = u[:, s * LANES:(s + 1) * LANES]
        ub = u.astype(BF16)
        c = 0
        for (n_blocks, dtype), o_ref in zip(NAT_OUTPUTS, out_refs):
            for j in range(n_blocks):
                o_ref[rows, j * COL_BLOCK:(j + 1) * COL_BLOCK] = _dot(
                    ub, w_ref[:, c * COL_BLOCK:(c + 1) * COL_BLOCK]).astype(dtype)
                c += 1


def _proj_nat(h1, gain, w_nat):
    t = h1.shape[0]
    n_slabs = D_MODEL // LANES
    return pl.pallas_call(
        _proj_nat_kernel,
        grid=(t // ROW_TILE,),
        in_specs=[
            pl.BlockSpec((ROW_TILE, D_MODEL), lambda i: (i, 0)),
            _resident((1, D_MODEL)),
            _resident((D_MODEL, NAT_BF16_COLS + NAT_F32_COLS)),
        ],
        out_specs=[pl.BlockSpec((ROW_TILE, n * COL_BLOCK), lambda i: (i, 0)) for n, _ in NAT_OUTPUTS]
        + [pl.BlockSpec((n_slabs, ROW_TILE, LANES), lambda i: (0, i, 0))],
        out_shape=[jax.ShapeDtypeStruct((t, n * COL_BLOCK), dtype) for n, dtype in NAT_OUTPUTS]
        + [jax.ShapeDtypeStruct((n_slabs, t, LANES), F32)],
        compiler_params=pltpu.CompilerParams(
            dimension_semantics=("parallel",), vmem_limit_bytes=VMEM_LIMIT),
        name="proj_nat",
    )(h1, gain, w_nat)


def _proj_perm_kernel(slab_ref, *refs, seq):
    w_refs, o_ref = refs[:-1], refs[-1]
    t = pl.program_id(1)
    n_slabs = D_MODEL // LANES
    rows_out = o_ref.shape[1]

    def gather(g):
        d = ATT_DILATIONS[g]
        run = seq // d
        per_tile = rows_out // run if run < rows_out else 1
        pieces = []
        for c in range(per_tile):
            if run >= rows_out:
                r = (t * rows_out) // run
                start = r + d * ((t * rows_out) % run)
                n_rows = rows_out
            else:
                start = t * per_tile + c
                n_rows = run
            pieces.append(jnp.concatenate(
                [slab_ref[s, pl.ds(start, n_rows, stride=d), :].astype(BF16) for s in range(n_slabs)], axis=1))
        return pieces[0] if len(pieces) == 1 else jnp.concatenate(pieces, axis=0)

    lhs = {g: gather(g) for g in (1, 2)}
    per_group = 3 * ATT_WIDTH // COL_BLOCK
    for g in (1, 2):
        for c in range(per_group):
            blk = (g - 1) * per_group + c
            o_ref[0, :, blk * COL_BLOCK:(blk + 1) * COL_BLOCK] = _dot(lhs[g], w_refs[blk][...]).astype(BF16)


def _proj_perm(u_slab, w_all, b, seq):
    n_slabs = D_MODEL // LANES
    n_cols = N_PERM_BLOCKS * COL_BLOCK
    return pl.pallas_call(
        functools.partial(_proj_perm_kernel, seq=seq),
        grid=(b, seq // PERM_TILE),
        in_specs=[pl.BlockSpec((n_slabs, seq, LANES), lambda i, j: (0, i, 0))]
        + [pl.BlockSpec((D_MODEL, COL_BLOCK), lambda i, j, c=c: (0, N_NAT_BLOCKS + c), pipeline_mode=pl.Buffered(1))
           for c in range(N_PERM_BLOCKS)],
        out_specs=pl.BlockSpec((1, PERM_TILE, n_cols), lambda i, j: (i, j, 0)),
        out_shape=jax.ShapeDtypeStruct((b, seq, n_cols), BF16),
        compiler_params=pltpu.CompilerParams(
            dimension_semantics=("parallel", "arbitrary"), vmem_limit_bytes=VMEM_LIMIT),
        name="proj_perm",
    )(u_slab, *([w_all] * N_PERM_BLOCKS))


def _bcast_rows(x, block, row):
    n = x.shape[0] // block
    parts = [jnp.broadcast_to(x[i * block + row:i * block + row + 1, :], (block, x.shape[1]))
             for i in range(n)]
    return parts[0] if n == 1 else jnp.concatenate(parts, axis=0)


def _hgrn2_kernel(q_ref, i_ref, f_ref, og_ref, lbp_ref, gain_ref, *refs):
    n_side = (len(refs) - 2) // 2
    o_ref, st_ref = refs[n_side], refs[-1]
    _side_cast(refs[:n_side], refs[n_side + 1:-1])
    c_len = HG_CHUNK

    @pl.when(pl.program_id(1) == 0)
    def _():
        st_ref[...] = jnp.zeros_like(st_ref)

    lbp = lbp_ref[...]
    e = jnp.exp(lbp - jnp.max(lbp, axis=0, keepdims=True))
    lb_all = e[0:1, :] / jnp.sum(e, axis=0, keepdims=True)
    gain_all = gain_ref[...]

    t_idx = lax.broadcasted_iota(jnp.int32, (c_len, c_len), 0)
    s_idx = lax.broadcasted_iota(jnp.int32, (c_len, c_len), 1)
    halves = [1 << i for i in range(c_len.bit_length() - 1)]
    level_masks = [((t_idx >> b.bit_length()) == (s_idx >> b.bit_length()))
                   & ((t_idx & b) != 0) & ((s_idx & b) == 0) for b in halves]
    diag_mask = t_idx == s_idx
    ones_rhs = jnp.ones((HG_DIM, c_len), BF16)
    sub = lax.broadcasted_iota(jnp.int32, (c_len, HG_WIDTH), 0) & (SUBLANES - 1)
    pair = sub & 6

    def chunk(c):
        rows = slice(c * c_len, (c + 1) * c_len)
        q = q_ref[0, rows, :].astype(F32)
        v = i_ref[0, rows, :]
        f = lb_all + (1.0 - lb_all) * _sigmoid(f_ref[0, rows, :])
        k = 1.0 - f

        operands = [(q * f, k)]
        odd = (sub & 1) == 1
        pq = f * jnp.where(odd, pltpu.roll(f, 1, axis=0), 1.0)
        sk = jnp.where(odd, 1.0, pltpu.roll(f, c_len - 1, axis=0))
        operands.append((q * pq, k * sk))
        r1, r3, r5, r7 = (_bcast_rows(pq, SUBLANES, r) for r in (1, 3, 5, 7))
        sk = sk * jnp.where(pair == 0, r3, jnp.where(pair == 4, r7, 1.0))
        pq = pq * jnp.where(pair == 2, r1, jnp.where(pair == 6, r5, 1.0))
        operands.append((q * pq, k * sk))
        r3, r7 = _bcast_rows(pq, SUBLANES, 3), _bcast_rows(pq, SUBLANES, 7)
        sk = sk * jnp.where(sub < 4, r7, 1.0)
        pq = pq * jnp.where(sub >= 4, r3, 1.0)
        b = SUBLANES
        while b < c_len:
            n = c_len // b
            tot = _bcast_rows(pq, b, b - 1)
            zero = jnp.zeros((b, HG_WIDTH), F32)

            def blk(x, i, b=b):
                return x[i * b:(i + 1) * b]

            operands.append((
                jnp.concatenate([zero if i % 2 == 0 else blk(q, i) * blk(pq, i) for i in range(n)], axis=0),
                jnp.concatenate([blk(k, i) * blk(sk, i) if i % 2 == 0 else zero for i in range(n)], axis=0)))
            pq, sk = (
                jnp.concatenate([blk(pq, i) if i % 2 == 0 else blk(pq, i) * blk(tot, i - 1)
                                 for i in range(n)], axis=0),
                jnp.concatenate([blk(sk, i) * blk(tot, i + 1) if i % 2 == 0 else blk(sk, i)
                                 for i in range(n)], axis=0))
            b *= 2
        operands = [(a.astype(BF16), kk.astype(BF16)) for a, kk in operands]
        qk = (q * k).astype(BF16)
        qg = (q * pq).astype(BF16)
        kg = (k * sk).astype(BF16)
        chunk_decay = pq[c_len - 1:c_len, :]

        outs = []
        for h in range(HG_HEADS):
            cols = slice(h * HG_DIM, (h + 1) * HG_DIM)
            a = jnp.where(diag_mask, _dot(qk[:, cols], ones_rhs), 0.0)
            for (ql, kl), mask in zip(operands, level_masks):
                a = jnp.where(mask, _dot_nt(ql[:, cols], kl[:, cols]), a)
            st = st_ref[h]
            o = _dot(a.astype(BF16), v[:, cols]) + _dot_nt(qg[:, cols], st.astype(BF16))
            st_ref[h] = st * chunk_decay[:, cols] + _dot_tn(v[:, cols], kg[:, cols])
            outs.append(o * lax.rsqrt(jnp.mean(o * o, axis=-1, keepdims=True) + EPS))
        og = og_ref[0, rows, :]
        o_ref[0, rows, :] = (jnp.concatenate(outs, axis=1) * gain_all * (og * _sigmoid(og))).astype(BF16)

    for c in range(HG_TILE // c_len):
        chunk(c)


def _hgrn2(hg_q, hg_i, hg_f, hg_og, lower_bounds, out_gain, side_weights):
    b, seq, _ = hg_q.shape
    blk = (1, HG_TILE, HG_WIDTH)
    grid = (b, seq // HG_TILE)
    side_in, side_out, side_shape = _side_cast_specs(side_weights, grid)
    return pl.pallas_call(
        _hgrn2_kernel,
        grid=grid,
        in_specs=[
            pl.BlockSpec(blk, lambda i, j: (i, j, 0)),
            pl.BlockSpec(blk, lambda i, j: (i, j, 0)),
            pl.BlockSpec(blk, lambda i, j: (i, j, 0)),
            pl.BlockSpec(blk, lambda i, j: (i, j, 0)),
            _resident(lower_bounds.shape),
            _resident((1, HG_WIDTH)),
        ] + side_in,
        out_specs=[pl.BlockSpec(blk, lambda i, j: (i, j, 0))] + side_out,
        out_shape=[jax.ShapeDtypeStruct((b, seq, HG_WIDTH), BF16)] + side_shape,
        scratch_shapes=[pltpu.VMEM((HG_HEADS, HG_DIM, HG_DIM), F32)],
        compiler_params=pltpu.CompilerParams(
            dimension_semantics=("arbitrary", "arbitrary"), vmem_limit_bytes=VMEM_LIMIT),
        name="hgrn2",
    )(hg_q, hg_i, hg_f, hg_og, lower_bounds, out_gain, *side_weights)


def _att_kernel(slope_ref, *refs, seq, n_side):
    qkv_refs = refs[:3 * N_GROUPS]
    y_ref = refs[3 * N_GROUPS + n_side]
    _side_cast(refs[3 * N_GROUPS:3 * N_GROUPS + n_side],
               refs[3 * N_GROUPS + n_side + 1:3 * N_GROUPS + 2 * n_side + 1])
    scratch = refs[3 * N_GROUPS + 2 * n_side + 1:]
    per_pair = len(scratch) // ATT_PAIRS_PER_STEP
    blk = ATT_BLOCK
    n_blocks = seq // blk
    merge_d = ATT_DILATIONS[1]
    assert all(d % merge_d == 0 for d in ATT_DILATIONS[1:])

    lane = lax.broadcasted_iota(jnp.int32, (blk, LANES), 1)
    first_head = lane < ATT_HEAD_DIM
    ones2 = jnp.ones((2 * blk, LANES), BF16)

    for pair in range(ATT_PAIRS_PER_STEP):
        lanes = slice(pair * LANES, (pair + 1) * LANES)
        scr = scratch[pair * per_pair:(pair + 1) * per_pair]
        o_scr, l_scr = scr[0:N_GROUPS], scr[N_GROUPS:2 * N_GROUPS]
        y_scr, bias_scr = scr[2 * N_GROUPS], scr[2 * N_GROUPS + 1]

        qi = lax.broadcasted_iota(jnp.int32, (blk, 2 * blk), 0)
        kj = lax.broadcasted_iota(jnp.int32, (blk, 2 * blk), 1)
        dist = qi + blk - kj
        in_window = (dist >= 0) & (dist <= blk)
        dist_f = dist.astype(F32)
        for gh in range(2 * N_GROUPS):
            alibi = -slope_ref[pair, gh:gh + 1, :] * dist_f
            bias_scr[gh] = jnp.where(in_window, alibi, NEG_INF)

        for g in range(N_GROUPS):
            d = ATT_DILATIONS[g]
            per_class = n_blocks // d
            q_ref, k_ref, v_ref = qkv_refs[3 * g:3 * g + 3]

            for idx in range(n_blocks):
                n, r = idx % per_class, idx // per_class
                rows = slice(idx * blk, (idx + 1) * blk)
                krows = rows if n == 0 else slice((idx - 1) * blk, (idx + 1) * blk)
                q = q_ref[0, rows, lanes]
                k = k_ref[0, krows, lanes]
                v_aug = jnp.concatenate([v_ref[0, krows, lanes], ones2[:k.shape[0]]], axis=1)
                pvs, ms = [], []
                for hh in range(2):
                    qm = jnp.where(first_head if hh == 0 else ~first_head, q, jnp.zeros_like(q))
                    if n == 0:
                        s = _dot_nt(qm, k) + bias_scr[2 * g + hh, :, blk:]
                        m = jnp.max(s, axis=-1, keepdims=True)
                    else:
                        s = _dot_nt(qm, k) + bias_scr[2 * g + hh]
                        m = jnp.max(jnp.maximum(s[:, :blk], s[:, blk:]), axis=-1, keepdims=True)
                    pvs.append(_dot(jnp.exp2(s - m).astype(BF16), v_aug))
                    ms.append(jnp.broadcast_to(m, (blk, LANES)))
                o = jnp.where(first_head, pvs[0][:, :LANES], pvs[1][:, :LANES])
                den = jnp.where(first_head, pvs[0][:, LANES:], pvs[1][:, LANES:])
                mx = jnp.where(first_head, ms[0], ms[1])
                if d == 1:
                    dst = rows
                else:
                    dst = pl.ds((r % merge_d) * (seq // merge_d) + n * blk * (d // merge_d) + r // merge_d,
                                blk, stride=d // merge_d)
                o_scr[g][dst, :] = o / den
                l_scr[g][dst, :] = mx + jnp.log2(den)

        rc = 256
        for c4 in range(merge_d):
            for j0 in range(0, seq // merge_d, rc):
                rows = slice(c4 * (seq // merge_d) + j0, c4 * (seq // merge_d) + j0 + rc)
                tokens = pl.ds(c4 + merge_d * j0, rc, stride=merge_d)
                ls = [l_scr[0][tokens, :]] + [l_scr[g][rows, :] for g in range(1, N_GROUPS)]
                os_ = [o_scr[0][tokens, :]] + [o_scr[g][rows, :] for g in range(1, N_GROUPS)]
                m = jnp.maximum(jnp.maximum(ls[0], ls[1]), ls[2])
                ws = [jnp.exp2(lg - m) for lg in ls]
                num = ws[0] * os_[0] + ws[1] * os_[1] + ws[2] * os_[2]
                y_scr[tokens, :] = num / (ws[0] + ws[1] + ws[2])
        for c in range(seq // rc):
            rows = slice(c * rc, (c + 1) * rc)
            y_ref[0, rows, lanes] = y_scr[rows, :].astype(BF16)


def _att(proj_nat, proj_perm, slopes, side_weights):
    b, seq, _ = proj_nat.shape
    per_step = ATT_PAIRS_PER_STEP * LANES
    per_tensor = ATT_WIDTH // per_step
    grid = (b, per_tensor)
    side_in, side_out, side_shape = _side_cast_specs(side_weights, grid)

    def qkv_spec(g, t):
        off = t * per_tensor if g == 0 else (3 * (g - 1) + t) * per_tensor
        return pl.BlockSpec((1, seq, per_step), lambda i, j, off=off: (i, 0, off + j))

    return pl.pallas_call(
        functools.partial(_att_kernel, seq=seq, n_side=len(side_weights)),
        grid=grid,
        in_specs=[pl.BlockSpec((ATT_PAIRS_PER_STEP, SUBLANES, 2 * ATT_BLOCK), lambda i, j: (j, 0, 0))]
        + [qkv_spec(g, t) for g in range(N_GROUPS) for t in range(3)] + side_in,
        out_specs=[pl.BlockSpec((1, seq, per_step), lambda i, j: (i, 0, j))] + side_out,
        out_shape=[jax.ShapeDtypeStruct((b, seq, ATT_WIDTH), BF16)] + side_shape,
        scratch_shapes=ATT_PAIRS_PER_STEP * ([pltpu.VMEM((seq, LANES), F32) for _ in range(2 * N_GROUPS + 1)]
                                             + [pltpu.VMEM((2 * N_GROUPS, ATT_BLOCK, 2 * ATT_BLOCK), F32)]),
        compiler_params=pltpu.CompilerParams(
            dimension_semantics=("arbitrary", "arbitrary"), vmem_limit_bytes=VMEM_LIMIT),
        name="att",
    )(slopes, *([proj_nat] * 3 + [proj_perm] * (3 * (N_GROUPS - 1))), *side_weights)


def _tail_kernel(h1_ref, yhg_ref, yatt_ref, ghg_ref, gatt_ref, wa_ref, wb_ref, wo_ref,
                 g2_ref, wgu_ref, wd_ref, gf_ref, o_ref):
    merged = (_sigmoid(ghg_ref[...]) * _dot(yhg_ref[...], wa_ref[...])
              + _sigmoid(gatt_ref[...]) * _dot(yatt_ref[...], wb_ref[...]))
    h2 = h1_ref[...] + _dot(merged.astype(BF16), wo_ref[...])
    xn = _rmsnorm(h2, g2_ref[...]).astype(BF16)
    h3 = h2 + 0.5 * _swiglu(xn, wgu_ref, wd_ref)
    o_ref[...] = _rmsnorm(h3, gf_ref[...])


def _tail(h1, y_hg, y_att, gates, wa, wb, wo, g2, wgu, wd, gf):
    t = h1.shape[0]
    row = lambda width, col=0: pl.BlockSpec((ROW_TILE, width), lambda i, col=col: (i, col))
    return pl.pallas_call(
        _tail_kernel,
        grid=(t // ROW_TILE,),
        in_specs=[
            row(D_MODEL), row(HG_WIDTH), row(ATT_WIDTH),
            row(D_MODEL, 0), row(D_MODEL, 1),
            _resident((HG_WIDTH, D_MODEL)), _resident((ATT_WIDTH, D_MODEL)), _resident((D_MODEL, D_MODEL)),
            _resident((1, D_MODEL)), _resident((D_MODEL, 2 * D_FF)), _resident((D_FF, D_MODEL)),
            _resident((1, D_MODEL)),
        ],
        out_specs=row(D_MODEL),
        out_shape=jax.ShapeDtypeStruct((t, D_MODEL), F32),
        compiler_params=pltpu.CompilerParams(
            dimension_semantics=("parallel",), vmem_limit_bytes=VMEM_LIMIT),
        name="tail",
    )(h1, y_hg, y_att, gates, gates, wa, wb, wo, g2, wgu, wd, gf)


def _alibi_table():
    n_heads = N_GROUPS * ATT_HEADS
    slopes = jnp.exp2(-ALIBI_MAX * jnp.arange(1, n_heads + 1, dtype=F32) / n_heads)
    slopes = slopes.reshape(N_GROUPS, ATT_HEADS // 2, 2) * jnp.asarray(ATT_DILATIONS, F32)[:, None, None] * LOG2_E
    table = jnp.zeros((ATT_HEADS // 2, SUBLANES, 2 * ATT_BLOCK), F32)
    table = table.at[:, :2 * N_GROUPS, :].set(
        jnp.broadcast_to(slopes.transpose(1, 0, 2).reshape(ATT_HEADS // 2, 2 * N_GROUPS, 1),
                         (ATT_HEADS // 2, 2 * N_GROUPS, 2 * ATT_BLOCK)))
    return table


def kernel(x, ffn1_norm, ffn1_w_gate_up, ffn1_w_down, mix_norm, w_in, hg_lower_bounds, hg_out_norm,
           w_branch_hg, w_branch_att, w_out, ffn2_norm, ffn2_w_gate_up, ffn2_w_down, final_norm):
    b, seq, d = x.shape
    assert d == D_MODEL and seq % (ATT_BLOCK * ATT_DILATIONS[-1]) == 0 and (b * seq) % ROW_TILE == 0
    x2d = x.reshape(b * seq, d)

    h1, w_all = _ffn1(x2d, ffn1_norm[0:1], ffn1_w_gate_up[0], ffn1_w_down[0], w_in[0])
    hg_q, hg_i, qkv0, hg_f, hg_og, gates, u_slab = _proj_nat(h1, mix_norm[0:1], w_all)
    proj_perm = _proj_perm(u_slab, w_all, b, seq)
    per_seq = lambda a: a.reshape(b, seq, -1)
    y_hg, wa, wb, wo = _hgrn2(per_seq(hg_q), per_seq(hg_i), per_seq(hg_f), per_seq(hg_og),
                              hg_lower_bounds.astype(F32), hg_out_norm[0:1],
                              [w_branch_hg[0], w_branch_att[0], w_out[0]])
    y_att, wgu2, wd2 = _att(per_seq(qkv0), proj_perm, _alibi_table(), [ffn2_w_gate_up[0], ffn2_w_down[0]])
    out = _tail(h1, y_hg.reshape(b * seq, -1), y_att.reshape(b * seq, -1), gates, wa, wb, wo,
                ffn2_norm[0:1], wgu2, wd2, final_norm.reshape(1, d))
    return out.reshape(b, seq, d)
```

```python
import functools

import jax
import jax.numpy as jnp
from jax import lax
from jax.experimental import pallas as pl
from jax.experimental.pallas import tpu as pltpu

F32 = jnp.float32
BF16 = jnp.bfloat16

D_MODEL = 1024
D_FF = 2816
HG_HEADS = 4
HG_DIM = 128
HG_WIDTH = HG_HEADS * HG_DIM
ATT_DILATIONS = (1, 4, 16)
ATT_BLOCK = 128
ATT_HEADS = 8
ATT_HEAD_DIM = 64
ATT_WIDTH = ATT_HEADS * ATT_HEAD_DIM
N_GROUPS = 3
ALIBI_MAX = 8.0
EPS = 1e-6
NEG_INF = -1e30
LOG2_E = 1.4426950408889634
Q_SCALE = ATT_HEAD_DIM ** -0.5 * LOG2_E

LANES = 128
SUBLANES = 8
MXU_DIM = 256
VMEM_LIMIT = 56 * 1024 * 1024

ROW_TILE = 512
COL_BLOCK = 512
NAT_BF16_COLS = 2 * HG_WIDTH + 3 * ATT_WIDTH
NAT_F32_COLS = 2 * HG_WIDTH + 2 * D_MODEL
HG_CHUNK = 128
HG_TILE = 1024
PERM_TILE = 1024
ATT_PAIRS_PER_STEP = 2


def _sigmoid(x):
    return 1.0 / (1.0 + jnp.exp(-x))


def _rmsnorm(x, gain):
    ms = jnp.mean(x * x, axis=-1, keepdims=True)
    return x * lax.rsqrt(ms + EPS) * gain


def _dot(a, b):
    return jnp.dot(a, b, preferred_element_type=F32)


def _dot_nt(a, b):
    return lax.dot_general(a, b, (((1,), (1,)), ((), ())), preferred_element_type=F32)


def _dot_tn(a, b):
    return lax.dot_general(a, b, (((0,), (0,)), ((), ())), preferred_element_type=F32)


def _swiglu(xn_bf16, wgu_ref, wd_ref):
    split = (D_FF // (2 * MXU_DIM) + 1) * MXU_DIM
    y = None
    for lo, hi in ((0, split), (split, D_FF)):
        a = _dot(xn_bf16, wgu_ref[:, lo:hi])
        b = _dot(xn_bf16, wgu_ref[:, D_FF + lo:D_FF + hi])
        act = (a * _sigmoid(a) * b).astype(BF16)
        part = _dot(act, wd_ref[lo:hi, :])
        y = part if y is None else y + part
    return y


W_IN_SRC_OF_DST = (0, 2, 4, 5, 6, 1, 3, 13, 14, 15, 16, 7, 8, 9, 10, 11, 12)
W_IN_Q_BLOCKS = (4, 7, 10)
N_NAT_BLOCKS = (NAT_BF16_COLS + NAT_F32_COLS) // COL_BLOCK
N_PERM_BLOCKS = len(W_IN_SRC_OF_DST) - N_NAT_BLOCKS


FFN1_LOAD_STEPS = 8


def _ffn1_kernel(x_ref, g_ref, wgu_f32_ref, wd_f32_ref, win_ref, o_ref, wall_ref, wgu_ref, wd_ref):
    step = pl.program_id(0)

    @pl.when(step < FFN1_LOAD_STEPS)
    def _load():
        for src, dst in ((wgu_f32_ref, wgu_ref), (wd_f32_ref, wd_ref)):
            n = src.shape[0]
            dst[pl.ds(pl.multiple_of(step * n, n), n), :] = src[...].astype(BF16)

    @pl.when(step >= FFN1_LOAD_STEPS)
    def _compute():
        for half in range(2):
            rows = slice(half * (ROW_TILE // 2), (half + 1) * (ROW_TILE // 2))
            x = x_ref[rows, :]
            xn = _rmsnorm(x, g_ref[...]).astype(BF16)
            o_ref[rows, :] = x + 0.5 * _swiglu(xn, wgu_ref, wd_ref)

        w = win_ref[...]
        wall_ref[...] = jnp.concatenate(
            [w[:, s * COL_BLOCK:(s + 1) * COL_BLOCK] * (Q_SCALE if s in W_IN_Q_BLOCKS else 1.0)
             for s in W_IN_SRC_OF_DST], axis=1).astype(BF16)


def _resident(shape):
    return pl.BlockSpec(shape, lambda *_: (0,) * len(shape), pipeline_mode=pl.Buffered(1))


BF16_ROWS = 2 * SUBLANES


def _side_cast_specs(weights, grid, first_step=0):
    n_steps = -first_step
    total = 1
    for g in grid:
        total *= g
    n_steps += total
    in_specs, out_specs, out_shape = [], [], []
    for w in weights:
        rows, cols = w.shape
        per_step = BF16_ROWS
        while rows % per_step or per_step * n_steps < rows:
            per_step += BF16_ROWS
        last = rows // per_step - 1

        def index(*ids, last=last):
            step = ids[0]
            for g, i in zip(grid[1:], ids[1:]):
                step = step * g + i
            return (jnp.clip(step - first_step, 0, last), 0)

        in_specs.append(pl.BlockSpec((per_step, cols), index))
        out_specs.append(pl.BlockSpec((per_step, cols), index))
        out_shape.append(jax.ShapeDtypeStruct((rows, cols), BF16))
    return in_specs, out_specs, out_shape


def _side_cast(in_refs, out_refs):
    for i_ref, o_ref in zip(in_refs, out_refs):
        o_ref[...] = i_ref[...].astype(BF16)


def _ffn1(x2d, gain, wgu, wd, w_in):
    t = x2d.shape[0]
    load = FFN1_LOAD_STEPS
    grid = (load + t // ROW_TILE,)
    side_in, side_out, side_shape = _side_cast_specs([w_in], grid, first_step=load)
    tile = lambda i: (jnp.maximum(i - load, 0), 0)
    piece = lambda i: (jnp.minimum(i, load - 1), 0)
    assert wgu.shape[0] % (load * BF16_ROWS) == 0 and wd.shape[0] % (load * BF16_ROWS) == 0
    return pl.pallas_call(
        _ffn1_kernel,
        grid=grid,
        in_specs=[
            pl.BlockSpec((ROW_TILE, D_MODEL), tile),
            _resident((1, D_MODEL)),
            pl.BlockSpec((wgu.shape[0] // load, wgu.shape[1]), piece),
            pl.BlockSpec((wd.shape[0] // load, wd.shape[1]), piece),
        ] + side_in,
        out_specs=[pl.BlockSpec((ROW_TILE, D_MODEL), tile)] + side_out,
        out_shape=[jax.ShapeDtypeStruct((t, D_MODEL), F32)] + side_shape,
        scratch_shapes=[pltpu.VMEM(wgu.shape, BF16), pltpu.VMEM(wd.shape, BF16)],
        compiler_params=pltpu.CompilerParams(
            dimension_semantics=("arbitrary",), vmem_limit_bytes=VMEM_LIMIT),
        name="ffn1",
    )(x2d, gain, wgu, wd, w_in)


NAT_OUTPUTS = ((1, BF16), (1, BF16), (3, BF16), (1, F32), (1, F32), (4, F32))


def _proj_nat_kernel(h_ref, g_ref, w_ref, *refs):
    out_refs, slab_ref = refs[:-1], refs[-1]
    for half in range(2):
        rows = slice(half * (ROW_TILE // 2), (half + 1) * (ROW_TILE // 2))
        u = _rmsnorm(h_ref[rows, :], g_ref[...])
        for s in range(D_MODEL // LANES):
            slab_ref[s, rows, :] = u[:, s * LANES:(s + 1) * LANES]
        ub = u.astype(BF16)
        c = 0
        for (n_blocks, dtype), o_ref in zip(NAT_OUTPUTS, out_refs):
            for j in range(n_blocks):
                o_ref[rows, j * COL_BLOCK:(j + 1) * COL_BLOCK] = _dot(
                    ub, w_ref[:, c * COL_BLOCK:(c + 1) * COL_BLOCK]).astype(dtype)
                c += 1


def _proj_nat(h1, gain, w_nat):
    t = h1.shape[0]
    n_slabs = D_MODEL // LANES
    return pl.pallas_call(
        _proj_nat_kernel,
        grid=(t // ROW_TILE,),
        in_specs=[
            pl.BlockSpec((ROW_TILE, D_MODEL), lambda i: (i, 0)),
            _resident((1, D_MODEL)),
            _resident((D_MODEL, NAT_BF16_COLS + NAT_F32_COLS)),
        ],
        out_specs=[pl.BlockSpec((ROW_TILE, n * COL_BLOCK), lambda i: (i, 0)) for n, _ in NAT_OUTPUTS]
        + [pl.BlockSpec((n_slabs, ROW_TILE, LANES), lambda i: (0, i, 0))],
        out_shape=[jax.ShapeDtypeStruct((t, n * COL_BLOCK), dtype) for n, dtype in NAT_OUTPUTS]
        + [jax.ShapeDtypeStruct((n_slabs, t, LANES), F32)],
        compiler_params=pltpu.CompilerParams(
            dimension_semantics=("parallel",), vmem_limit_bytes=VMEM_LIMIT),
        name="proj_nat",
    )(h1, gain, w_nat)


def _proj_perm_kernel(slab_ref, *refs, seq):
    w_refs, o_ref = refs[:-1], refs[-1]
    t = pl.program_id(1)
    n_slabs = D_MODEL // LANES
    rows_out = o_ref.shape[1]

    def gather(g):
        d = ATT_DILATIONS[g]
        run = seq // d
        per_tile = rows_out // run if run < rows_out else 1
        pieces = []
        for c in range(per_tile):
            if run >= rows_out:
                r = (t * rows_out) // run
                start = r + d * ((t * rows_out) % run)
                n_rows = rows_out
            else:
                start = t * per_tile + c
                n_rows = run
            pieces.append(jnp.concatenate(
                [slab_ref[s, pl.ds(start, n_rows, stride=d), :].astype(BF16) for s in range(n_slabs)], axis=1))
        return pieces[0] if len(pieces) == 1 else jnp.concatenate(pieces, axis=0)

    lhs = {g: gather(g) for g in (1, 2)}
    per_group = 3 * ATT_WIDTH // COL_BLOCK
    for g in (1, 2):
        for c in range(per_group):
            blk = (g - 1) * per_group + c
            o_ref[0, :, blk * COL_BLOCK:(blk + 1) * COL_BLOCK] = _dot(lhs[g], w_refs[blk][...]).astype(BF16)


def _proj_perm(u_slab, w_all, b, seq):
    n_slabs = D_MODEL // LANES
    n_cols = N_PERM_BLOCKS * COL_BLOCK
    return pl.pallas_call(
        functools.partial(_proj_perm_kernel, seq=seq),
        grid=(b, seq // PERM_TILE),
        in_specs=[pl.BlockSpec((n_slabs, seq, LANES), lambda i, j: (0, i, 0))]
        + [pl.BlockSpec((D_MODEL, COL_BLOCK), lambda i, j, c=c: (0, N_NAT_BLOCKS + c), pipeline_mode=pl.Buffered(1))
           for c in range(N_PERM_BLOCKS)],
        out_specs=pl.BlockSpec((1, PERM_TILE, n_cols), lambda i, j: (i, j, 0)),
        out_shape=jax.ShapeDtypeStruct((b, seq, n_cols), BF16),
        compiler_params=pltpu.CompilerParams(
            dimension_semantics=("parallel", "arbitrary"), vmem_limit_bytes=VMEM_LIMIT),
        name="proj_perm",
    )(u_slab, *([w_all] * N_PERM_BLOCKS))


def _bcast_rows(x, block, row):
    n = x.shape[0] // block
    parts = [jnp.broadcast_to(x[i * block + row:i * block + row + 1, :], (block, x.shape[1]))
             for i in range(n)]
    return parts[0] if n == 1 else jnp.concatenate(parts, axis=0)


def _hgrn2_kernel(q_ref, i_ref, f_ref, og_ref, lbp_ref, gain_ref, *refs):
    n_side = (len(refs) - 2) // 2
    o_ref, st_ref = refs[n_side], refs[-1]
    _side_cast(refs[:n_side], refs[n_side + 1:-1])
    c_len = HG_CHUNK

    @pl.when(pl.program_id(1) == 0)
    def _():
        st_ref[...] = jnp.zeros_like(st_ref)

    lbp = lbp_ref[...]
    e = jnp.exp(lbp - jnp.max(lbp, axis=0, keepdims=True))
    lb_all = e[0:1, :] / jnp.sum(e, axis=0, keepdims=True)
    gain_all = gain_ref[...]

    t_idx = lax.broadcasted_iota(jnp.int32, (c_len, c_len), 0)
    s_idx = lax.broadcasted_iota(jnp.int32, (c_len, c_len), 1)
    halves = [1 << i for i in range(c_len.bit_length() - 1)]
    level_masks = [((t_idx >> b.bit_length()) == (s_idx >> b.bit_length()))
                   & ((t_idx & b) != 0) & ((s_idx & b) == 0) for b in halves]
    diag_mask = t_idx == s_idx
    ones_rhs = jnp.ones((HG_DIM, c_len), BF16)
    sub = jnp.concatenate([lax.broadcasted_iota(jnp.int32, (SUBLANES, HG_WIDTH), 0)] * (c_len // SUBLANES), axis=0)
    pair = sub & 6

    def chunk(c):
        rows = slice(c * c_len, (c + 1) * c_len)
        q = q_ref[0, rows, :].astype(F32)
        v = i_ref[0, rows, :]
        f = lb_all + (1.0 - lb_all) * _sigmoid(f_ref[0, rows, :])
        k = 1.0 - f

        operands = [(q * f, k)]
        odd = (sub & 1) == 1
        pq = f * jnp.where(odd, pltpu.roll(f, 1, axis=0), 1.0)
        sk = jnp.where(odd, 1.0, pltpu.roll(f, c_len - 1, axis=0))
        operands.append((q * pq, k * sk))
        r1, r3, r5, r7 = (_bcast_rows(pq, SUBLANES, r) for r in (1, 3, 5, 7))
        sk = sk * jnp.where(pair == 0, r3, jnp.where(pair == 4, r7, 1.0))
        pq = pq * jnp.where(pair == 2, r1, jnp.where(pair == 6, r5, 1.0))
        operands.append((q * pq, k * sk))
        r3, r7 = _bcast_rows(pq, SUBLANES, 3), _bcast_rows(pq, SUBLANES, 7)
        sk = sk * jnp.where(sub < 4, r7, 1.0)
        pq = pq * jnp.where(sub >= 4, r3, 1.0)
        b = SUBLANES
        while b < c_len:
            n = c_len // b
            tot = _bcast_rows(pq, b, b - 1)
            zero = jnp.zeros((b, HG_WIDTH), F32)

            def blk(x, i, b=b):
                return x[i * b:(i + 1) * b]

            operands.append((
                jnp.concatenate([zero if i % 2 == 0 else blk(q, i) * blk(pq, i) for i in range(n)], axis=0),
                jnp.concatenate([blk(k, i) * blk(sk, i) if i % 2 == 0 else zero for i in range(n)], axis=0)))
            pq, sk = (
                jnp.concatenate([blk(pq, i) if i % 2 == 0 else blk(pq, i) * blk(tot, i - 1)
                                 for i in range(n)], axis=0),
                jnp.concatenate([blk(sk, i) * blk(tot, i + 1) if i % 2 == 0 else blk(sk, i)
                                 for i in range(n)], axis=0))
            b *= 2
        operands = [(a.astype(BF16), kk.astype(BF16)) for a, kk in operands]
        qk = (q * k).astype(BF16)
        qg = (q * pq).astype(BF16)
        kg = (k * sk).astype(BF16)
        chunk_decay = pq[c_len - 1:c_len, :]

        outs = []
        for h in range(HG_HEADS):
            cols = slice(h * HG_DIM, (h + 1) * HG_DIM)
            a = jnp.where(diag_mask, _dot(qk[:, cols], ones_rhs), 0.0)
            for (ql, kl), mask in zip(operands, level_masks):
                a = jnp.where(mask, _dot_nt(ql[:, cols], kl[:, cols]), a)
            st = st_ref[h]
            o = _dot(a.astype(BF16), v[:, cols]) + _dot_nt(qg[:, cols], st.astype(BF16))
            st_ref[h] = st * chunk_decay[:, cols] + _dot_tn(v[:, cols], kg[:, cols])
            outs.append(o * lax.rsqrt(jnp.mean(o * o, axis=-1, keepdims=True) + EPS))
        og = og_ref[0, rows, :]
        o_ref[0, rows, :] = (jnp.concatenate(outs, axis=1) * gain_all * (og * _sigmoid(og))).astype(BF16)

    for c in range(HG_TILE // c_len):
        chunk(c)


def _hgrn2(hg_q, hg_i, hg_f, hg_og, lower_bounds, out_gain, side_weights):
    b, seq, _ = hg_q.shape
    blk = (1, HG_TILE, HG_WIDTH)
    grid = (b, seq // HG_TILE)
    side_in, side_out, side_shape = _side_cast_specs(side_weights, grid)
    return pl.pallas_call(
        _hgrn2_kernel,
        grid=grid,
        in_specs=[
            pl.BlockSpec(blk, lambda i, j: (i, j, 0)),
            pl.BlockSpec(blk, lambda i, j: (i, j, 0)),
            pl.BlockSpec(blk, lambda i, j: (i, j, 0)),
            pl.BlockSpec(blk, lambda i, j: (i, j, 0)),
            _resident(lower_bounds.shape),
            _resident((1, HG_WIDTH)),
        ] + side_in,
        out_specs=[pl.BlockSpec(blk, lambda i, j: (i, j, 0))] + side_out,
        out_shape=[jax.ShapeDtypeStruct((b, seq, HG_WIDTH), BF16)] + side_shape,
        scratch_shapes=[pltpu.VMEM((HG_HEADS, HG_DIM, HG_DIM), F32)],
        compiler_params=pltpu.CompilerParams(
            dimension_semantics=("arbitrary", "arbitrary"), vmem_limit_bytes=VMEM_LIMIT),
        name="hgrn2",
    )(hg_q, hg_i, hg_f, hg_og, lower_bounds, out_gain, *side_weights)


def _att_kernel(slope_ref, *refs, seq, n_side):
    qkv_refs = refs[:3 * N_GROUPS]
    y_ref = refs[3 * N_GROUPS + n_side]
    _side_cast(refs[3 * N_GROUPS:3 * N_GROUPS + n_side],
               refs[3 * N_GROUPS + n_side + 1:3 * N_GROUPS + 2 * n_side + 1])
    scratch = refs[3 * N_GROUPS + 2 * n_side + 1:]
    per_pair = len(scratch) // ATT_PAIRS_PER_STEP
    blk = ATT_BLOCK
    n_blocks = seq // blk
    merge_d = ATT_DILATIONS[1]
    assert all(d % merge_d == 0 for d in ATT_DILATIONS[1:])

    lane = lax.broadcasted_iota(jnp.int32, (blk, LANES), 1)
    first_head = lane < ATT_HEAD_DIM
    ones2 = jnp.ones((2 * blk, LANES), BF16)

    for pair in range(ATT_PAIRS_PER_STEP):
        lanes = slice(pair * LANES, (pair + 1) * LANES)
        scr = scratch[pair * per_pair:(pair + 1) * per_pair]
        o_scr, l_scr = scr[0:N_GROUPS], scr[N_GROUPS:2 * N_GROUPS]
        y_scr, bias_scr = scr[2 * N_GROUPS], scr[2 * N_GROUPS + 1]

        qi = lax.broadcasted_iota(jnp.int32, (blk, 2 * blk), 0)
        kj = lax.broadcasted_iota(jnp.int32, (blk, 2 * blk), 1)
        dist = qi + blk - kj
        in_window = (dist >= 0) & (dist <= blk)
        dist_f = dist.astype(F32)
        for gh in range(2 * N_GROUPS):
            alibi = -slope_ref[pair, gh:gh + 1, :] * dist_f
            bias_scr[gh] = jnp.where(in_window, alibi, NEG_INF)

        for g in range(N_GROUPS):
            d = ATT_DILATIONS[g]
            per_class = n_blocks // d
            q_ref, k_ref, v_ref = qkv_refs[3 * g:3 * g + 3]

            for idx in range(n_blocks):
                n, r = idx % per_class, idx // per_class
                rows = slice(idx * blk, (idx + 1) * blk)
                krows = rows if n == 0 else slice((idx - 1) * blk, (idx + 1) * blk)
                q = q_ref[0, rows, lanes]
                k = k_ref[0, krows, lanes]
                v_aug = jnp.concatenate([v_ref[0, krows, lanes], ones2[:k.shape[0]]], axis=1)
                pvs, ms = [], []
                for hh in range(2):
                    qm = jnp.where(first_head if hh == 0 else ~first_head, q, jnp.zeros_like(q))
                    if n == 0:
                        s = _dot_nt(qm, k) + bias_scr[2 * g + hh, :, blk:]
                        m = jnp.max(s, axis=-1, keepdims=True)
                    else:
                        s = _dot_nt(qm, k) + bias_scr[2 * g + hh]
                        m = jnp.max(jnp.maximum(s[:, :blk], s[:, blk:]), axis=-1, keepdims=True)
                    pvs.append(_dot(jnp.exp2(s - m).astype(BF16), v_aug))
                    ms.append(jnp.broadcast_to(m, (blk, LANES)))
                o = jnp.where(first_head, pvs[0][:, :LANES], pvs[1][:, :LANES])
                den = jnp.where(first_head, pvs[0][:, LANES:], pvs[1][:, LANES:])
                mx = jnp.where(first_head, ms[0], ms[1])
                if d == 1:
                    dst = rows
                else:
                    dst = pl.ds((r % merge_d) * (seq // merge_d) + n * blk * (d // merge_d) + r // merge_d,
                                blk, stride=d // merge_d)
                o_scr[g][dst, :] = o / den
                l_scr[g][dst, :] = mx + jnp.log2(den)

        rc = 256
        for c4 in range(merge_d):
            for j0 in range(0, seq // merge_d, rc):
                rows = slice(c4 * (seq // merge_d) + j0, c4 * (seq // merge_d) + j0 + rc)
                tokens = pl.ds(c4 + merge_d * j0, rc, stride=merge_d)
                ls = [l_scr[0][tokens, :]] + [l_scr[g][rows, :] for g in range(1, N_GROUPS)]
                os_ = [o_scr[0][tokens, :]] + [o_scr[g][rows, :] for g in range(1, N_GROUPS)]
                m = jnp.maximum(jnp.maximum(ls[0], ls[1]), ls[2])
                ws = [jnp.exp2(lg - m) for lg in ls]
                num = ws[0] * os_[0] + ws[1] * os_[1] + ws[2] * os_[2]
                y_scr[tokens, :] = num / (ws[0] + ws[1] + ws[2])
        for c in range(seq // rc):
            rows = slice(c * rc, (c + 1) * rc)
            y_ref[0, rows, lanes] = y_scr[rows, :].astype(BF16)


def _att(proj_nat, proj_perm, slopes, side_weights):
    b, seq, _ = proj_nat.shape
    per_step = ATT_PAIRS_PER_STEP * LANES
    per_tensor = ATT_WIDTH // per_step
    grid = (b, per_tensor)
    side_in, side_out, side_shape = _side_cast_specs(side_weights, grid)

    def qkv_spec(g, t):
        off = t * per_tensor if g == 0 else (3 * (g - 1) + t) * per_tensor
        return pl.BlockSpec((1, seq, per_step), lambda i, j, off=off: (i, 0, off + j))

    return pl.pallas_call(
        functools.partial(_att_kernel, seq=seq, n_side=len(side_weights)),
        grid=grid,
        in_specs=[pl.BlockSpec((ATT_PAIRS_PER_STEP, SUBLANES, 2 * ATT_BLOCK), lambda i, j: (j, 0, 0))]
        + [qkv_spec(g, t) for g in range(N_GROUPS) for t in range(3)] + side_in,
        out_specs=[pl.BlockSpec((1, seq, per_step), lambda i, j: (i, 0, j))] + side_out,
        out_shape=[jax.ShapeDtypeStruct((b, seq, ATT_WIDTH), BF16)] + side_shape,
        scratch_shapes=ATT_PAIRS_PER_STEP * ([pltpu.VMEM((seq, LANES), F32) for _ in range(2 * N_GROUPS + 1)]
                                             + [pltpu.VMEM((2 * N_GROUPS, ATT_BLOCK, 2 * ATT_BLOCK), F32)]),
        compiler_params=pltpu.CompilerParams(
            dimension_semantics=("arbitrary", "arbitrary"), vmem_limit_bytes=VMEM_LIMIT),
        name="att",
    )(slopes, *([proj_nat] * 3 + [proj_perm] * (3 * (N_GROUPS - 1))), *side_weights)


def _tail_kernel(h1_ref, yhg_ref, yatt_ref, ghg_ref, gatt_ref, wa_ref, wb_ref, wo_ref,
                 g2_ref, wgu_ref, wd_ref, gf_ref, o_ref):
    merged = (_sigmoid(ghg_ref[...]) * _dot(yhg_ref[...], wa_ref[...])
              + _sigmoid(gatt_ref[...]) * _dot(yatt_ref[...], wb_ref[...]))
    h2 = h1_ref[...] + _dot(merged.astype(BF16), wo_ref[...])
    xn = _rmsnorm(h2, g2_ref[...]).astype(BF16)
    h3 = h2 + 0.5 * _swiglu(xn, wgu_ref, wd_ref)
    o_ref[...] = _rmsnorm(h3, gf_ref[...])


def _tail(h1, y_hg, y_att, gates, wa, wb, wo, g2, wgu, wd, gf):
    t = h1.shape[0]
    row = lambda width, col=0: pl.BlockSpec((ROW_TILE, width), lambda i, col=col: (i, col))
    return pl.pallas_call(
        _tail_kernel,
        grid=(t // ROW_TILE,),
        in_specs=[
            row(D_MODEL), row(HG_WIDTH), row(ATT_WIDTH),
            row(D_MODEL, 0), row(D_MODEL, 1),
            _resident((HG_WIDTH, D_MODEL)), _resident((ATT_WIDTH, D_MODEL)), _resident((D_MODEL, D_MODEL)),
            _resident((1, D_MODEL)), _resident((D_MODEL, 2 * D_FF)), _resident((D_FF, D_MODEL)),
            _resident((1, D_MODEL)),
        ],
        out_specs=row(D_MODEL),
        out_shape=jax.ShapeDtypeStruct((t, D_MODEL), F32),
        compiler_params=pltpu.CompilerParams(
            dimension_semantics=("parallel",), vmem_limit_bytes=VMEM_LIMIT),
        name="tail",
    )(h1, y_hg, y_att, gates, gates, wa, wb, wo, g2, wgu, wd, gf)


def _alibi_table():
    n_heads = N_GROUPS * ATT_HEADS
    slopes = jnp.exp2(-ALIBI_MAX * jnp.arange(1, n_heads + 1, dtype=F32) / n_heads)
    slopes = slopes.reshape(N_GROUPS, ATT_HEADS // 2, 2) * jnp.asarray(ATT_DILATIONS, F32)[:, None, None] * LOG2_E
    table = jnp.zeros((ATT_HEADS // 2, SUBLANES, 2 * ATT_BLOCK), F32)
    table = table.at[:, :2 * N_GROUPS, :].set(
        jnp.broadcast_to(slopes.transpose(1, 0, 2).reshape(ATT_HEADS // 2, 2 * N_GROUPS, 1),
                         (ATT_HEADS // 2, 2 * N_GROUPS, 2 * ATT_BLOCK)))
    return table


def kernel(x, ffn1_norm, ffn1_w_gate_up, ffn1_w_down, mix_norm, w_in, hg_lower_bounds, hg_out_norm,
           w_branch_hg, w_branch_att, w_out, ffn2_norm, ffn2_w_gate_up, ffn2_w_down, final_norm):
    b, seq, d = x.shape
    assert d == D_MODEL and seq % (ATT_BLOCK * ATT_DILATIONS[-1]) == 0 and (b * seq) % ROW_TILE == 0
    x2d = x.reshape(b * seq, d)

    h1, w_all = _ffn1(x2d, ffn1_norm[0:1], ffn1_w_gate_up[0], ffn1_w_down[0], w_in[0])
    hg_q, hg_i, qkv0, hg_f, hg_og, gates, u_slab = _proj_nat(h1, mix_norm[0:1], w_all)
    proj_perm = _proj_perm(u_slab, w_all, b, seq)
    per_seq = lambda a: a.reshape(b, seq, -1)
    y_hg, wa, wb, wo = _hgrn2(per_seq(hg_q), per_seq(hg_i), per_seq(hg_f), per_seq(hg_og),
                              hg_lower_bounds.astype(F32), hg_out_norm[0:1],
                              [w_branch_hg[0], w_branch_att[0], w_out[0]])
    y_att, wgu2, wd2 = _att(per_seq(qkv0), proj_perm, _alibi_table(), [ffn2_w_gate_up[0], ffn2_w_down[0]])
    out = _tail(h1, y_hg.reshape(b * seq, -1), y_att.reshape(b * seq, -1), gates, wa, wb, wo,
                ffn2_norm[0:1], wgu2, wd2, final_norm.reshape(1, d))
    return out.reshape(b, seq, d)
```

```python
import functools

import jax
import jax.numpy as jnp
from jax import lax
from jax.experimental import pallas as pl
from jax.experimental.pallas import tpu as pltpu

F32 = jnp.float32
BF16 = jnp.bfloat16

D_MODEL = 1024
D_FF = 2816
HG_HEADS = 4
HG_DIM = 128
HG_WIDTH = HG_HEADS * HG_DIM
ATT_DILATIONS = (1, 4, 16)
ATT_BLOCK = 128
ATT_HEADS = 8
ATT_HEAD_DIM = 64
ATT_WIDTH = ATT_HEADS * ATT_HEAD_DIM
N_GROUPS = 3
ALIBI_MAX = 8.0
EPS = 1e-6
NEG_INF = -1e30
LOG2_E = 1.4426950408889634
Q_SCALE = ATT_HEAD_DIM ** -0.5 * LOG2_E

LANES = 128
SUBLANES = 8
MXU_DIM = 256
VMEM_LIMIT = 56 * 1024 * 1024

ROW_TILE = 512
COL_BLOCK = 512
NAT_BF16_COLS = 2 * HG_WIDTH + 3 * ATT_WIDTH
NAT_F32_COLS = 2 * HG_WIDTH + 2 * D_MODEL
HG_CHUNK = 128
HG_TILE = 1024
PERM_TILE = 1024
ATT_PAIRS_PER_STEP = 2


def _sigmoid(x):
    return 1.0 / (1.0 + jnp.exp(-x))


def _rmsnorm(x, gain):
    ms = jnp.mean(x * x, axis=-1, keepdims=True)
    return x * lax.rsqrt(ms + EPS) * gain


def _dot(a, b):
    return jnp.dot(a, b, preferred_element_type=F32)


def _dot_nt(a, b):
    return lax.dot_general(a, b, (((1,), (1,)), ((), ())), preferred_element_type=F32)


def _dot_tn(a, b):
    return lax.dot_general(a, b, (((0,), (0,)), ((), ())), preferred_element_type=F32)


def _swiglu(xn_bf16, wgu_ref, wd_ref):
    split = (D_FF // (2 * MXU_DIM) + 1) * MXU_DIM
    y = None
    for lo, hi in ((0, split), (split, D_FF)):
        a = _dot(xn_bf16, wgu_ref[:, lo:hi])
        b = _dot(xn_bf16, wgu_ref[:, D_FF + lo:D_FF + hi])
        act = (a * _sigmoid(a) * b).astype(BF16)
        part = _dot(act, wd_ref[lo:hi, :])
        y = part if y is None else y + part
    return y


W_IN_SRC_OF_DST = (0, 2, 4, 5, 6, 1, 3, 13, 14, 15, 16, 7, 8, 9, 10, 11, 12)
W_IN_Q_BLOCKS = (4, 7, 10)
N_NAT_BLOCKS = (NAT_BF16_COLS + NAT_F32_COLS) // COL_BLOCK
N_PERM_BLOCKS = len(W_IN_SRC_OF_DST) - N_NAT_BLOCKS


FFN1_LOAD_STEPS = 8


def _ffn1_kernel(x_ref, g_ref, wgu_f32_ref, wd_f32_ref, win_ref, o_ref, wall_ref, wgu_ref, wd_ref):
    step = pl.program_id(0)

    @pl.when(step < FFN1_LOAD_STEPS)
    def _load():
        for src, dst in ((wgu_f32_ref, wgu_ref), (wd_f32_ref, wd_ref)):
            n = src.shape[0]
            dst[pl.ds(pl.multiple_of(step * n, n), n), :] = src[...].astype(BF16)

    @pl.when(step >= FFN1_LOAD_STEPS)
    def _compute():
        for half in range(2):
            rows = slice(half * (ROW_TILE // 2), (half + 1) * (ROW_TILE // 2))
            x = x_ref[rows, :]
            xn = _rmsnorm(x, g_ref[...]).astype(BF16)
            o_ref[rows, :] = x + 0.5 * _swiglu(xn, wgu_ref, wd_ref)

        w = win_ref[...]
        wall_ref[...] = jnp.concatenate(
            [w[:, s * COL_BLOCK:(s + 1) * COL_BLOCK] * (Q_SCALE if s in W_IN_Q_BLOCKS else 1.0)
             for s in W_IN_SRC_OF_DST], axis=1).astype(BF16)


def _resident(shape):
    return pl.BlockSpec(shape, lambda *_: (0,) * len(shape), pipeline_mode=pl.Buffered(1))


BF16_ROWS = 2 * SUBLANES


def _side_cast_specs(weights, grid, first_step=0):
    n_steps = -first_step
    total = 1
    for g in grid:
        total *= g
    n_steps += total
    in_specs, out_specs, out_shape = [], [], []
    for w in weights:
        rows, cols = w.shape
        per_step = BF16_ROWS
        while rows % per_step or per_step * n_steps < rows:
            per_step += BF16_ROWS
        last = rows // per_step - 1

        def index(*ids, last=last):
            step = ids[0]
            for g, i in zip(grid[1:], ids[1:]):
                step = step * g + i
            return (jnp.clip(step - first_step, 0, last), 0)

        in_specs.append(pl.BlockSpec((per_step, cols), index))
        out_specs.append(pl.BlockSpec((per_step, cols), index))
        out_shape.append(jax.ShapeDtypeStruct((rows, cols), BF16))
    return in_specs, out_specs, out_shape


def _side_cast(in_refs, out_refs):
    for i_ref, o_ref in zip(in_refs, out_refs):
        o_ref[...] = i_ref[...].astype(BF16)


def _ffn1(x2d, gain, wgu, wd, w_in):
    t = x2d.shape[0]
    load = FFN1_LOAD_STEPS
    grid = (load + t // ROW_TILE,)
    side_in, side_out, side_shape = _side_cast_specs([w_in], grid, first_step=load)
    tile = lambda i: (jnp.maximum(i - load, 0), 0)
    piece = lambda i: (jnp.minimum(i, load - 1), 0)
    assert wgu.shape[0] % (load * BF16_ROWS) == 0 and wd.shape[0] % (load * BF16_ROWS) == 0
    return pl.pallas_call(
        _ffn1_kernel,
        grid=grid,
        in_specs=[
            pl.BlockSpec((ROW_TILE, D_MODEL), tile),
            _resident((1, D_MODEL)),
            pl.BlockSpec((wgu.shape[0] // load, wgu.shape[1]), piece),
            pl.BlockSpec((wd.shape[0] // load, wd.shape[1]), piece),
        ] + side_in,
        out_specs=[pl.BlockSpec((ROW_TILE, D_MODEL), tile)] + side_out,
        out_shape=[jax.ShapeDtypeStruct((t, D_MODEL), F32)] + side_shape,
        scratch_shapes=[pltpu.VMEM(wgu.shape, BF16), pltpu.VMEM(wd.shape, BF16)],
        compiler_params=pltpu.CompilerParams(
            dimension_semantics=("arbitrary",), vmem_limit_bytes=VMEM_LIMIT),
        name="ffn1",
    )(x2d, gain, wgu, wd, w_in)


NAT_OUTPUTS = ((1, BF16), (1, BF16), (3, BF16), (1, F32), (1, F32), (4, F32))


def _proj_nat_kernel(h_ref, g_ref, w_ref, *refs):
    out_refs, slab_ref = refs[:-1], refs[-1]
    for half in range(2):
        rows = slice(half * (ROW_TILE // 2), (half + 1) * (ROW_TILE // 2))
        u = _rmsnorm(h_ref[rows, :], g_ref[...])
        for s in range(D_MODEL // LANES):
            slab_ref[s, rows, :] = u[:, s * LANES:(s + 1) * LANES]
        ub = u.astype(BF16)
        c = 0
        for (n_blocks, dtype), o_ref in zip(NAT_OUTPUTS, out_refs):
            for j in range(n_blocks):
                o_ref[rows, j * COL_BLOCK:(j + 1) * COL_BLOCK] = _dot(
                    ub, w_ref[:, c * COL_BLOCK:(c + 1) * COL_BLOCK]).astype(dtype)
                c += 1


def _proj_nat(h1, gain, w_nat):
    t = h1.shape[0]
    n_slabs = D_MODEL // LANES
    return pl.pallas_call(
        _proj_nat_kernel,
        grid=(t // ROW_TILE,),
        in_specs=[
            pl.BlockSpec((ROW_TILE, D_MODEL), lambda i: (i, 0)),
            _resident((1, D_MODEL)),
            _resident((D_MODEL, NAT_BF16_COLS + NAT_F32_COLS)),
        ],
        out_specs=[pl.BlockSpec((ROW_TILE, n * COL_BLOCK), lambda i: (i, 0)) for n, _ in NAT_OUTPUTS]
        + [pl.BlockSpec((n_slabs, ROW_TILE, LANES), lambda i: (0, i, 0))],
        out_shape=[jax.ShapeDtypeStruct((t, n * COL_BLOCK), dtype) for n, dtype in NAT_OUTPUTS]
        + [jax.ShapeDtypeStruct((n_slabs, t, LANES), F32)],
        compiler_params=pltpu.CompilerParams(
            dimension_semantics=("parallel",), vmem_limit_bytes=VMEM_LIMIT),
        name="proj_nat",
    )(h1, gain, w_nat)


def _proj_perm_kernel(slab_ref, *refs, seq):
    w_refs, o_ref = refs[:-1], refs[-1]
    t = pl.program_id(1)
    n_slabs = D_MODEL // LANES
    rows_out = o_ref.shape[1]

    def gather(g):
        d = ATT_DILATIONS[g]
        run = seq // d
        per_tile = rows_out // run if run < rows_out else 1
        pieces = []
        for c in range(per_tile):
            if run >= rows_out:
                r = (t * rows_out) // run
                start = r + d * ((t * rows_out) % run)
                n_rows = rows_out
            else:
                start = t * per_tile + c
                n_rows = run
            pieces.append(jnp.concatenate(
                [slab_ref[s, pl.ds(start, n_rows, stride=d), :].astype(BF16) for s in range(n_slabs)], axis=1))
        return pieces[0] if len(pieces) == 1 else jnp.concatenate(pieces, axis=0)

    lhs = {g: gather(g) for g in (1, 2)}
    per_group = 3 * ATT_WIDTH // COL_BLOCK
    for g in (1, 2):
        for c in range(per_group):
            blk = (g - 1) * per_group + c
            o_ref[0, :, blk * COL_BLOCK:(blk + 1) * COL_BLOCK] = _dot(lhs[g], w_refs[blk][...]).astype(BF16)


def _proj_perm(u_slab, w_all, b, seq):
    n_slabs = D_MODEL // LANES
    n_cols = N_PERM_BLOCKS * COL_BLOCK
    return pl.pallas_call(
        functools.partial(_proj_perm_kernel, seq=seq),
        grid=(b, seq // PERM_TILE),
        in_specs=[pl.BlockSpec((n_slabs, seq, LANES), lambda i, j: (0, i, 0))]
        + [pl.BlockSpec((D_MODEL, COL_BLOCK), lambda i, j, c=c: (0, N_NAT_BLOCKS + c), pipeline_mode=pl.Buffered(1))
           for c in range(N_PERM_BLOCKS)],
        out_specs=pl.BlockSpec((1, PERM_TILE, n_cols), lambda i, j: (i, j, 0)),
        out_shape=jax.ShapeDtypeStruct((b, seq, n_cols), BF16),
        compiler_params=pltpu.CompilerParams(
            dimension_semantics=("parallel", "arbitrary"), vmem_limit_bytes=VMEM_LIMIT),
        name="proj_perm",
    )(u_slab, *([w_all] * N_PERM_BLOCKS))


def _bcast_rows(x, block, row):
    n = x.shape[0] // block
    parts = [jnp.broadcast_to(x[i * block + row:i * block + row + 1, :], (block, x.shape[1]))
             for i in range(n)]
    return parts[0] if n == 1 else jnp.concatenate(parts, axis=0)


def _hgrn2_kernel(q_ref, i_ref, f_ref, og_ref, lbp_ref, gain_ref, *refs):
    n_side = (len(refs) - 2) // 2
    o_ref, st_ref = refs[n_side], refs[-1]
    _side_cast(refs[:n_side], refs[n_side + 1:-1])
    c_len = HG_CHUNK

    @pl.when(pl.program_id(1) == 0)
    def _():
        st_ref[...] = jnp.zeros_like(st_ref)

    lbp = lbp_ref[...]
    e = jnp.exp(lbp - jnp.max(lbp, axis=0, keepdims=True))
    lb_all = e[0:1, :] / jnp.sum(e, axis=0, keepdims=True)
    gain_all = gain_ref[...]

    t_idx = lax.broadcasted_iota(jnp.int32, (c_len, c_len), 0)
    s_idx = lax.broadcasted_iota(jnp.int32, (c_len, c_len), 1)
    halves = [1 << i for i in range(c_len.bit_length() - 1)]
    level_masks = [((t_idx >> b.bit_length()) == (s_idx >> b.bit_length()))
                   & ((t_idx & b) != 0) & ((s_idx & b) == 0) for b in halves]
    diag_mask = t_idx == s_idx
    ones_rhs = jnp.ones((HG_DIM, c_len), BF16)
    sub = jnp.concatenate([lax.broadcasted_iota(jnp.int32, (SUBLANES, HG_WIDTH), 0)] * (c_len // SUBLANES), axis=0)
    pair = sub & 6

    def chunk(c):
        rows = slice(c * c_len, (c + 1) * c_len)
        q = q_ref[0, rows, :].astype(F32)
        v = i_ref[0, rows, :]
        f = lb_all + (1.0 - lb_all) * _sigmoid(f_ref[0, rows, :])
        k = 1.0 - f

        operands = [(q * f, k)]
        odd = (sub & 1) == 1
        pq = f * jnp.where(odd, pltpu.roll(f, 1, axis=0), 1.0)
        sk = jnp.where(odd, 1.0, pltpu.roll(f, c_len - 1, axis=0))
        operands.append((q * pq, k * sk))
        r1, r3, r5, r7 = (_bcast_rows(pq, SUBLANES, r) for r in (1, 3, 5, 7))
        sk = sk * jnp.where(pair == 0, r3, jnp.where(pair == 4, r7, 1.0))
        pq = pq * jnp.where(pair == 2, r1, jnp.where(pair == 6, r5, 1.0))
        operands.append((q * pq, k * sk))
        r3, r7 = _bcast_rows(pq, SUBLANES, 3), _bcast_rows(pq, SUBLANES, 7)
        sk = sk * jnp.where(sub < 4, r7, 1.0)
        pq = pq * jnp.where(sub >= 4, r3, 1.0)
        b = SUBLANES
        while b < c_len:
            n = c_len // b
            tot = _bcast_rows(pq, b, b - 1)
            zero = jnp.zeros((b, HG_WIDTH), F32)

            def blk(x, i, b=b):
                return x[i * b:(i + 1) * b]

            operands.append((
                jnp.concatenate([zero if i % 2 == 0 else blk(q, i) * blk(pq, i) for i in range(n)], axis=0),
                jnp.concatenate([blk(k, i) * blk(sk, i) if i % 2 == 0 else zero for i in range(n)], axis=0)))
            pq, sk = (
                jnp.concatenate([blk(pq, i) if i % 2 == 0 else blk(pq, i) * blk(tot, i - 1)
                                 for i in range(n)], axis=0),
                jnp.concatenate([blk(sk, i) * blk(tot, i + 1) if i % 2 == 0 else blk(sk, i)
                                 for i in range(n)], axis=0))
            b *= 2
        operands = [(a.astype(BF16), kk.astype(BF16)) for a, kk in operands]
        qk = (q * k).astype(BF16)
        qg = (q * pq).astype(BF16)
        kg = (k * sk).astype(BF16)
        chunk_decay = pq[c_len - 1:c_len, :]

        outs = []
        for h in range(HG_HEADS):
            cols = slice(h * HG_DIM, (h + 1) * HG_DIM)
            a = jnp.where(diag_mask, _dot(qk[:, cols], ones_rhs), 0.0)
            for (ql, kl), mask in zip(operands, level_masks):
                a = jnp.where(mask, _dot_nt(ql[:, cols], kl[:, cols]), a)
            st = st_ref[h]
            o = _dot(a.astype(BF16), v[:, cols]) + _dot_nt(qg[:, cols], st.astype(BF16))
            st_ref[h] = st * chunk_decay[:, cols] + _dot_tn(v[:, cols], kg[:, cols])
            outs.append(o * lax.rsqrt(jnp.mean(o * o, axis=-1, keepdims=True) + EPS))
        og = og_ref[0, rows, :]
        o_ref[0, rows, :] = (jnp.concatenate(outs, axis=1) * gain_all * (og * _sigmoid(og))).astype(BF16)

    for c in range(HG_TILE // c_len):
        chunk(c)


def _hgrn2(hg_q, hg_i, hg_f, hg_og, lower_bounds, out_gain, side_weights):
    b, seq, _ = hg_q.shape
    blk = (1, HG_TILE, HG_WIDTH)
    grid = (b, seq // HG_TILE)
    side_in, side_out, side_shape = _side_cast_specs(side_weights, grid)
    return pl.pallas_call(
        _hgrn2_kernel,
        grid=grid,
        in_specs=[
            pl.BlockSpec(blk, lambda i, j: (i, j, 0)),
            pl.BlockSpec(blk, lambda i, j: (i, j, 0)),
            pl.BlockSpec(blk, lambda i, j: (i, j, 0)),
            pl.BlockSpec(blk, lambda i, j: (i, j, 0)),
            _resident(lower_bounds.shape),
            _resident((1, HG_WIDTH)),
        ] + side_in,
        out_specs=[pl.BlockSpec(blk, lambda i, j: (i, j, 0))] + side_out,
        out_shape=[jax.ShapeDtypeStruct((b, seq, HG_WIDTH), BF16)] + side_shape,
        scratch_shapes=[pltpu.VMEM((HG_HEADS, HG_DIM, HG_DIM), F32)],
        compiler_params=pltpu.CompilerParams(
            dimension_semantics=("arbitrary", "arbitrary"), vmem_limit_bytes=VMEM_LIMIT),
        name="hgrn2",
    )(hg_q, hg_i, hg_f, hg_og, lower_bounds, out_gain, *side_weights)


def _att_kernel(slope_ref, *refs, seq, n_side):
    qkv_refs = refs[:3 * N_GROUPS]
    y_ref = refs[3 * N_GROUPS + n_side]
    _side_cast(refs[3 * N_GROUPS:3 * N_GROUPS + n_side],
               refs[3 * N_GROUPS + n_side + 1:3 * N_GROUPS + 2 * n_side + 1])
    scratch = refs[3 * N_GROUPS + 2 * n_side + 1:]
    per_pair = len(scratch) // ATT_PAIRS_PER_STEP
    blk = ATT_BLOCK
    n_blocks = seq // blk
    merge_d = ATT_DILATIONS[1]
    assert all(d % merge_d == 0 for d in ATT_DILATIONS[1:])

    lane = lax.broadcasted_iota(jnp.int32, (blk, LANES), 1)
    first_head = lane < ATT_HEAD_DIM
    ones2 = jnp.ones((2 * blk, LANES), BF16)

    for pair in range(ATT_PAIRS_PER_STEP):
        lanes = slice(pair * LANES, (pair + 1) * LANES)
        scr = scratch[pair * per_pair:(pair + 1) * per_pair]
        o_scr, l_scr = scr[0:N_GROUPS], scr[N_GROUPS:2 * N_GROUPS]
        y_scr, bias_scr = scr[2 * N_GROUPS], scr[2 * N_GROUPS + 1]

        qi = lax.broadcasted_iota(jnp.int32, (blk, 2 * blk), 0)
        kj = lax.broadcasted_iota(jnp.int32, (blk, 2 * blk), 1)
        dist = qi + blk - kj
        in_window = (dist >= 0) & (dist <= blk)
        dist_f = dist.astype(F32)
        for gh in range(2 * N_GROUPS):
            alibi = -slope_ref[pair, gh:gh + 1, :] * dist_f
            bias_scr[gh] = jnp.where(in_window, alibi, NEG_INF)

        for g in range(N_GROUPS):
            d = ATT_DILATIONS[g]
            per_class = n_blocks // d
            q_ref, k_ref, v_ref = qkv_refs[3 * g:3 * g + 3]

            for idx in range(n_blocks):
                n, r = idx % per_class, idx // per_class
                rows = slice(idx * blk, (idx + 1) * blk)
                krows = rows if n == 0 else slice((idx - 1) * blk, (idx + 1) * blk)
                q = q_ref[0, rows, lanes]
                k = k_ref[0, krows, lanes]
                v_aug = jnp.concatenate([v_ref[0, krows, lanes], ones2[:k.shape[0]]], axis=1)
                pvs, ms = [], []
                for hh in range(2):
                    qm = jnp.where(first_head if hh == 0 else ~first_head, q, jnp.zeros_like(q))
                    if n == 0:
                        s = _dot_nt(qm, k) + bias_scr[2 * g + hh, :, blk:]
                        m = jnp.max(s, axis=-1, keepdims=True)
                    else:
                        s = _dot_nt(qm, k) + bias_scr[2 * g + hh]
                        m = jnp.max(jnp.maximum(s[:, :blk], s[:, blk:]), axis=-1, keepdims=True)
                    pvs.append(_dot(jnp.exp2(s - m).astype(BF16), v_aug))
                    ms.append(jnp.broadcast_to(m, (blk, LANES)))
                o = jnp.where(first_head, pvs[0][:, :LANES], pvs[1][:, :LANES])
                den = jnp.where(first_head, pvs[0][:, LANES:], pvs[1][:, LANES:])
                mx = jnp.where(first_head, ms[0], ms[1])
                if d == 1:
                    dst = rows
                else:
                    dst = pl.ds((r % merge_d) * (seq // merge_d) + n * blk * (d // merge_d) + r // merge_d,
                                blk, stride=d // merge_d)
                o_scr[g][dst, :] = o / den
                l_scr[g][dst, :] = mx + jnp.log2(den)

        rc = 256
        for c4 in range(merge_d):
            for j0 in range(0, seq // merge_d, rc):
                rows = slice(c4 * (seq // merge_d) + j0, c4 * (seq // merge_d) + j0 + rc)
                tokens = pl.ds(c4 + merge_d * j0, rc, stride=merge_d)
                ls = [l_scr[0][tokens, :]] + [l_scr[g][rows, :] for g in range(1, N_GROUPS)]
                os_ = [o_scr[0][tokens, :]] + [o_scr[g][rows, :] for g in range(1, N_GROUPS)]
                m = jnp.maximum(jnp.maximum(ls[0], ls[1]), ls[2])
                ws = [jnp.exp2(lg - m) for lg in ls]
                num = ws[0] * os_[0] + ws[1] * os_[1] + ws[2] * os_[2]
                y_scr[tokens, :] = num / (ws[0] + ws[1] + ws[2])
        for c in range(seq // rc):
            rows = slice(c * rc, (c + 1) * rc)
            y_ref[0, rows, lanes] = y_scr[rows, :].astype(BF16)


def _att(proj_nat, proj_perm, slopes, side_weights):
    b, seq, _ = proj_nat.shape
    per_step = ATT_PAIRS_PER_STEP * LANES
    per_tensor = ATT_WIDTH // per_step
    grid = (b, per_tensor)
    side_in, side_out, side_shape = _side_cast_specs(side_weights, grid)

    def qkv_spec(g, t):
        off = t * per_tensor if g == 0 else (3 * (g - 1) + t) * per_tensor
        return pl.BlockSpec((1, seq, per_step), lambda i, j, off=off: (i, 0, off + j))

    return pl.pallas_call(
        functools.partial(_att_kernel, seq=seq, n_side=len(side_weights)),
        grid=grid,
        in_specs=[pl.BlockSpec((ATT_PAIRS_PER_STEP, SUBLANES, 2 * ATT_BLOCK), lambda i, j: (j, 0, 0))]
        + [qkv_spec(g, t) for g in range(N_GROUPS) for t in range(3)] + side_in,
        out_specs=[pl.BlockSpec((1, seq, per_step), lambda i, j: (i, 0, j))] + side_out,
        out_shape=[jax.ShapeDtypeStruct((b, seq, ATT_WIDTH), BF16)] + side_shape,
        scratch_shapes=ATT_PAIRS_PER_STEP * ([pltpu.VMEM((seq, LANES), F32) for _ in range(2 * N_GROUPS + 1)]
                                             + [pltpu.VMEM((2 * N_GROUPS, ATT_BLOCK, 2 * ATT_BLOCK), F32)]),
        compiler_params=pltpu.CompilerParams(
            dimension_semantics=("arbitrary", "arbitrary"), vmem_limit_bytes=VMEM_LIMIT),
        name="att",
    )(slopes, *([proj_nat] * 3 + [proj_perm] * (3 * (N_GROUPS - 1))), *side_weights)


def _tail_kernel(h1_ref, yhg_ref, yatt_ref, ghg_ref, gatt_ref, wa_ref, wb_ref, wo_ref,
                 g2_ref, wgu_ref, wd_ref, gf_ref, o_ref):
    merged = (_sigmoid(ghg_ref[...]) * _dot(yhg_ref[...], wa_ref[...])
              + _sigmoid(gatt_ref[...]) * _dot(yatt_ref[...], wb_ref[...]))
    h2 = h1_ref[...] + _dot(merged.astype(BF16), wo_ref[...])
    for half in range(2):
        rows = slice(half * (ROW_TILE // 2), (half + 1) * (ROW_TILE // 2))
        xn = _rmsnorm(h2[rows], g2_ref[...]).astype(BF16)
        h3 = h2[rows] + 0.5 * _swiglu(xn, wgu_ref, wd_ref)
        o_ref[rows, :] = _rmsnorm(h3, gf_ref[...])


def _tail(h1, y_hg, y_att, gates, wa, wb, wo, g2, wgu, wd, gf):
    t = h1.shape[0]
    row = lambda width, col=0: pl.BlockSpec((ROW_TILE, width), lambda i, col=col: (i, col))
    return pl.pallas_call(
        _tail_kernel,
        grid=(t // ROW_TILE,),
        in_specs=[
            row(D_MODEL), row(HG_WIDTH), row(ATT_WIDTH),
            row(D_MODEL, 0), row(D_MODEL, 1),
            _resident((HG_WIDTH, D_MODEL)), _resident((ATT_WIDTH, D_MODEL)), _resident((D_MODEL, D_MODEL)),
            _resident((1, D_MODEL)), _resident((D_MODEL, 2 * D_FF)), _resident((D_FF, D_MODEL)),
            _resident((1, D_MODEL)),
        ],
        out_specs=row(D_MODEL),
        out_shape=jax.ShapeDtypeStruct((t, D_MODEL), F32),
        compiler_params=pltpu.CompilerParams(
            dimension_semantics=("parallel",), vmem_limit_bytes=VMEM_LIMIT),
        name="tail",
    )(h1, y_hg, y_att, gates, gates, wa, wb, wo, g2, wgu, wd, gf)


def _alibi_table():
    n_heads = N_GROUPS * ATT_HEADS
    slopes = jnp.exp2(-ALIBI_MAX * jnp.arange(1, n_heads + 1, dtype=F32) / n_heads)
    slopes = slopes.reshape(N_GROUPS, ATT_HEADS // 2, 2) * jnp.asarray(ATT_DILATIONS, F32)[:, None, None] * LOG2_E
    table = jnp.zeros((ATT_HEADS // 2, SUBLANES, 2 * ATT_BLOCK), F32)
    table = table.at[:, :2 * N_GROUPS, :].set(
        jnp.broadcast_to(slopes.transpose(1, 0, 2).reshape(ATT_HEADS // 2, 2 * N_GROUPS, 1),
                         (ATT_HEADS // 2, 2 * N_GROUPS, 2 * ATT_BLOCK)))
    return table


def kernel(x, ffn1_norm, ffn1_w_gate_up, ffn1_w_down, mix_norm, w_in, hg_lower_bounds, hg_out_norm,
           w_branch_hg, w_branch_att, w_out, ffn2_norm, ffn2_w_gate_up, ffn2_w_down, final_norm):
    b, seq, d = x.shape
    assert d == D_MODEL and seq % (ATT_BLOCK * ATT_DILATIONS[-1]) == 0 and (b * seq) % ROW_TILE == 0
    x2d = x.reshape(b * seq, d)

    h1, w_all = _ffn1(x2d, ffn1_norm[0:1], ffn1_w_gate_up[0], ffn1_w_down[0], w_in[0])
    hg_q, hg_i, qkv0, hg_f, hg_og, gates, u_slab = _proj_nat(h1, mix_norm[0:1], w_all)
    proj_perm = _proj_perm(u_slab, w_all, b, seq)
    per_seq = lambda a: a.reshape(b, seq, -1)
    y_hg, wa, wb, wo = _hgrn2(per_seq(hg_q), per_seq(hg_i), per_seq(hg_f), per_seq(hg_og),
                              hg_lower_bounds.astype(F32), hg_out_norm[0:1],
                              [w_branch_hg[0], w_branch_att[0], w_out[0]])
    y_att, wgu2, wd2 = _att(per_seq(qkv0), proj_perm, _alibi_table(), [ffn2_w_gate_up[0], ffn2_w_down[0]])
    out = _tail(h1, y_hg.reshape(b * seq, -1), y_att.reshape(b * seq, -1), gates, wa, wb, wo,
                ffn2_norm[0:1], wgu2, wd2, final_norm.reshape(1, d))
    return out.reshape(b, seq, d)
```
